```python
import math
import jax, jax.numpy as jnp
from jax import lax
import numpy as np


D_MODEL = 1024
BATCH = 2
SEQ = 8192
DEPTH = 4

GRID_W = 64
CTX_LEN = 256
MIX_W = D_MODEL // 2
N_BRANCH = 4
CHUNK = 128
GMLP_GROUPS = 4
GMLP_GD = MIX_W // GMLP_GROUPS
CONV_W = 31
MLSTM_HEADS = 4
MLSTM_DH = MIX_W // MLSTM_HEADS
QK_CONV = 3
POOL_WINDOWS = (2, 4, 8, 16)
POOL_GD = MIX_W // len(POOL_WINDOWS)
D_FF = 128 * ((8 * D_MODEL // 3 + 127) // 128)
ALPHA = (2 * DEPTH) ** 0.25
BETA = (8 * DEPTH) ** -0.25
LN_EPS = 1e-6
FFN_RES = 0.5

OFF_A = 0
OFF_B = OFF_A + 2 * MIX_W
OFF_C = OFF_B + 2 * MIX_W
OFF_C_GATES = OFF_C + 3 * MIX_W
OFF_C_O = OFF_C_GATES + 4 * MLSTM_HEADS
OFF_D = OFF_C_O + MIX_W
OFF_G = OFF_D + MIX_W
IN_COLS = OFF_G + N_BRANCH * D_MODEL

kernel_name = "hybrid_gated_branch_dit_trunk"


def _norm_f32(x):
    xf = x.astype(jnp.float32)
    mu = jnp.mean(xf, axis=-1, keepdims=True)
    var = jnp.mean(jnp.square(xf - mu), axis=-1, keepdims=True)
    return (xf - mu) * lax.rsqrt(var + LN_EPS)


def layer_norm(x, g, b):
    return (_norm_f32(x) * g + b).astype(x.dtype)


def modulate(h, mods, s):
    return h * (1 + mods[..., s, 1, :]) + mods[..., s, 0, :]


def residual(h, y, mods, s, g, b, r):
    return layer_norm(ALPHA * h + r * mods[..., s, 2, :] * y, g, b)


def pos_emb_2d(rows):
    quarter = D_MODEL // 4
    t = jnp.arange(rows * GRID_W)
    r = (t // GRID_W).astype(jnp.float32)
    col = (t % GRID_W).astype(jnp.float32)
    freqs = jnp.exp(-math.log(10000.0) * jnp.arange(quarter, dtype=jnp.float32) / quarter)
    er = r[:, None] * freqs
    ec = col[:, None] * freqs
    return jnp.concatenate([jnp.sin(er), jnp.cos(er), jnp.sin(ec), jnp.cos(ec)], axis=-1)


def swiglu_ffn(h, w_in, w_out):
    a = h @ w_in
    return (jax.nn.silu(a[..., :D_FF]) * a[..., D_FF:]) @ w_out


def depthwise_conv(x, w):
    k = w.shape[0]
    return lax.conv_general_dilated(
        x, w[:, None, :], window_strides=(1,), padding=[(k // 2, k - 1 - k // 2)],
        dimension_numbers=('NWC', 'WIO', 'NWC'), feature_group_count=x.shape[-1])


def gmlp_mixer(p, ln_g, ln_b, ws, bs):
    B, N, _ = p.shape
    a = jax.nn.gelu(p)
    u, v = a[..., :MIX_W], a[..., MIX_W:]
    v = layer_norm(v, ln_g, ln_b).reshape(B, N // CHUNK, CHUNK, GMLP_GROUPS, GMLP_GD)
    z = jnp.einsum('gts,bcsgd->bctgd', ws, v) + bs.T[:, :, None]
    return u * z.reshape(B, N, MIX_W)


def conv_module(p, w, b, ln_g, ln_b):
    a = p[..., :MIX_W] * jax.nn.sigmoid(p[..., MIX_W:])
    a = depthwise_conv(a, w) + b
    return jax.nn.silu(layer_norm(a, ln_g, ln_b))


def _heads(t):
    B, N, _ = t.shape
    return t.astype(jnp.float32).reshape(B, N, MLSTM_HEADS, MLSTM_DH).transpose(0, 2, 1, 3)


def _gates(pg):
    g = pg.astype(jnp.float32).reshape(pg.shape[0], pg.shape[1], 4, MLSTM_HEADS).transpose(2, 0, 3, 1)
    return g[0], jax.nn.log_sigmoid(g[1]), g[2], jax.nn.log_sigmoid(g[3])


def _chunked(t):
    B, H, N = t.shape[:3]
    return jnp.moveaxis(t.reshape((B, H, N // CHUNK, CHUNK) + t.shape[3:]), 2, 0)


def _unchunk(t):
    t = jnp.moveaxis(t, 0, 2)
    return t.reshape(t.shape[:2] + (t.shape[2] * t.shape[3],) + t.shape[4:])


def _flip(t):
    return jnp.flip(t, axis=2)


def zero_state(b):
    return (jnp.zeros((b, MLSTM_HEADS, MLSTM_DH, MLSTM_DH), jnp.float32),
            jnp.zeros((b, MLSTM_HEADS, MLSTM_DH), jnp.float32),
            jnp.zeros((b, MLSTM_HEADS), jnp.float32))


def _state_update(state, k, v, li, b):
    C, n, m = state
    b_end = b[..., -1]
    logw = b_end[..., None] - b + li
    m_new = jnp.maximum(b_end + m, jnp.max(logw, axis=-1))
    w = jnp.exp(logw - m_new[..., None])
    decay = jnp.exp(b_end + m - m_new)
    C_new = decay[..., None, None] * C + jnp.einsum('bhs,bhsv,bhsk->bhvk', w, v, k)
    n_new = decay[..., None] * n + jnp.einsum('bhs,bhsk->bhk', w, k)
    return (C_new, n_new, m_new)


def _chunk_step(state, xs):
    q, k, v, li, lf = xs
    C, n, m = state
    b = jnp.cumsum(lf, axis=-1)
    seen = jnp.tril(jnp.ones((CHUNK, CHUNK), dtype=bool))
    d = jnp.where(seen, b[..., :, None] - b[..., None, :] + li[..., None, :], -jnp.inf)
    inter = b + m[..., None]
    m_t = jnp.maximum(inter, jnp.max(d, axis=-1))
    s = jnp.einsum('bhtk,bhsk->bhts', q, k) * jnp.exp(d - m_t[..., None])
    w_inter = jnp.exp(inter - m_t)
    num = jnp.einsum('bhts,bhsv->bhtv', s, v) + w_inter[..., None] * jnp.einsum('bhvk,bhtk->bhtv', C, q)
    den = jnp.sum(s, axis=-1) + w_inter * jnp.einsum('bhk,bhtk->bht', n, q)
    h = num / jnp.maximum(jnp.abs(den), jnp.exp(-m_t))[..., None]
    return _state_update(state, k, v, li, b), h


def _state_step(state, xs):
    k, v, li, lf = xs
    return _state_update(state, k, v, li, jnp.cumsum(lf, axis=-1)), None


def mlstm_scan(q, k, v, li, lf, state):
    state, h = lax.scan(_chunk_step, state, tuple(_chunked(t) for t in (q, k, v, li, lf)))
    return _unchunk(h), state


def mlstm_final_state(k, v, li, lf, state):
    state, _ = lax.scan(_state_step, state, tuple(_chunked(t) for t in (k, v, li, lf)))
    return state


def _mlstm_k(pk, w_k):
    return _heads(jax.nn.silu(depthwise_conv(pk, w_k))) * MLSTM_DH ** -0.5


def mlstm_mixer(pc, qk_conv_w, ln_g, st_f, st_b):
    B, N, _ = pc.shape
    q = _heads(jax.nn.silu(depthwise_conv(pc[..., :MIX_W], qk_conv_w[:, :MIX_W])))
    k = _mlstm_k(pc[..., MIX_W:2 * MIX_W], qk_conv_w[:, MIX_W:])
    v = _heads(pc[..., 2 * MIX_W:3 * MIX_W])
    li_f, lf_f, li_b, lf_b = _gates(pc[..., 3 * MIX_W:3 * MIX_W + 4 * MLSTM_HEADS])
    o = pc[..., 3 * MIX_W + 4 * MLSTM_HEADS:]
    h_f, st_f = mlstm_scan(q, k, v, li_f, lf_f, st_f)
    h_b, st_b = mlstm_scan(_flip(q), _flip(k), _flip(v), _flip(li_b), _flip(lf_b), st_b)
    h = _norm_f32(h_f + _flip(h_b)).transpose(0, 2, 1, 3).reshape(B, N, MIX_W)
    return jax.nn.sigmoid(o) * (h * ln_g).astype(pc.dtype), st_f, st_b


def mlstm_context_states(h_ctx, w_kvg, b_kvg, w_k):
    pc = h_ctx @ w_kvg + b_kvg
    k = _mlstm_k(pc[..., :MIX_W], w_k)
    v = _heads(pc[..., MIX_W:2 * MIX_W])
    li_f, lf_f, li_b, lf_b = _gates(pc[..., 2 * MIX_W:])
    st0 = zero_state(h_ctx.shape[0])
    return (mlstm_final_state(k, v, li_f, lf_f, st0),
            mlstm_final_state(_flip(k), _flip(v), _flip(li_b), _flip(lf_b), st0))


def pool_mixer(p, w, scale):
    B, N, _ = p.shape
    pf = p.astype(jnp.float32)
    cs = jnp.concatenate([jnp.zeros((B, 1, MIX_W), jnp.float32), jnp.cumsum(pf, axis=1)], axis=1)
    t = jnp.arange(N)
    outs = []
    for g, win in enumerate(POOL_WINDOWS):
        lo, hi = win // 2, win - 1 - win // 2
        start = jnp.clip(t - lo, 0, N)
        stop = jnp.clip(t + hi + 1, 0, N)
        sl = slice(g * POOL_GD, (g + 1) * POOL_GD)
        mean = (cs[:, stop, sl] - cs[:, start, sl]) / (stop - start).astype(jnp.float32)[:, None]
        outs.append(mean - pf[..., sl])
    d = jnp.stack(outs, axis=2)
    y = jnp.einsum('bngd,gde->bnge', d, w.astype(jnp.float32)).reshape(B, N, MIX_W)
    return (y * scale).astype(p.dtype)


def gated_merge(branches, pg, w_branch, w_out):
    B, N, _ = pg.shape
    ys = jnp.stack(branches, axis=2)
    proj = jnp.einsum('bnim,imd->bnid', ys, w_branch)
    gate = jax.nn.sigmoid(pg.reshape(B, N, N_BRANCH, D_MODEL))
    return jnp.sum(gate * proj, axis=2) @ w_out


def token_mixer(h, lw, st_f, st_b):
    p = h @ lw['w_in'] + lw['b_in']
    y_a = gmlp_mixer(p[..., OFF_A:OFF_B], lw['gmlp_ln_g'], lw['gmlp_ln_b'], lw['gmlp_ws'], lw['gmlp_bs'])
    y_b = conv_module(p[..., OFF_B:OFF_C], lw['conv_w'], lw['conv_b'], lw['conv_ln_g'], lw['conv_ln_b'])
    y_c, st_f, st_b = mlstm_mixer(p[..., OFF_C:OFF_D], lw['qk_conv_w'], lw['mlstm_ln_g'], st_f, st_b)
    y_d = pool_mixer(p[..., OFF_D:OFF_G], lw['pool_w'], lw['pool_scale'])
    y = gated_merge([y_a, y_b, y_c, y_d], p[..., OFF_G:], lw['w_branch'], lw['w_out'])
    return y, st_f, st_b


def setup_inputs(seed: int = 0) -> dict:
    key = jax.random.key(seed)
    ks = jax.random.split(key, 32)

    def nrm(k, shape, s):
        return jax.random.normal(k, shape, jnp.float32) * s

    d = D_MODEL
    b_in = nrm(ks[11], (DEPTH, IN_COLS), 0.02)
    f_bias = jnp.linspace(3.0, 6.0, MLSTM_HEADS, dtype=jnp.float32) + nrm(ks[12], (DEPTH, 2, MLSTM_HEADS), 0.1)
    fo = OFF_C_GATES
    b_in = b_in.at[:, fo + MLSTM_HEADS:fo + 2 * MLSTM_HEADS].set(f_bias[:, 0])
    b_in = b_in.at[:, fo + 3 * MLSTM_HEADS:fo + 4 * MLSTM_HEADS].set(f_bias[:, 1])
    return {
        'x': nrm(ks[0], (BATCH, SEQ, d), 1.0),
        'c': nrm(ks[1], (BATCH, d), 1.0),
        'ctx': nrm(ks[2], (BATCH, CTX_LEN, d), 1.0),
        'c_ctx': nrm(ks[3], (d,), 1.0),
        'w_ada': nrm(ks[4], (DEPTH, d, 9 * d), 0.5 * d ** -0.5),
        'b_ada': nrm(ks[5], (DEPTH, 9 * d), 0.02),
        'ln_g': 1.0 + nrm(ks[6], (DEPTH, 3, d), 0.05),
        'ln_b': nrm(ks[7], (DEPTH, 3, d), 0.02),
        'ffn_w_in': nrm(ks[8], (DEPTH, 2, d, 2 * D_FF), d ** -0.5),
        'ffn_w_out': nrm(ks[9], (DEPTH, 2, D_FF, d), BETA * D_FF ** -0.5),
        'w_in': nrm(ks[10], (DEPTH, d, IN_COLS), d ** -0.5),
        'b_in': b_in,
        'gmlp_ln_g': 1.0 + nrm(ks[13], (DEPTH, MIX_W), 0.05),
        'gmlp_ln_b': nrm(ks[14], (DEPTH, MIX_W), 0.02),
        'gmlp_ws': nrm(ks[15], (DEPTH, GMLP_GROUPS, CHUNK, CHUNK), CHUNK ** -0.5),
        'gmlp_bs': 1.0 + nrm(ks[16], (DEPTH, GMLP_GROUPS, CHUNK), 0.05),
        'conv_w': nrm(ks[17], (DEPTH, CONV_W, MIX_W), CONV_W ** -0.5),
        'conv_b': nrm(ks[18], (DEPTH, MIX_W), 0.02),
        'conv_ln_g': 1.0 + nrm(ks[19], (DEPTH, MIX_W), 0.05),
        'conv_ln_b': nrm(ks[20], (DEPTH, MIX_W), 0.02),
        'qk_conv_w': nrm(ks[21], (DEPTH, QK_CONV, 2 * MIX_W), QK_CONV ** -0.5),
        'mlstm_ln_g': 1.0 + nrm(ks[22], (DEPTH, MIX_W), 0.05),
        'pool_w': nrm(ks[23], (DEPTH, len(POOL_WINDOWS), POOL_GD, POOL_GD), POOL_GD ** -0.5),
        'pool_scale': 1.0 + nrm(ks[24], (DEPTH, MIX_W), 0.05),
        'w_branch': nrm(ks[25], (DEPTH, N_BRANCH, MIX_W, d), MIX_W ** -0.5),
        'w_out': nrm(ks[26], (DEPTH, d, d), BETA * d ** -0.5),
    }


def reference(x, c, ctx, c_ctx, w_ada, b_ada, ln_g, ln_b, ffn_w_in, ffn_w_out, w_in, b_in,
              gmlp_ln_g, gmlp_ln_b, gmlp_ws, gmlp_bs, conv_w, conv_b, conv_ln_g, conv_ln_b,
              qk_conv_w, mlstm_ln_g, pool_w, pool_scale, w_branch, w_out):
    n_lat = x.shape[1]
    rows = n_lat // GRID_W
    x = x + pos_emb_2d(rows).astype(x.dtype)[None]
    h_ctx = ctx
    for l in range(DEPTH):
        last = l == DEPTH - 1
        lw = {'w_in': w_in[l], 'b_in': b_in[l], 'gmlp_ln_g': gmlp_ln_g[l], 'gmlp_ln_b': gmlp_ln_b[l],
              'gmlp_ws': gmlp_ws[l], 'gmlp_bs': gmlp_bs[l], 'conv_w': conv_w[l], 'conv_b': conv_b[l],
              'conv_ln_g': conv_ln_g[l], 'conv_ln_b': conv_ln_b[l], 'qk_conv_w': qk_conv_w[l],
              'mlstm_ln_g': mlstm_ln_g[l], 'pool_w': pool_w[l], 'pool_scale': pool_scale[l],
              'w_branch': w_branch[l], 'w_out': w_out[l]}
        mod_x = (jax.nn.silu(c) @ w_ada[l] + b_ada[l]).reshape(c.shape[0], 1, 3, 3, D_MODEL)
        mod_c = (jax.nn.silu(c_ctx) @ w_ada[l] + b_ada[l]).reshape(3, 3, D_MODEL)

        x = residual(x, swiglu_ffn(modulate(x, mod_x, 0), ffn_w_in[l, 0], ffn_w_out[l, 0]),
                     mod_x, 0, ln_g[l, 0], ln_b[l, 0], FFN_RES)
        h_ctx = residual(h_ctx, swiglu_ffn(modulate(h_ctx, mod_c, 0), ffn_w_in[l, 0], ffn_w_out[l, 0]),
                         mod_c, 0, ln_g[l, 0], ln_b[l, 0], FFN_RES)

        cm = modulate(h_ctx, mod_c, 1)
        if last:
            st_f, st_b = mlstm_context_states(cm, w_in[l, :, OFF_C + MIX_W:OFF_C_O],
                                              b_in[l, OFF_C + MIX_W:OFF_C_O], qk_conv_w[l, :, MIX_W:])
        else:
            st0 = zero_state(h_ctx.shape[0])
            y_ctx, st_f, st_b = token_mixer(cm, lw, st0, st0)
            h_ctx = residual(h_ctx, y_ctx, mod_c, 1, ln_g[l, 1], ln_b[l, 1], 1.0)
        y_x, _, _ = token_mixer(modulate(x, mod_x, 1), lw, st_f, st_b)
        x = residual(x, y_x, mod_x, 1, ln_g[l, 1], ln_b[l, 1], 1.0)

        x = residual(x, swiglu_ffn(modulate(x, mod_x, 2), ffn_w_in[l, 1], ffn_w_out[l, 1]),
                     mod_x, 2, ln_g[l, 2], ln_b[l, 2], FFN_RES)
        if not last:
            h_ctx = residual(h_ctx, swiglu_ffn(modulate(h_ctx, mod_c, 2), ffn_w_in[l, 1], ffn_w_out[l, 1]),
                             mod_c, 2, ln_g[l, 2], ln_b[l, 2], FFN_RES)
    return x
```

```python
import functools
import math

import jax
import jax.numpy as jnp
from jax import lax
from jax.experimental import pallas as pl
from jax.experimental.pallas import tpu as pltpu

D_MODEL = 1024
BATCH = 2
SEQ = 8192
DEPTH = 4
GRID_W = 64
CTX_LEN = 256
MIX_W = D_MODEL // 2
N_BRANCH = 4
CHUNK = 128
GMLP_GROUPS = 4
CONV_W = 31
MLSTM_HEADS = 4
MLSTM_DH = MIX_W // MLSTM_HEADS
QK_CONV = 3
POOL_WINDOWS = (2, 4, 8, 16)
POOL_GD = MIX_W // len(POOL_WINDOWS)
D_FF = 128 * ((8 * D_MODEL // 3 + 127) // 128)
ALPHA = (2 * DEPTH) ** 0.25
LN_EPS = 1e-6
FFN_RES = 0.5

OFF_A = 0
OFF_B = OFF_A + 2 * MIX_W
OFF_C = OFF_B + 2 * MIX_W
OFF_C_GATES = OFF_C + 3 * MIX_W
OFF_C_O = OFF_C_GATES + 4 * MLSTM_HEADS
OFF_D = OFF_C_O + MIX_W
OFF_G = OFF_D + MIX_W
IN_COLS = OFF_G + N_BRANCH * D_MODEL

LANES = 128
SUBLANES = 8
BF16_ROWS = 16
TM = 256
SEQ_B = CTX_LEN + SEQ
T_ALL = BATCH * SEQ_B
NT_B = SEQ_B // TM
NT = BATCH * NT_B
NCH_B = SEQ_B // CHUNK
CH_T = TM // CHUNK
CTX_CH = CTX_LEN // CHUNK
HALO = 16
FF_CH = 256
GATE_PAD = LANES
MOD_ROWS = 8
CTX_MOD_ROW = BATCH
VMEM_LIMIT = 56 * 1024 * 1024

MX_A = 0
MX_B = MX_A + 2 * MIX_W
MX_O = MX_B + 2 * MIX_W
MX_D = MX_O + MIX_W
MX_G = MX_D + MIX_W
MX_COLS = MX_G + N_BRANCH * D_MODEL
CP_QK = 0
CP_V = 2 * MIX_W
CP_G = 3 * MIX_W
CP_COLS = CP_G + GATE_PAD

F32 = jnp.float32
BF16 = jnp.bfloat16


def _dot(a, b):
    return jnp.dot(a, b, preferred_element_type=F32)


def _dot_f32(a, b):
    return jnp.dot(a, b, preferred_element_type=F32, precision=lax.Precision.HIGHEST)


def _ln(r, g, b):
    mu = jnp.mean(r, axis=-1, keepdims=True)
    xc = r - mu
    var = jnp.mean(xc * xc, axis=-1, keepdims=True)
    return xc * lax.rsqrt(var + LN_EPS) * g + b


def _sigmoid(x):
    return 1.0 / (1.0 + jnp.exp(-x))


def _silu(x):
    return x * _sigmoid(x)


def _mod_row(i):
    return jnp.where(i % NT_B == 0, CTX_MOD_ROW, i // NT_B)


def _const_spec(block, index):
    return pl.BlockSpec(block, lambda *_: index, pipeline_mode=pl.Buffered(1))


def _params(sem):
    return pltpu.CompilerParams(dimension_semantics=sem, vmem_limit_bytes=VMEM_LIMIT)


def _mods_kernel(ct_ref, w_ref, b_ref, o_ref):
    s = _silu(ct_ref[...])
    w = w_ref[...]
    o_ref[...] = jnp.zeros(o_ref.shape, F32)
    for r in range(BATCH + 1):
        o_ref[r:r + 1, :] = jnp.sum(s[:, r:r + 1] * w, axis=0, keepdims=True) + b_ref[...]


def _mods_call(c_t, w_ada, b_ada):
    tn = D_MODEL
    n_col = w_ada.shape[-1] // tn
    return pl.pallas_call(
        _mods_kernel,
        grid=(DEPTH, n_col),
        in_specs=[
            pl.BlockSpec((D_MODEL, MOD_ROWS), lambda l, n: (0, 0)),
            pl.BlockSpec((None, D_MODEL, tn), lambda l, n: (l, 0, n)),
            pl.BlockSpec((None, 1, tn), lambda l, n: (l, 0, n)),
        ],
        out_specs=pl.BlockSpec((None, MOD_ROWS, tn), lambda l, n: (l, 0, n)),
        out_shape=jax.ShapeDtypeStruct((DEPTH, MOD_ROWS, w_ada.shape[-1]), F32),
        compiler_params=_params(("parallel", "parallel")),
        name="mods",
    )(c_t, w_ada, b_ada.reshape(DEPTH, 1, -1))


def _embed_kernel(x_ref, ctx_ref, o_ref):
    j = pl.program_id(0) % NT_B

    @pl.when(j == 0)
    def _():
        o_ref[...] = ctx_ref[...]

    @pl.when(j > 0)
    def _():
        quarter = D_MODEL // 4
        t = (j - 1) * TM + lax.broadcasted_iota(jnp.int32, (TM, 1), 0)
        r = lax.shift_right_logical(t, GRID_W.bit_length() - 1).astype(F32)
        col = jnp.bitwise_and(t, GRID_W - 1).astype(F32)
        k = lax.broadcasted_iota(jnp.int32, (1, quarter), 1).astype(F32)
        freqs = jnp.exp(-math.log(10000.0) * k / quarter)
        er = r * freqs
        ec = col * freqs
        o_ref[:, 0 * quarter:1 * quarter] = x_ref[:, 0 * quarter:1 * quarter] + jnp.sin(er)
        o_ref[:, 1 * quarter:2 * quarter] = x_ref[:, 1 * quarter:2 * quarter] + jnp.cos(er)
        o_ref[:, 2 * quarter:3 * quarter] = x_ref[:, 2 * quarter:3 * quarter] + jnp.sin(ec)
        o_ref[:, 3 * quarter:4 * quarter] = x_ref[:, 3 * quarter:4 * quarter] + jnp.cos(ec)


def _embed_call(x, ctx):
    lat_tiles_b = SEQ // TM
    return pl.pallas_call(
        _embed_kernel,
        grid=(NT,),
        in_specs=[
            pl.BlockSpec((TM, D_MODEL), lambda i: ((i // NT_B) * lat_tiles_b + jnp.maximum(i % NT_B - 1, 0), 0)),
            pl.BlockSpec((TM, D_MODEL), lambda i: (i // NT_B, 0)),
        ],
        out_specs=pl.BlockSpec((TM, D_MODEL), lambda i: (i, 0)),
        out_shape=jax.ShapeDtypeStruct((T_ALL, D_MODEL), F32),
        compiler_params=_params(("parallel",)),
        name="embed",
    )(x.reshape(BATCH * SEQ, D_MODEL), ctx.reshape(BATCH * CTX_LEN, D_MODEL))


def _ffn_kernel(x_ref, mod_ref, win_ref, wout_ref, g_ref, b_ref, o_ref, *, sub):
    x = x_ref[...]
    shift = mod_ref[3 * sub + 0:3 * sub + 1, :]
    scale = mod_ref[3 * sub + 1:3 * sub + 2, :]
    gate = mod_ref[3 * sub + 2:3 * sub + 3, :]
    xm = (x * (1.0 + scale) + shift).astype(BF16)
    acc = jnp.zeros((x.shape[0], D_MODEL), F32)
    for c in range(D_FF // FF_CH):
        a1 = _dot(xm, win_ref[:, c * FF_CH:(c + 1) * FF_CH])
        a2 = _dot(xm, win_ref[:, D_FF + c * FF_CH:D_FF + (c + 1) * FF_CH])
        hh = (_silu(a1) * a2).astype(BF16)
        acc = acc + _dot(hh, wout_ref[c * FF_CH:(c + 1) * FF_CH, :])
    o_ref[...] = _ln(ALPHA * x + (FFN_RES * gate) * acc, g_ref[...], b_ref[...])


def _ffn_call(s, mods, ffn_w_in, ffn_w_out, ln_g, ln_b, *, layer, which, final):
    sub = 2 * which
    if final:
        lat_tiles_b = SEQ // TM
        grid = (BATCH * lat_tiles_b,)
        in_tile = lambda i: (i // lat_tiles_b) * NT_B + 1 + i % lat_tiles_b
        mod_row = lambda i: i // lat_tiles_b
        out_rows = BATCH * SEQ
    else:
        grid = (NT,)
        in_tile = lambda i: i
        mod_row = _mod_row
        out_rows = T_ALL
    return pl.pallas_call(
        functools.partial(_ffn_kernel, sub=sub),
        grid=grid,
        in_specs=[
            pl.BlockSpec((TM, D_MODEL), lambda i: (in_tile(i), 0)),
            pl.BlockSpec((None, 9, D_MODEL), lambda i: (layer * MOD_ROWS + mod_row(i), 0, 0)),
            _const_spec((None, None, D_MODEL, 2 * D_FF), (layer, which, 0, 0)),
            _const_spec((None, None, D_FF, D_MODEL), (layer, which, 0, 0)),
            _const_spec((None, 1, D_MODEL), (layer * 3 + sub, 0, 0)),
            _const_spec((None, 1, D_MODEL), (layer * 3 + sub, 0, 0)),
        ],
        out_specs=pl.BlockSpec((TM, D_MODEL), lambda i: (i, 0)),
        out_shape=jax.ShapeDtypeStruct((out_rows, D_MODEL), F32),
        compiler_params=_params(("parallel",)),
        name=f"ffn{which}",
    )(s, mods, ffn_w_in, ffn_w_out, ln_g, ln_b)


def _cproj_kernel(x_ref, prev_ref, next_ref, mod_ref, w_ref, b_ref, cw_ref,
                  q_ref, kt_ref, v_ref, g_ref, gt_ref, xe_ref, p_ref):
    j = pl.program_id(0) % NT_B
    first = jnp.logical_or(j == 0, j == 1)
    last = jnp.logical_or(j == 0, j == NT_B - 1)
    shift = mod_ref[3:4, :]
    scale = mod_ref[4:5, :]

    def modulate(v):
        return (v * (1.0 + scale) + shift).astype(BF16)

    xe_ref[0:HALO, :] = modulate(prev_ref[...])
    xe_ref[HALO:HALO + TM, :] = modulate(x_ref[...])
    xe_ref[HALO + TM:2 * HALO + TM, :] = modulate(next_ref[...])
    xe = xe_ref[...]
    rows = lax.broadcasted_iota(jnp.int32, (TM + 2 * HALO, 1), 0)
    valid = jnp.logical_and(jnp.logical_or(rows >= HALO, jnp.logical_not(first)),
                            jnp.logical_or(rows < HALO + TM, jnp.logical_not(last)))
    p_ref[...] = jnp.where(valid, _dot(xe, w_ref[:, CP_QK:CP_V]) + b_ref[:, CP_QK:CP_V], 0.0)
    conv = (cw_ref[0:1, :] * p_ref[HALO - 1:HALO - 1 + TM, :]
            + cw_ref[1:2, :] * p_ref[HALO:HALO + TM, :]
            + cw_ref[2:3, :] * p_ref[HALO + 1:HALO + 1 + TM, :])
    qk = _silu(conv)
    q_ref[...] = qk[:, :MIX_W].astype(BF16)
    k = qk[:, MIX_W:] * MLSTM_DH ** -0.5
    for c in range(CH_T):
        for h in range(MLSTM_HEADS):
            blk = k[c * CHUNK:(c + 1) * CHUNK, h * MLSTM_DH:(h + 1) * MLSTM_DH]
            r0 = (c * MLSTM_HEADS + h) * MLSTM_DH
            kt_ref[r0:r0 + MLSTM_DH, :] = blk.T.astype(BF16)

    xm = xe_ref[HALO:HALO + TM, :]
    pvg = _dot(xm, w_ref[:, CP_V:CP_COLS]) + b_ref[:, CP_V:CP_COLS]
    v_ref[...] = pvg[:, :MIX_W].astype(BF16)
    graw = pvg[:, MIX_W:]
    lane = lax.broadcasted_iota(jnp.int32, graw.shape, 1)
    is_forget = jnp.logical_and((lane // MLSTM_HEADS) % 2 == 1, lane < 4 * MLSTM_HEADS)
    log_sig = jnp.minimum(graw, 0.0) - jnp.log1p(jnp.exp(-jnp.abs(graw)))
    g = jnp.where(is_forget, log_sig, graw)
    g_ref[...] = g
    for c in range(CH_T):
        gt_ref[c * BF16_ROWS:(c + 1) * BF16_ROWS, :] = g[c * CHUNK:(c + 1) * CHUNK, :].T[:BF16_ROWS, :]


def _cproj_call(s, mods, w_cp, b_cp, qk_conv_w, *, layer):
    hb = TM // HALO
    n_hb = T_ALL // HALO
    return pl.pallas_call(
        _cproj_kernel,
        grid=(NT,),
        in_specs=[
            pl.BlockSpec((TM, D_MODEL), lambda i: (i, 0)),
            pl.BlockSpec((HALO, D_MODEL), lambda i: (jnp.maximum(i * hb - 1, 0), 0)),
            pl.BlockSpec((HALO, D_MODEL), lambda i: (jnp.minimum((i + 1) * hb, n_hb - 1), 0)),
            pl.BlockSpec((None, 9, D_MODEL), lambda i: (layer * MOD_ROWS + _mod_row(i), 0, 0)),
            _const_spec((None, D_MODEL, CP_COLS), (layer, 0, 0)),
            _const_spec((None, 1, CP_COLS), (layer, 0, 0)),
            _const_spec((None, QK_CONV, 2 * MIX_W), (layer, 0, 0)),
        ],
        out_specs=[
            pl.BlockSpec((TM, MIX_W), lambda i: (i, 0)),
            pl.BlockSpec((TM * MLSTM_HEADS, MLSTM_DH), lambda i: (i, 0)),
            pl.BlockSpec((TM, MIX_W), lambda i: (i, 0)),
            pl.BlockSpec((TM, GATE_PAD), lambda i: (i, 0)),
            pl.BlockSpec((CH_T * BF16_ROWS, CHUNK), lambda i: (i, 0)),
        ],
        out_shape=[
            jax.ShapeDtypeStruct((T_ALL, MIX_W), BF16),
            jax.ShapeDtypeStruct((T_ALL * MLSTM_HEADS, MLSTM_DH), BF16),
            jax.ShapeDtypeStruct((T_ALL, MIX_W), BF16),
            jax.ShapeDtypeStruct((T_ALL, GATE_PAD), F32),
            jax.ShapeDtypeStruct((T_ALL // CHUNK * BF16_ROWS, CHUNK), F32),
        ],
        scratch_shapes=[
            pltpu.VMEM((TM + 2 * HALO, D_MODEL), BF16),
            pltpu.VMEM((TM + 2 * HALO, 2 * MIX_W), F32),
        ],
        compiler_params=_params(("parallel",)),
        name="cproj",
    )(s, s, s, mods, w_cp, b_cp, qk_conv_w)


def _mlstm_kernel(qf_ref, ktf_ref, vf_ref, gf_ref, gtf_ref, qb_ref, ktb_ref, vb_ref, gb_ref, gtb_ref,
                  hf_ref, hb_ref, c_ref, m_ref):
    @pl.when(pl.program_id(0) == 0)
    def _():
        c_ref[...] = jnp.zeros(c_ref.shape, F32)
        m_ref[...] = jnp.zeros(m_ref.shape, F32)

    row = lax.broadcasted_iota(jnp.int32, (CHUNK, CHUNK), 0)
    col = lax.broadcasted_iota(jnp.int32, (CHUNK, CHUNK), 1)
    lower = col <= row
    upper = col >= row
    lower_f = lower.astype(F32)
    upper_f = upper.astype(F32)
    ones_col = (col == 0).astype(BF16)
    nh = MLSTM_HEADS
    dirs = (
        (qf_ref, ktf_ref, vf_ref, gf_ref, gtf_ref, hf_ref, lower, lower_f, upper_f, 0, CHUNK - 1),
        (qb_ref, ktb_ref, vb_ref, gb_ref, gtb_ref, hb_ref, upper, upper_f, lower_f, 2 * nh, 0),
    )
    for d, (q_ref, kt_ref, v_ref, g_ref, gt_ref, h_ref, seen, tri, tri_t, gcol, end_row) in enumerate(dirs):
        for b in range(BATCH):
            g = g_ref[b]
            gt = gt_ref[b]
            cum_c = _dot_f32(tri, g)
            cum_r = _dot_f32(gt, tri_t)
            for h in range(nh):
                idx = (d * BATCH + b) * nh + h
                ci, cf = gcol + h, gcol + nh + h
                b_col = cum_c[:, cf:cf + 1]
                li_col = g[:, ci:ci + 1]
                b_row = cum_r[cf:cf + 1, :]
                li_row = gt[ci:ci + 1, :]
                m_prev = m_ref[idx, 0:1, 0:1]
                dmat = jnp.where(seen, b_col - b_row + li_row, -jnp.inf)
                inter = b_col + m_prev
                m_t = jnp.maximum(inter, jnp.max(dmat, axis=-1, keepdims=True))
                qh = q_ref[b, :, h * MLSTM_DH:(h + 1) * MLSTM_DH]
                kt = kt_ref[b, h * MLSTM_DH:(h + 1) * MLSTM_DH, :]
                vh = v_ref[b, :, h * MLSTM_DH:(h + 1) * MLSTM_DH]
                v_aug = jnp.concatenate([vh, ones_col], axis=1)
                s = _dot(qh, kt) * jnp.exp(dmat - m_t)
                w_inter = jnp.exp(inter - m_t)
                c_old = c_ref[idx]
                inter_aug = _dot(qh, c_old.astype(BF16))
                num = _dot(s.astype(BF16), vh) + w_inter * inter_aug[:, :MLSTM_DH]
                den = jnp.sum(s, axis=-1, keepdims=True) + w_inter * inter_aug[:, MLSTM_DH:MLSTM_DH + 1]
                h_ref[b, :, h * MLSTM_DH:(h + 1) * MLSTM_DH] = num / jnp.maximum(jnp.abs(den), jnp.exp(-m_t))
                b_end = cum_c[end_row:end_row + 1, cf:cf + 1]
                logw = b_end - b_col + li_col
                m_new = jnp.maximum(b_end + m_prev, jnp.max(logw, axis=0, keepdims=True))
                w = jnp.exp(logw - m_new)
                decay = jnp.exp(b_end + m_prev - m_new)
                c_ref[idx] = decay * c_old + _dot(kt, (w * v_aug.astype(F32)).astype(BF16))
                m_ref[idx] = jnp.broadcast_to(m_new, m_ref.shape[1:])


def _mlstm_call(q, kt, v, g, gt):
    fwd = lambda s: s
    bwd = lambda s: jnp.where(s < CTX_CH, CTX_CH - 1 - s, NCH_B + CTX_CH - 1 - s)
    q3 = q.reshape(BATCH, SEQ_B, MIX_W)
    kt3 = kt.reshape(BATCH, SEQ_B * MLSTM_HEADS, MLSTM_DH)
    v3 = v.reshape(BATCH, SEQ_B, MIX_W)
    g3 = g.reshape(BATCH, SEQ_B, GATE_PAD)
    gt3 = gt.reshape(BATCH, NCH_B * BF16_ROWS, CHUNK)

    def specs(order):
        return [
            pl.BlockSpec((BATCH, CHUNK, MIX_W), lambda s: (0, order(s), 0)),
            pl.BlockSpec((BATCH, CHUNK * MLSTM_HEADS, MLSTM_DH), lambda s: (0, order(s), 0)),
            pl.BlockSpec((BATCH, CHUNK, MIX_W), lambda s: (0, order(s), 0)),
            pl.BlockSpec((BATCH, CHUNK, GATE_PAD), lambda s: (0, order(s), 0)),
            pl.BlockSpec((BATCH, BF16_ROWS, CHUNK), lambda s: (0, order(s), 0)),
        ]

    n_state = 2 * BATCH * MLSTM_HEADS
    hf, hb = pl.pallas_call(
        _mlstm_kernel,
        grid=(NCH_B,),
        in_specs=specs(fwd) + specs(bwd),
        out_specs=[
            pl.BlockSpec((BATCH, CHUNK, MIX_W), lambda s: (0, fwd(s), 0)),
            pl.BlockSpec((BATCH, CHUNK, MIX_W), lambda s: (0, bwd(s), 0)),
        ],
        out_shape=[jax.ShapeDtypeStruct((BATCH, SEQ_B, MIX_W), F32)] * 2,
        scratch_shapes=[
            pltpu.VMEM((n_state, MLSTM_DH, 2 * MLSTM_DH), F32),
            pltpu.VMEM((n_state, SUBLANES, LANES), F32),
        ],
        compiler_params=_params(("arbitrary",)),
        name="mlstm",
    )(q3, kt3, v3, g3, gt3, q3, kt3, v3, g3, gt3)
    return hf.reshape(T_ALL, MIX_W), hb.reshape(T_ALL, MIX_W)


VEC_GMLP_G, VEC_GMLP_B, VEC_CONV_B, VEC_CONV_G, VEC_CONV_LB, VEC_MLSTM_G, VEC_POOL_S = range(7)
VEC_ROWS = 8


def _mix_kernel(x_ref, prev_ref, next_ref, mod_ref, hf_ref, hb_ref, w_ref, b_ref, vec_ref, ws_ref, bst_ref,
                cw_ref, pw_ref, wbr_ref, wout_ref, g_ref, beta_ref, o_ref, xe_ref, a_ref, d_ref):
    j = pl.program_id(0) % NT_B
    is_ctx = j == 0
    first = jnp.logical_or(is_ctx, j == 1)
    last = jnp.logical_or(is_ctx, j == NT_B - 1)
    shift = mod_ref[3:4, :]
    scale = mod_ref[4:5, :]
    gate = mod_ref[5:6, :]

    def modulate(v):
        return (v * (1.0 + scale) + shift).astype(BF16)

    def vec(r):
        return vec_ref[r:r + 1, :]

    x = x_ref[...]
    xe_ref[0:HALO, :] = modulate(prev_ref[...])
    xe_ref[HALO:HALO + TM, :] = modulate(x)
    xe_ref[HALO + TM:2 * HALO + TM, :] = modulate(next_ref[...])
    xe = xe_ref[...]
    xm = xe_ref[HALO:HALO + TM, :]
    rows = lax.broadcasted_iota(jnp.int32, (TM + 2 * HALO, 1), 0)
    valid = jnp.logical_and(jnp.logical_or(rows >= HALO, jnp.logical_not(first)),
                            jnp.logical_or(rows < HALO + TM, jnp.logical_not(last)))

    def branch_gate(i):
        lo = MX_G + i * D_MODEL
        return _sigmoid(_dot(xm, w_ref[:, lo:lo + D_MODEL]) + b_ref[:, lo:lo + D_MODEL])

    def merged(i, y):
        return branch_gate(i) * _dot(y.astype(BF16), wbr_ref[i])

    pa = jax.nn.gelu(_dot(xm, w_ref[:, MX_A:MX_B]) + b_ref[:, MX_A:MX_B])
    u = pa[:, :MIX_W]
    vn = _ln(pa[:, MIX_W:], vec(VEC_GMLP_G), vec(VEC_GMLP_B)).astype(BF16)
    gd = MIX_W // GMLP_GROUPS
    z_rows = []
    for c in range(CH_T):
        z_cols = []
        for gi in range(GMLP_GROUPS):
            blk = vn[c * CHUNK:(c + 1) * CHUNK, gi * gd:(gi + 1) * gd]
            z_cols.append(_dot(ws_ref[gi], blk) + bst_ref[:, gi:gi + 1])
        z_rows.append(jnp.concatenate(z_cols, axis=1))
    acc = merged(0, u * jnp.concatenate(z_rows, axis=0))

    pb = _dot(xe, w_ref[:, MX_B:MX_O]) + b_ref[:, MX_B:MX_O]
    a_ref[...] = jnp.where(valid, pb[:, :MIX_W] * _sigmoid(pb[:, MIX_W:]), 0.0)
    conv = jnp.zeros((TM, MIX_W), F32)
    for k in range(CONV_W):
        lo = HALO - CONV_W // 2 + k
        conv = conv + cw_ref[k:k + 1, :] * a_ref[lo:lo + TM, :]
    yb = _silu(_ln(conv + vec(VEC_CONV_B), vec(VEC_CONV_G), vec(VEC_CONV_LB)))
    acc = acc + merged(1, yb)

    og = _sigmoid(_dot(xm, w_ref[:, MX_O:MX_D]) + b_ref[:, MX_O:MX_D])
    hsum = hf_ref[...] + hb_ref[...]
    hn = []
    for h in range(MLSTM_HEADS):
        hh = hsum[:, h * MLSTM_DH:(h + 1) * MLSTM_DH]
        mu = jnp.mean(hh, axis=-1, keepdims=True)
        hc = hh - mu
        hn.append(hc * lax.rsqrt(jnp.mean(hc * hc, axis=-1, keepdims=True) + LN_EPS))
    yc = og * (jnp.concatenate(hn, axis=1) * vec(VEC_MLSTM_G))
    acc = acc + merged(2, yc)

    d_ref[...] = jnp.where(valid, _dot(xe, w_ref[:, MX_D:MX_G]) + b_ref[:, MX_D:MX_G], 0.0)
    pos = jnp.where(is_ctx, 0, (j - 1) * TM) + lax.broadcasted_iota(jnp.int32, (TM, 1), 0)
    n_seq = jnp.where(is_ctx, CTX_LEN, SEQ)
    yd = []
    for gi, win in enumerate(POOL_WINDOWS):
        lo, hi = win // 2, win - 1 - win // 2
        cols = slice(gi * POOL_GD, (gi + 1) * POOL_GD)
        wsum = d_ref[HALO - lo:HALO - lo + TM, cols]
        for k in range(-lo + 1, hi + 1):
            wsum = wsum + d_ref[HALO + k:HALO + k + TM, cols]
        cnt = (jnp.minimum(pos + hi + 1, n_seq) - jnp.maximum(pos - lo, 0)).astype(F32)
        diff = wsum / cnt - d_ref[HALO:HALO + TM, cols]
        yd.append(_dot(diff.astype(BF16), pw_ref[gi]))
    acc = acc + merged(3, jnp.concatenate(yd, axis=1) * vec(VEC_POOL_S))

    y = _dot(acc.astype(BF16), wout_ref[...])
    o_ref[...] = _ln(ALPHA * x + gate * y, g_ref[...], beta_ref[...])


def _mix_call(s, mods, hf, hb, w_mx, b_mx, vec512, gmlp_ws, gmlp_bst, conv_w, pool_w, w_branch, w_out,
              ln_g, ln_b, *, layer):
    hb_per_tile = TM // HALO
    n_hb = T_ALL // HALO
    return pl.pallas_call(
        _mix_kernel,
        grid=(NT,),
        in_specs=[
            pl.BlockSpec((TM, D_MODEL), lambda i: (i, 0)),
            pl.BlockSpec((HALO, D_MODEL), lambda i: (jnp.maximum(i * hb_per_tile - 1, 0), 0)),
            pl.BlockSpec((HALO, D_MODEL), lambda i: (jnp.minimum((i + 1) * hb_per_tile, n_hb - 1), 0)),
            pl.BlockSpec((None, 9, D_MODEL), lambda i: (layer * MOD_ROWS + _mod_row(i), 0, 0)),
            pl.BlockSpec((TM, MIX_W), lambda i: (i, 0)),
            pl.BlockSpec((TM, MIX_W), lambda i: (i, 0)),
            _const_spec((None, D_MODEL, MX_COLS), (layer, 0, 0)),
            _const_spec((None, 1, MX_COLS), (layer, 0, 0)),
            _const_spec((None, VEC_ROWS, MIX_W), (layer, 0, 0)),
            _const_spec((None, GMLP_GROUPS, CHUNK, CHUNK), (layer, 0, 0, 0)),
            _const_spec((None, CHUNK, GMLP_GROUPS), (layer, 0, 0)),
            _const_spec((None, CONV_W, MIX_W), (layer, 0, 0)),
            _const_spec((None, len(POOL_WINDOWS), POOL_GD, POOL_GD), (layer, 0, 0, 0)),
            _const_spec((None, N_BRANCH, MIX_W, D_MODEL), (layer, 0, 0, 0)),
            _const_spec((None, D_MODEL, D_MODEL), (layer, 0, 0)),
            _const_spec((None, 1, D_MODEL), (layer * 3 + 1, 0, 0)),
            _const_spec((None, 1, D_MODEL), (layer * 3 + 1, 0, 0)),
        ],
        out_specs=pl.BlockSpec((TM, D_MODEL), lambda i: (i, 0)),
        out_shape=jax.ShapeDtypeStruct((T_ALL, D_MODEL), F32),
        scratch_shapes=[
            pltpu.VMEM((TM + 2 * HALO, D_MODEL), BF16),
            pltpu.VMEM((TM + 2 * HALO, MIX_W), F32),
            pltpu.VMEM((TM + 2 * HALO, MIX_W), F32),
        ],
        compiler_params=_params(("parallel",)),
        name="mix",
    )(s, s, s, mods, hf, hb, w_mx, b_mx, vec512, gmlp_ws, gmlp_bst, conv_w, pool_w, w_branch, w_out, ln_g, ln_b)


def _repack_cols(w, pieces, pad=0):
    parts = [w[..., lo:hi] for lo, hi in pieces]
    if pad:
        parts.append(jnp.zeros(w.shape[:-1] + (pad,), w.dtype))
    return jnp.concatenate(parts, axis=-1)


def kernel(x, c, ctx, c_ctx, w_ada, b_ada, ln_g, ln_b, ffn_w_in, ffn_w_out, w_in, b_in, gmlp_ln_g, gmlp_ln_b,
           gmlp_ws, gmlp_bs, conv_w, conv_b, conv_ln_g, conv_ln_b, qk_conv_w, mlstm_ln_g, pool_w, pool_scale,
           w_branch, w_out):
    mix_pieces = [(OFF_A, OFF_C), (OFF_C_O, OFF_D), (OFF_D, OFF_G), (OFF_G, IN_COLS)]
    cp_pieces = [(OFF_C, OFF_C_O)]
    w_mx = _repack_cols(w_in, mix_pieces).astype(BF16)
    b_mx = _repack_cols(b_in, mix_pieces).reshape(DEPTH, 1, MX_COLS)
    gate_pad = GATE_PAD - 4 * MLSTM_HEADS
    w_cp = _repack_cols(w_in, cp_pieces, gate_pad).astype(BF16)
    b_cp = _repack_cols(b_in, cp_pieces, gate_pad).reshape(DEPTH, 1, CP_COLS)
    ffn_w_in_b = ffn_w_in.astype(BF16)
    ffn_w_out_b = ffn_w_out.astype(BF16)
    w_branch_b = w_branch.astype(BF16)
    w_out_b = w_out.astype(BF16)
    gmlp_ws_b = gmlp_ws.astype(BF16)
    pool_w_b = pool_w.astype(BF16)
    gmlp_bst = jnp.swapaxes(gmlp_bs, 1, 2)
    vec512 = jnp.stack([gmlp_ln_g, gmlp_ln_b, conv_b, conv_ln_g, conv_ln_b, mlstm_ln_g, pool_scale,
                        jnp.zeros_like(pool_scale)], axis=1)
    ln_g3 = ln_g.reshape(DEPTH * 3, 1, D_MODEL)
    ln_b3 = ln_b.reshape(DEPTH * 3, 1, D_MODEL)
    c_t = jnp.concatenate([c, c_ctx[None], jnp.zeros((MOD_ROWS - BATCH - 1, D_MODEL), F32)], axis=0).T

    mods = _mods_call(c_t, w_ada, b_ada).reshape(DEPTH * MOD_ROWS, 9, D_MODEL)
    s = _embed_call(x, ctx)
    for l in range(DEPTH):
        last = l == DEPTH - 1
        s = _ffn_call(s, mods, ffn_w_in_b, ffn_w_out_b, ln_g3, ln_b3, layer=l, which=0, final=False)
        q, kt, v, g, gt = _cproj_call(s, mods, w_cp, b_cp, qk_conv_w, layer=l)
        hf, hb = _mlstm_call(q, kt, v, g, gt)
        s = _mix_call(s, mods, hf, hb, w_mx, b_mx, vec512, gmlp_ws_b, gmlp_bst, conv_w, pool_w_b, w_branch_b,
                      w_out_b, ln_g3, ln_b3, layer=l)
        s = _ffn_call(s, mods, ffn_w_in_b, ffn_w_out_b, ln_g3, ln_b3, layer=l, which=1, final=last)
    return s.reshape(BATCH, SEQ, D_MODEL)
```

```python
import functools
import math

import jax
import jax.numpy as jnp
from jax import lax
from jax.experimental import pallas as pl
from jax.experimental.pallas import tpu as pltpu

D_MODEL = 1024
BATCH = 2
SEQ = 8192
DEPTH = 4
GRID_W = 64
CTX_LEN = 256
MIX_W = D_MODEL // 2
N_BRANCH = 4
CHUNK = 128
GMLP_GROUPS = 4
CONV_W = 31
MLSTM_HEADS = 4
MLSTM_DH = MIX_W // MLSTM_HEADS
QK_CONV = 3
POOL_WINDOWS = (2, 4, 8, 16)
POOL_GD = MIX_W // len(POOL_WINDOWS)
D_FF = 128 * ((8 * D_MODEL // 3 + 127) // 128)
ALPHA = (2 * DEPTH) ** 0.25
LN_EPS = 1e-6
FFN_RES = 0.5

OFF_A = 0
OFF_B = OFF_A + 2 * MIX_W
OFF_C = OFF_B + 2 * MIX_W
OFF_C_GATES = OFF_C + 3 * MIX_W
OFF_C_O = OFF_C_GATES + 4 * MLSTM_HEADS
OFF_D = OFF_C_O + MIX_W
OFF_G = OFF_D + MIX_W
IN_COLS = OFF_G + N_BRANCH * D_MODEL

LANES = 128
SUBLANES = 8
BF16_ROWS = 16
TM = 256
SEQ_B = CTX_LEN + SEQ
T_ALL = BATCH * SEQ_B
NT_B = SEQ_B // TM
NT = BATCH * NT_B
NCH_B = SEQ_B // CHUNK
CH_T = TM // CHUNK
CTX_CH = CTX_LEN // CHUNK
HALO = 16
FF_CH = 256
GS_RMAX = 2 * MLSTM_HEADS
MOD_ROWS = 8
CTX_MOD_ROW = BATCH
VMEM_LIMIT = 56 * 1024 * 1024

MX_A = 0
MX_B = MX_A + 2 * MIX_W
MX_O = MX_B + 2 * MIX_W
MX_D = MX_O + MIX_W
MX_G = MX_D + MIX_W
MX_COLS = MX_G + N_BRANCH * D_MODEL
CP_QK = 0
CP_V = 2 * MIX_W
CP_GI = 3 * MIX_W
CP_GF = CP_GI + LANES
CP_COLS = CP_GF + LANES

F32 = jnp.float32
BF16 = jnp.bfloat16


def _dot(a, b):
    return jnp.dot(a, b, preferred_element_type=F32)


def _dot_f32(a, b):
    return jnp.dot(a, b, preferred_element_type=F32, precision=lax.Precision.HIGHEST)


def _ln(r, g, b):
    mu = jnp.mean(r, axis=-1, keepdims=True)
    xc = r - mu
    var = jnp.mean(xc * xc, axis=-1, keepdims=True)
    return xc * lax.rsqrt(var + LN_EPS) * g + b


def _sigmoid(x):
    return 1.0 / (1.0 + jnp.exp(-x))


def _silu(x):
    return x * _sigmoid(x)


def _mod_row(i):
    return jnp.where(i % NT_B == 0, CTX_MOD_ROW, i // NT_B)


def _const_spec(block, index):
    return pl.BlockSpec(block, lambda *_: index, pipeline_mode=pl.Buffered(1))


def _params(sem):
    return pltpu.CompilerParams(dimension_semantics=sem, vmem_limit_bytes=VMEM_LIMIT)


def _mods_kernel(ct_ref, w_ref, b_ref, o_ref):
    s = _silu(ct_ref[...])
    w = w_ref[...]
    o_ref[...] = jnp.zeros(o_ref.shape, F32)
    for r in range(BATCH + 1):
        o_ref[r:r + 1, :] = jnp.sum(s[:, r:r + 1] * w, axis=0, keepdims=True) + b_ref[...]


def _mods_call(c_t, w_ada, b_ada):
    tn = D_MODEL
    n_col = w_ada.shape[-1] // tn
    return pl.pallas_call(
        _mods_kernel,
        grid=(DEPTH, n_col),
        in_specs=[
            pl.BlockSpec((D_MODEL, MOD_ROWS), lambda l, n: (0, 0)),
            pl.BlockSpec((None, D_MODEL, tn), lambda l, n: (l, 0, n)),
            pl.BlockSpec((None, 1, tn), lambda l, n: (l, 0, n)),
        ],
        out_specs=pl.BlockSpec((None, MOD_ROWS, tn), lambda l, n: (l, 0, n)),
        out_shape=jax.ShapeDtypeStruct((DEPTH, MOD_ROWS, w_ada.shape[-1]), F32),
        compiler_params=_params(("parallel", "parallel")),
        name="mods",
    )(c_t, w_ada, b_ada.reshape(DEPTH, 1, -1))


def _embed_kernel(x_ref, ctx_ref, o_ref):
    j = pl.program_id(0) % NT_B

    @pl.when(j == 0)
    def _():
        o_ref[...] = ctx_ref[...]

    @pl.when(j > 0)
    def _():
        quarter = D_MODEL // 4
        t = (j - 1) * TM + lax.broadcasted_iota(jnp.int32, (TM, 1), 0)
        r = lax.shift_right_logical(t, GRID_W.bit_length() - 1).astype(F32)
        col = jnp.bitwise_and(t, GRID_W - 1).astype(F32)
        k = lax.broadcasted_iota(jnp.int32, (1, quarter), 1).astype(F32)
        freqs = jnp.exp(-math.log(10000.0) * k / quarter)
        er = r * freqs
        ec = col * freqs
        o_ref[:, 0 * quarter:1 * quarter] = x_ref[:, 0 * quarter:1 * quarter] + jnp.sin(er)
        o_ref[:, 1 * quarter:2 * quarter] = x_ref[:, 1 * quarter:2 * quarter] + jnp.cos(er)
        o_ref[:, 2 * quarter:3 * quarter] = x_ref[:, 2 * quarter:3 * quarter] + jnp.sin(ec)
        o_ref[:, 3 * quarter:4 * quarter] = x_ref[:, 3 * quarter:4 * quarter] + jnp.cos(ec)


def _embed_call(x, ctx):
    lat_tiles_b = SEQ // TM
    return pl.pallas_call(
        _embed_kernel,
        grid=(NT,),
        in_specs=[
            pl.BlockSpec((TM, D_MODEL), lambda i: ((i // NT_B) * lat_tiles_b + jnp.maximum(i % NT_B - 1, 0), 0)),
            pl.BlockSpec((TM, D_MODEL), lambda i: (i // NT_B, 0)),
        ],
        out_specs=pl.BlockSpec((TM, D_MODEL), lambda i: (i, 0)),
        out_shape=jax.ShapeDtypeStruct((T_ALL, D_MODEL), F32),
        compiler_params=_params(("parallel",)),
        name="embed",
    )(x.reshape(BATCH * SEQ, D_MODEL), ctx.reshape(BATCH * CTX_LEN, D_MODEL))


def _ffn_kernel(x_ref, mod_ref, win_ref, wout_ref, g_ref, b_ref, o_ref, *, sub):
    x = x_ref[...]
    shift = mod_ref[3 * sub + 0:3 * sub + 1, :]
    scale = mod_ref[3 * sub + 1:3 * sub + 2, :]
    gate = mod_ref[3 * sub + 2:3 * sub + 3, :]
    xm = (x * (1.0 + scale) + shift).astype(BF16)
    acc = jnp.zeros((x.shape[0], D_MODEL), F32)
    for c in range(D_FF // FF_CH):
        a1 = _dot(xm, win_ref[:, c * FF_CH:(c + 1) * FF_CH])
        a2 = _dot(xm, win_ref[:, D_FF + c * FF_CH:D_FF + (c + 1) * FF_CH])
        hh = (_silu(a1) * a2).astype(BF16)
        acc = acc + _dot(hh, wout_ref[c * FF_CH:(c + 1) * FF_CH, :])
    o_ref[...] = _ln(ALPHA * x + (FFN_RES * gate) * acc, g_ref[...], b_ref[...])


def _ffn_call(s, mods, ffn_w_in, ffn_w_out, ln_g, ln_b, *, layer, which, final):
    sub = 2 * which
    if final:
        lat_tiles_b = SEQ // TM
        grid = (BATCH * lat_tiles_b,)
        in_tile = lambda i: (i // lat_tiles_b) * NT_B + 1 + i % lat_tiles_b
        mod_row = lambda i: i // lat_tiles_b
        out_rows = BATCH * SEQ
    else:
        grid = (NT,)
        in_tile = lambda i: i
        mod_row = _mod_row
        out_rows = T_ALL
    return pl.pallas_call(
        functools.partial(_ffn_kernel, sub=sub),
        grid=grid,
        in_specs=[
            pl.BlockSpec((TM, D_MODEL), lambda i: (in_tile(i), 0)),
            pl.BlockSpec((None, 9, D_MODEL), lambda i: (layer * MOD_ROWS + mod_row(i), 0, 0)),
            _const_spec((None, None, D_MODEL, 2 * D_FF), (layer, which, 0, 0)),
            _const_spec((None, None, D_FF, D_MODEL), (layer, which, 0, 0)),
            _const_spec((None, 1, D_MODEL), (layer * 3 + sub, 0, 0)),
            _const_spec((None, 1, D_MODEL), (layer * 3 + sub, 0, 0)),
        ],
        out_specs=pl.BlockSpec((TM, D_MODEL), lambda i: (i, 0)),
        out_shape=jax.ShapeDtypeStruct((out_rows, D_MODEL), F32),
        compiler_params=_params(("parallel",)),
        name=f"ffn{which}",
    )(s, mods, ffn_w_in, ffn_w_out, ln_g, ln_b)


def _cproj_kernel(x_ref, prev_ref, next_ref, mod_ref, w_ref, b_ref, cw_ref,
                  q_ref, kt_ref, v_ref, g_ref, gt_ref, xe_ref, p_ref):
    j = pl.program_id(0) % NT_B
    first = jnp.logical_or(j == 0, j == 1)
    last = jnp.logical_or(j == 0, j == NT_B - 1)
    shift = mod_ref[3:4, :]
    scale = mod_ref[4:5, :]

    def modulate(v):
        return (v * (1.0 + scale) + shift).astype(BF16)

    xe_ref[0:HALO, :] = modulate(prev_ref[...])
    xe_ref[HALO:HALO + TM, :] = modulate(x_ref[...])
    xe_ref[HALO + TM:2 * HALO + TM, :] = modulate(next_ref[...])
    xe = xe_ref[...]
    rows = lax.broadcasted_iota(jnp.int32, (TM + 2 * HALO, 1), 0)
    valid = jnp.logical_and(jnp.logical_or(rows >= HALO, jnp.logical_not(first)),
                            jnp.logical_or(rows < HALO + TM, jnp.logical_not(last)))
    p_ref[...] = jnp.where(valid, _dot(xe, w_ref[:, CP_QK:CP_V]) + b_ref[:, CP_QK:CP_V], 0.0)
    conv = (cw_ref[0:1, :] * p_ref[HALO - 1:HALO - 1 + TM, :]
            + cw_ref[1:2, :] * p_ref[HALO:HALO + TM, :]
            + cw_ref[2:3, :] * p_ref[HALO + 1:HALO + 1 + TM, :])
    qk = _silu(conv)
    q_ref[...] = qk[:, :MIX_W].astype(BF16)
    k = qk[:, MIX_W:] * MLSTM_DH ** -0.5
    for c in range(CH_T):
        for h in range(MLSTM_HEADS):
            blk = k[c * CHUNK:(c + 1) * CHUNK, h * MLSTM_DH:(h + 1) * MLSTM_DH]
            r0 = (c * MLSTM_HEADS + h) * MLSTM_DH
            kt_ref[r0:r0 + MLSTM_DH, :] = blk.T.astype(BF16)

    xm = xe_ref[HALO:HALO + TM, :]
    pvg = _dot(xm, w_ref[:, CP_V:CP_COLS]) + b_ref[:, CP_V:CP_COLS]
    v_ref[...] = pvg[:, :MIX_W].astype(BF16)
    li = pvg[:, CP_GI - CP_V:CP_GF - CP_V]
    fraw = pvg[:, CP_GF - CP_V:]
    lf = jnp.minimum(fraw, 0.0) - jnp.log1p(jnp.exp(-jnp.abs(fraw)))
    row = lax.broadcasted_iota(jnp.int32, (CHUNK, LANES), 0)
    lane = lax.broadcasted_iota(jnp.int32, (CHUNK, LANES), 1)
    fwd_lane = lane < MLSTM_HEADS
    lower_f = (lane <= row).astype(F32)
    upper_f = (lane >= row).astype(F32)
    for c in range(CH_T):
        tok = slice(c * CHUNK, (c + 1) * CHUNK)
        cum = jnp.where(fwd_lane, _dot_f32(lower_f, lf[tok]), _dot_f32(upper_f, lf[tok]))
        r = li[tok] - cum
        rf, rb = r, r
        k = 1
        while k < CHUNK:
            rf = jnp.maximum(rf, jnp.where(row >= k, pltpu.roll(rf, k, 0), -jnp.inf))
            rb = jnp.maximum(rb, jnp.where(row < CHUNK - k, pltpu.roll(rb, CHUNK - k, 0), -jnp.inf))
            k *= 2
        rmax = jnp.where(fwd_lane, rf, rb)
        g_ref[tok, :] = jnp.where(lane < GS_RMAX, cum, pltpu.roll(rmax, GS_RMAX, 1))
        gt_ref[c * SUBLANES:(c + 1) * SUBLANES, :] = r.T[:SUBLANES, :]


def _cproj_call(s, mods, w_cp, b_cp, qk_conv_w, *, layer):
    hb = TM // HALO
    n_hb = T_ALL // HALO
    return pl.pallas_call(
        _cproj_kernel,
        grid=(NT,),
        in_specs=[
            pl.BlockSpec((TM, D_MODEL), lambda i: (i, 0)),
            pl.BlockSpec((HALO, D_MODEL), lambda i: (jnp.maximum(i * hb - 1, 0), 0)),
            pl.BlockSpec((HALO, D_MODEL), lambda i: (jnp.minimum((i + 1) * hb, n_hb - 1), 0)),
            pl.BlockSpec((None, 9, D_MODEL), lambda i: (layer * MOD_ROWS + _mod_row(i), 0, 0)),
            _const_spec((None, D_MODEL, CP_COLS), (layer, 0, 0)),
            _const_spec((None, 1, CP_COLS), (layer, 0, 0)),
            _const_spec((None, QK_CONV, 2 * MIX_W), (layer, 0, 0)),
        ],
        out_specs=[
            pl.BlockSpec((TM, MIX_W), lambda i: (i, 0)),
            pl.BlockSpec((TM * MLSTM_HEADS, MLSTM_DH), lambda i: (i, 0)),
            pl.BlockSpec((TM, MIX_W), lambda i: (i, 0)),
            pl.BlockSpec((TM, LANES), lambda i: (i, 0)),
            pl.BlockSpec((CH_T * SUBLANES, CHUNK), lambda i: (i, 0)),
        ],
        out_shape=[
            jax.ShapeDtypeStruct((T_ALL, MIX_W), BF16),
            jax.ShapeDtypeStruct((T_ALL * MLSTM_HEADS, MLSTM_DH), BF16),
            jax.ShapeDtypeStruct((T_ALL, MIX_W), BF16),
            jax.ShapeDtypeStruct((T_ALL, LANES), F32),
            jax.ShapeDtypeStruct((T_ALL // CHUNK * SUBLANES, CHUNK), F32),
        ],
        scratch_shapes=[
            pltpu.VMEM((TM + 2 * HALO, D_MODEL), BF16),
            pltpu.VMEM((TM + 2 * HALO, 2 * MIX_W), F32),
        ],
        compiler_params=_params(("parallel",)),
        name="cproj",
    )(s, s, s, mods, w_cp, b_cp, qk_conv_w)


def _mlstm_kernel(qf_ref, ktf_ref, vf_ref, gf_ref, gtf_ref, qb_ref, ktb_ref, vb_ref, gb_ref, gtb_ref,
                  hf_ref, hb_ref, c_ref, m_ref):
    @pl.when(pl.program_id(0) == 0)
    def _():
        c_ref[...] = jnp.zeros(c_ref.shape, F32)
        m_ref[...] = jnp.zeros(m_ref.shape, F32)

    row = lax.broadcasted_iota(jnp.int32, (CHUNK, CHUNK), 0)
    col = lax.broadcasted_iota(jnp.int32, (CHUNK, CHUNK), 1)
    ones_col = (col == 0).astype(BF16)
    nh = MLSTM_HEADS
    dirs = ((qf_ref, ktf_ref, vf_ref, gf_ref, gtf_ref, hf_ref, col <= row, CHUNK - 1),
            (qb_ref, ktb_ref, vb_ref, gb_ref, gtb_ref, hb_ref, col >= row, 0))
    heads = []
    for d, (q_ref, kt_ref, v_ref, g_ref, gt_ref, h_ref, seen, end_row) in enumerate(dirs):
        for b in range(BATCH):
            g = g_ref[b]
            gt = gt_ref[b]
            for h in range(nh):
                lane = d * nh + h
                hd = slice(h * MLSTM_DH, (h + 1) * MLSTM_DH)
                heads.append(dict(
                    idx=(d * BATCH + b) * nh + h, seen=seen, end_row=end_row, out=(h_ref, b, hd),
                    cum=g[:, lane:lane + 1], rmax=g[:, GS_RMAX + lane:GS_RMAX + lane + 1], r=gt[lane:lane + 1, :],
                    q=q_ref[b, :, hd], kt=kt_ref[b, hd, :], v=v_ref[b, :, hd]))

    for hd in heads:
        hd["qk"] = _dot(hd["q"], hd["kt"])
        hd["m_prev"] = m_ref[hd["idx"], 0:1, 0:1]
        hd["c_old"] = c_ref[hd["idx"]]
        hd["v_aug"] = jnp.concatenate([hd["v"], ones_col], axis=1)
    for hd in heads:
        e = hd["end_row"]
        top = jnp.maximum(hd["m_prev"], hd["rmax"][e:e + 1, :])
        w_row = jnp.exp(hd["r"] - top)
        ktw = (hd["kt"].astype(F32) * w_row).astype(BF16)
        c_ref[hd["idx"]] = jnp.exp(hd["m_prev"] - top) * hd["c_old"] + _dot(ktw, hd["v_aug"])
        m_ref[hd["idx"]] = jnp.broadcast_to(hd["cum"][e:e + 1, :] + top, m_ref.shape[1:])
    for hd in heads:
        m_prev = hd["m_prev"]
        top = jnp.maximum(m_prev, hd["rmax"])
        s = hd["qk"] * jnp.exp(jnp.where(hd["seen"], hd["r"] - top, -jnp.inf))
        qw = hd["q"].astype(F32) * jnp.exp(m_prev - top)
        lhs = jnp.concatenate([s.astype(BF16), qw.astype(BF16)], axis=1)
        rhs = jnp.concatenate([hd["v_aug"], hd["c_old"].astype(BF16)], axis=0)
        hd["na"] = _dot(lhs, rhs)
        hd["floor"] = jnp.exp(-(hd["cum"] + top))
    for hd in heads:
        den = jnp.maximum(jnp.abs(hd["na"][:, MLSTM_DH:MLSTM_DH + 1]), hd["floor"])
        h_ref, b, cols = hd["out"]
        h_ref[b, :, cols] = hd["na"][:, :MLSTM_DH] * (1.0 / den)


def _mlstm_call(q, kt, v, g, gt):
    fwd = lambda s: s
    bwd = lambda s: jnp.where(s < CTX_CH, CTX_CH - 1 - s, NCH_B + CTX_CH - 1 - s)
    q3 = q.reshape(BATCH, SEQ_B, MIX_W)
    kt3 = kt.reshape(BATCH, SEQ_B * MLSTM_HEADS, MLSTM_DH)
    v3 = v.reshape(BATCH, SEQ_B, MIX_W)
    g3 = g.reshape(BATCH, SEQ_B, LANES)
    gt3 = gt.reshape(BATCH, NCH_B * SUBLANES, CHUNK)

    def specs(order):
        return [
            pl.BlockSpec((BATCH, CHUNK, MIX_W), lambda s: (0, order(s), 0)),
            pl.BlockSpec((BATCH, CHUNK * MLSTM_HEADS, MLSTM_DH), lambda s: (0, order(s), 0)),
            pl.BlockSpec((BATCH, CHUNK, MIX_W), lambda s: (0, order(s), 0)),
            pl.BlockSpec((BATCH, CHUNK, LANES), lambda s: (0, order(s), 0)),
            pl.BlockSpec((BATCH, SUBLANES, CHUNK), lambda s: (0, order(s), 0)),
        ]

    n_state = 2 * BATCH * MLSTM_HEADS
    hf, hb = pl.pallas_call(
        _mlstm_kernel,
        grid=(NCH_B,),
        in_specs=specs(fwd) + specs(bwd),
        out_specs=[
            pl.BlockSpec((BATCH, CHUNK, MIX_W), lambda s: (0, fwd(s), 0)),
            pl.BlockSpec((BATCH, CHUNK, MIX_W), lambda s: (0, bwd(s), 0)),
        ],
        out_shape=[jax.ShapeDtypeStruct((BATCH, SEQ_B, MIX_W), F32)] * 2,
        scratch_shapes=[
            pltpu.VMEM((n_state, MLSTM_DH, 2 * MLSTM_DH), F32),
            pltpu.VMEM((n_state, SUBLANES, LANES), F32),
        ],
        compiler_params=_params(("arbitrary",)),
        name="mlstm",
    )(q3, kt3, v3, g3, gt3, q3, kt3, v3, g3, gt3)
    return hf.reshape(T_ALL, MIX_W), hb.reshape(T_ALL, MIX_W)


VEC_GMLP_G, VEC_GMLP_B, VEC_CONV_B, VEC_CONV_G, VEC_CONV_LB, VEC_MLSTM_G, VEC_POOL_S = range(7)
VEC_ROWS = 8


def _mix_kernel(x_ref, prev_ref, next_ref, mod_ref, hf_ref, hb_ref, w_ref, b_ref, vec_ref, ws_ref, bst_ref,
                cw_ref, pw_ref, wbr_ref, wout_ref, g_ref, beta_ref, o_ref, xe_ref, a_ref, d_ref):
    j = pl.program_id(0) % NT_B
    is_ctx = j == 0
    first = jnp.logical_or(is_ctx, j == 1)
    last = jnp.logical_or(is_ctx, j == NT_B - 1)
    shift = mod_ref[3:4, :]
    scale = mod_ref[4:5, :]
    gate = mod_ref[5:6, :]

    def modulate(v):
        return (v * (1.0 + scale) + shift).astype(BF16)

    def vec(r):
        return vec_ref[r:r + 1, :]

    x = x_ref[...]
    xe_ref[0:HALO, :] = modulate(prev_ref[...])
    xe_ref[HALO:HALO + TM, :] = modulate(x)
    xe_ref[HALO + TM:2 * HALO + TM, :] = modulate(next_ref[...])
    xe = xe_ref[...]
    xm = xe_ref[HALO:HALO + TM, :]
    rows = lax.broadcasted_iota(jnp.int32, (TM + 2 * HALO, 1), 0)
    valid = jnp.logical_and(jnp.logical_or(rows >= HALO, jnp.logical_not(first)),
                            jnp.logical_or(rows < HALO + TM, jnp.logical_not(last)))

    def branch_gate(i):
        lo = MX_G + i * D_MODEL
        return _sigmoid(_dot(xm, w_ref[:, lo:lo + D_MODEL]) + b_ref[:, lo:lo + D_MODEL])

    def merged(i, y):
        return branch_gate(i) * _dot(y.astype(BF16), wbr_ref[i])

    pa = jax.nn.gelu(_dot(xm, w_ref[:, MX_A:MX_B]) + b_ref[:, MX_A:MX_B])
    u = pa[:, :MIX_W]
    vn = _ln(pa[:, MIX_W:], vec(VEC_GMLP_G), vec(VEC_GMLP_B)).astype(BF16)
    gd = MIX_W // GMLP_GROUPS
    z_rows = []
    for c in range(CH_T):
        z_cols = []
        for gi in range(GMLP_GROUPS):
            blk = vn[c * CHUNK:(c + 1) * CHUNK, gi * gd:(gi + 1) * gd]
            z_cols.append(_dot(ws_ref[gi], blk) + bst_ref[:, gi:gi + 1])
        z_rows.append(jnp.concatenate(z_cols, axis=1))
    acc = merged(0, u * jnp.concatenate(z_rows, axis=0))

    pb = _dot(xe, w_ref[:, MX_B:MX_O]) + b_ref[:, MX_B:MX_O]
    a_ref[...] = jnp.where(valid, pb[:, :MIX_W] * _sigmoid(pb[:, MIX_W:]), 0.0)
    conv = jnp.zeros((TM, MIX_W), F32)
    for k in range(CONV_W):
        lo = HALO - CONV_W // 2 + k
        conv = conv + cw_ref[k:k + 1, :] * a_ref[lo:lo + TM, :]
    yb = _silu(_ln(conv + vec(VEC_CONV_B), vec(VEC_CONV_G), vec(VEC_CONV_LB)))
    acc = acc + merged(1, yb)

    og = _sigmoid(_dot(xm, w_ref[:, MX_O:MX_D]) + b_ref[:, MX_O:MX_D])
    hsum = hf_ref[...] + hb_ref[...]
    hn = []
    for h in range(MLSTM_HEADS):
        hh = hsum[:, h * MLSTM_DH:(h + 1) * MLSTM_DH]
        mu = jnp.mean(hh, axis=-1, keepdims=True)
        hc = hh - mu
        hn.append(hc * lax.rsqrt(jnp.mean(hc * hc, axis=-1, keepdims=True) + LN_EPS))
    yc = og * (jnp.concatenate(hn, axis=1) * vec(VEC_MLSTM_G))
    acc = acc + merged(2, yc)

    d_ref[...] = jnp.where(valid, _dot(xe, w_ref[:, MX_D:MX_G]) + b_ref[:, MX_D:MX_G], 0.0)
    pos = jnp.where(is_ctx, 0, (j - 1) * TM) + lax.broadcasted_iota(jnp.int32, (TM, 1), 0)
    n_seq = jnp.where(is_ctx, CTX_LEN, SEQ)
    yd = []
    for gi, win in enumerate(POOL_WINDOWS):
        lo, hi = win // 2, win - 1 - win // 2
        cols = slice(gi * POOL_GD, (gi + 1) * POOL_GD)
        wsum = d_ref[HALO - lo:HALO - lo + TM, cols]
        for k in range(-lo + 1, hi + 1):
            wsum = wsum + d_ref[HALO + k:HALO + k + TM, cols]
        cnt = (jnp.minimum(pos + hi + 1, n_seq) - jnp.maximum(pos - lo, 0)).astype(F32)
        diff = wsum / cnt - d_ref[HALO:HALO + TM, cols]
        yd.append(_dot(diff.astype(BF16), pw_ref[gi]))
    acc = acc + merged(3, jnp.concatenate(yd, axis=1) * vec(VEC_POOL_S))

    y = _dot(acc.astype(BF16), wout_ref[...])
    o_ref[...] = _ln(ALPHA * x + gate * y, g_ref[...], beta_ref[...])


def _mix_call(s, mods, hf, hb, w_mx, b_mx, vec512, gmlp_ws, gmlp_bst, conv_w, pool_w, w_branch, w_out,
              ln_g, ln_b, *, layer):
    hb_per_tile = TM // HALO
    n_hb = T_ALL // HALO
    return pl.pallas_call(
        _mix_kernel,
        grid=(NT,),
        in_specs=[
            pl.BlockSpec((TM, D_MODEL), lambda i: (i, 0)),
            pl.BlockSpec((HALO, D_MODEL), lambda i: (jnp.maximum(i * hb_per_tile - 1, 0), 0)),
            pl.BlockSpec((HALO, D_MODEL), lambda i: (jnp.minimum((i + 1) * hb_per_tile, n_hb - 1), 0)),
            pl.BlockSpec((None, 9, D_MODEL), lambda i: (layer * MOD_ROWS + _mod_row(i), 0, 0)),
            pl.BlockSpec((TM, MIX_W), lambda i: (i, 0)),
            pl.BlockSpec((TM, MIX_W), lambda i: (i, 0)),
            _const_spec((None, D_MODEL, MX_COLS), (layer, 0, 0)),
            _const_spec((None, 1, MX_COLS), (layer, 0, 0)),
            _const_spec((None, VEC_ROWS, MIX_W), (layer, 0, 0)),
            _const_spec((None, GMLP_GROUPS, CHUNK, CHUNK), (layer, 0, 0, 0)),
            _const_spec((None, CHUNK, GMLP_GROUPS), (layer, 0, 0)),
            _const_spec((None, CONV_W, MIX_W), (layer, 0, 0)),
            _const_spec((None, len(POOL_WINDOWS), POOL_GD, POOL_GD), (layer, 0, 0, 0)),
            _const_spec((None, N_BRANCH, MIX_W, D_MODEL), (layer, 0, 0, 0)),
            _const_spec((None, D_MODEL, D_MODEL), (layer, 0, 0)),
            _const_spec((None, 1, D_MODEL), (layer * 3 + 1, 0, 0)),
            _const_spec((None, 1, D_MODEL), (layer * 3 + 1, 0, 0)),
        ],
        out_specs=pl.BlockSpec((TM, D_MODEL), lambda i: (i, 0)),
        out_shape=jax.ShapeDtypeStruct((T_ALL, D_MODEL), F32),
        scratch_shapes=[
            pltpu.VMEM((TM + 2 * HALO, D_MODEL), BF16),
            pltpu.VMEM((TM + 2 * HALO, MIX_W), F32),
            pltpu.VMEM((TM + 2 * HALO, MIX_W), F32),
        ],
        compiler_params=_params(("parallel",)),
        name="mix",
    )(s, s, s, mods, hf, hb, w_mx, b_mx, vec512, gmlp_ws, gmlp_bst, conv_w, pool_w, w_branch, w_out, ln_g, ln_b)


def _repack_cols(w, pieces):
    parts = [jnp.zeros(w.shape[:-1] + (p,), w.dtype) if isinstance(p, int) else w[..., p[0]:p[1]] for p in pieces]
    return jnp.concatenate(parts, axis=-1)


def kernel(x, c, ctx, c_ctx, w_ada, b_ada, ln_g, ln_b, ffn_w_in, ffn_w_out, w_in, b_in, gmlp_ln_g, gmlp_ln_b,
           gmlp_ws, gmlp_bs, conv_w, conv_b, conv_ln_g, conv_ln_b, qk_conv_w, mlstm_ln_g, pool_w, pool_scale,
           w_branch, w_out):
    mix_pieces = [(OFF_A, OFF_C), (OFF_C_O, OFF_D), (OFF_D, OFF_G), (OFF_G, IN_COLS)]
    nh, og = MLSTM_HEADS, OFF_C_GATES
    gate_pad = LANES - 2 * nh
    cp_pieces = [(OFF_C, OFF_C_GATES),
                 (og, og + nh), (og + 2 * nh, og + 3 * nh), gate_pad,
                 (og + nh, og + 2 * nh), (og + 3 * nh, og + 4 * nh), gate_pad]
    w_mx = _repack_cols(w_in, mix_pieces).astype(BF16)
    b_mx = _repack_cols(b_in, mix_pieces).reshape(DEPTH, 1, MX_COLS)
    w_cp = _repack_cols(w_in, cp_pieces).astype(BF16)
    b_cp = _repack_cols(b_in, cp_pieces).reshape(DEPTH, 1, CP_COLS)
    ffn_w_in_b = ffn_w_in.astype(BF16)
    ffn_w_out_b = ffn_w_out.astype(BF16)
    w_branch_b = w_branch.astype(BF16)
    w_out_b = w_out.astype(BF16)
    gmlp_ws_b = gmlp_ws.astype(BF16)
    pool_w_b = pool_w.astype(BF16)
    gmlp_bst = jnp.swapaxes(gmlp_bs, 1, 2)
    vec512 = jnp.stack([gmlp_ln_g, gmlp_ln_b, conv_b, conv_ln_g, conv_ln_b, mlstm_ln_g, pool_scale,
                        jnp.zeros_like(pool_scale)], axis=1)
    ln_g3 = ln_g.reshape(DEPTH * 3, 1, D_MODEL)
    ln_b3 = ln_b.reshape(DEPTH * 3, 1, D_MODEL)
    c_t = jnp.concatenate([c, c_ctx[None], jnp.zeros((MOD_ROWS - BATCH - 1, D_MODEL), F32)], axis=0).T

    mods = _mods_call(c_t, w_ada, b_ada).reshape(DEPTH * MOD_ROWS, 9, D_MODEL)
    s = _embed_call(x, ctx)
    for l in range(DEPTH):
        last = l == DEPTH - 1
        s = _ffn_call(s, mods, ffn_w_in_b, ffn_w_out_b, ln_g3, ln_b3, layer=l, which=0, final=False)
        q, kt, v, g, gt = _cproj_call(s, mods, w_cp, b_cp, qk_conv_w, layer=l)
        hf, hb = _mlstm_call(q, kt, v, g, gt)
        s = _mix_call(s, mods, hf, hb, w_mx, b_mx, vec512, gmlp_ws_b, gmlp_bst, conv_w, pool_w_b, w_branch_b,
                      w_out_b, ln_g3, ln_b3, layer=l)
        s = _ffn_call(s, mods, ffn_w_in_b, ffn_w_out_b, ln_g3, ln_b3, layer=l, which=1, final=last)
    return s.reshape(BATCH, SEQ, D_MODEL)
```

```python
import functools
import math

import jax
import jax.numpy as jnp
from jax import lax
from jax.experimental import pallas as pl
from jax.experimental.pallas import tpu as pltpu

D_MODEL = 1024
BATCH = 2
SEQ = 8192
DEPTH = 4
GRID_W = 64
CTX_LEN = 256
MIX_W = D_MODEL // 2
N_BRANCH = 4
CHUNK = 128
GMLP_GROUPS = 4
CONV_W = 31
MLSTM_HEADS = 4
MLSTM_DH = MIX_W // MLSTM_HEADS
QK_CONV = 3
POOL_WINDOWS = (2, 4, 8, 16)
POOL_GD = MIX_W // len(POOL_WINDOWS)
D_FF = 128 * ((8 * D_MODEL // 3 + 127) // 128)
ALPHA = (2 * DEPTH) ** 0.25
LN_EPS = 1e-6
FFN_RES = 0.5

OFF_A = 0
OFF_B = OFF_A + 2 * MIX_W
OFF_C = OFF_B + 2 * MIX_W
OFF_C_GATES = OFF_C + 3 * MIX_W
OFF_C_O = OFF_C_GATES + 4 * MLSTM_HEADS
OFF_D = OFF_C_O + MIX_W
OFF_G = OFF_D + MIX_W
IN_COLS = OFF_G + N_BRANCH * D_MODEL

LANES = 128
SUBLANES = 8
BF16_ROWS = 16
TM = 256
SEQ_B = CTX_LEN + SEQ
T_ALL = BATCH * SEQ_B
NT_B = SEQ_B // TM
NT = BATCH * NT_B
NCH_B = SEQ_B // CHUNK
CH_T = TM // CHUNK
CTX_CH = CTX_LEN // CHUNK
HALO = 16
FF_CH = 256
FFN_TILES = 2
CONV_RB = 64
GS_RMAX = 2 * MLSTM_HEADS
MOD_ROWS = 8
CTX_MOD_ROW = BATCH
VMEM_LIMIT = 56 * 1024 * 1024

MX_A = 0
MX_B = MX_A + 2 * MIX_W
MX_O = MX_B + 2 * MIX_W
MX_D = MX_O + MIX_W
MX_G = MX_D + MIX_W
MX_COLS = MX_G + N_BRANCH * D_MODEL
CP_QK = 0
CP_V = 2 * MIX_W
CP_GI = 3 * MIX_W
CP_GF = CP_GI + LANES
CP_COLS = CP_GF + LANES

F32 = jnp.float32
BF16 = jnp.bfloat16


def _dot(a, b):
    return jnp.dot(a, b, preferred_element_type=F32)


def _dot_f32(a, b):
    return jnp.dot(a, b, preferred_element_type=F32, precision=lax.Precision.HIGHEST)


def _ln(r, g, b):
    mu = jnp.mean(r, axis=-1, keepdims=True)
    xc = r - mu
    var = jnp.mean(xc * xc, axis=-1, keepdims=True)
    return xc * lax.rsqrt(var + LN_EPS) * g + b


def _sigmoid(x):
    return 1.0 / (1.0 + jnp.exp(-x))


def _silu(x):
    return x * _sigmoid(x)


def _mod_row(i):
    return jnp.where(i % NT_B == 0, CTX_MOD_ROW, i // NT_B)


def _const_spec(block, index):
    return pl.BlockSpec(block, lambda *_: index, pipeline_mode=pl.Buffered(1))


def _params(sem):
    return pltpu.CompilerParams(dimension_semantics=sem, vmem_limit_bytes=VMEM_LIMIT)


def _mods_kernel(ct_ref, w_ref, b_ref, o_ref):
    s = _silu(ct_ref[...])
    w = w_ref[...]
    o_ref[...] = jnp.zeros(o_ref.shape, F32)
    for r in range(BATCH + 1):
        o_ref[r:r + 1, :] = jnp.sum(s[:, r:r + 1] * w, axis=0, keepdims=True) + b_ref[...]


def _mods_call(c_t, w_ada, b_ada):
    tn = D_MODEL
    n_col = w_ada.shape[-1] // tn
    return pl.pallas_call(
        _mods_kernel,
        grid=(DEPTH, n_col),
        in_specs=[
            pl.BlockSpec((D_MODEL, MOD_ROWS), lambda l, n: (0, 0)),
            pl.BlockSpec((None, D_MODEL, tn), lambda l, n: (l, 0, n)),
            pl.BlockSpec((None, 1, tn), lambda l, n: (l, 0, n)),
        ],
        out_specs=pl.BlockSpec((None, MOD_ROWS, tn), lambda l, n: (l, 0, n)),
        out_shape=jax.ShapeDtypeStruct((DEPTH, MOD_ROWS, w_ada.shape[-1]), F32),
        compiler_params=_params(("parallel", "parallel")),
        name="mods",
    )(c_t, w_ada, b_ada.reshape(DEPTH, 1, -1))


def _embed_kernel(x_ref, ctx_ref, o_ref):
    j = pl.program_id(0) % NT_B

    @pl.when(j == 0)
    def _():
        o_ref[...] = ctx_ref[...]

    @pl.when(j > 0)
    def _():
        quarter = D_MODEL // 4
        t = (j - 1) * TM + lax.broadcasted_iota(jnp.int32, (TM, 1), 0)
        r = lax.shift_right_logical(t, GRID_W.bit_length() - 1).astype(F32)
        col = jnp.bitwise_and(t, GRID_W - 1).astype(F32)
        k = lax.broadcasted_iota(jnp.int32, (1, quarter), 1).astype(F32)
        freqs = jnp.exp(-math.log(10000.0) * k / quarter)
        er = r * freqs
        ec = col * freqs
        o_ref[:, 0 * quarter:1 * quarter] = x_ref[:, 0 * quarter:1 * quarter] + jnp.sin(er)
        o_ref[:, 1 * quarter:2 * quarter] = x_ref[:, 1 * quarter:2 * quarter] + jnp.cos(er)
        o_ref[:, 2 * quarter:3 * quarter] = x_ref[:, 2 * quarter:3 * quarter] + jnp.sin(ec)
        o_ref[:, 3 * quarter:4 * quarter] = x_ref[:, 3 * quarter:4 * quarter] + jnp.cos(ec)


def _embed_call(x, ctx):
    lat_tiles_b = SEQ // TM
    return pl.pallas_call(
        _embed_kernel,
        grid=(NT,),
        in_specs=[
            pl.BlockSpec((TM, D_MODEL), lambda i: ((i // NT_B) * lat_tiles_b + jnp.maximum(i % NT_B - 1, 0), 0)),
            pl.BlockSpec((TM, D_MODEL), lambda i: (i // NT_B, 0)),
        ],
        out_specs=pl.BlockSpec((TM, D_MODEL), lambda i: (i, 0)),
        out_shape=jax.ShapeDtypeStruct((T_ALL, D_MODEL), F32),
        compiler_params=_params(("parallel",)),
        name="embed",
    )(x.reshape(BATCH * SEQ, D_MODEL), ctx.reshape(BATCH * CTX_LEN, D_MODEL))


def _ffn_kernel(*refs, sub):
    x_refs = refs[:FFN_TILES]
    mod_refs = refs[FFN_TILES:2 * FFN_TILES]
    win_ref, wout_ref, g_ref, b_ref, o_ref, xm_ref, h_ref = refs[2 * FFN_TILES:]
    for t, (x_ref, mod_ref) in enumerate(zip(x_refs, mod_refs)):
        shift = mod_ref[3 * sub + 0:3 * sub + 1, :]
        scale = mod_ref[3 * sub + 1:3 * sub + 2, :]
        xm_ref[t * TM:(t + 1) * TM, :] = (x_ref[...] * (1.0 + scale) + shift).astype(BF16)
    xm = xm_ref[...]
    for c in range(D_FF // FF_CH):
        a1 = _dot(xm, win_ref[:, c * FF_CH:(c + 1) * FF_CH])
        a2 = _dot(xm, win_ref[:, D_FF + c * FF_CH:D_FF + (c + 1) * FF_CH])
        h_ref[:, c * FF_CH:(c + 1) * FF_CH] = (_silu(a1) * a2).astype(BF16)
    y = _dot(h_ref[...], wout_ref[...])
    for t, (x_ref, mod_ref) in enumerate(zip(x_refs, mod_refs)):
        gate = mod_ref[3 * sub + 2:3 * sub + 3, :]
        r = ALPHA * x_ref[...] + (FFN_RES * gate) * y[t * TM:(t + 1) * TM, :]
        o_ref[t * TM:(t + 1) * TM, :] = _ln(r, g_ref[...], b_ref[...])


def _ffn_call(s, mods, ffn_w_in, ffn_w_out, ln_g, ln_b, *, layer, which, final):
    sub = 2 * which
    if final:
        lat_tiles_b = SEQ // TM
        out_tiles = BATCH * lat_tiles_b
        in_tile = lambda t: (t // lat_tiles_b) * NT_B + 1 + t % lat_tiles_b
    else:
        out_tiles = NT
        in_tile = lambda t: t
    tile_of = lambda i, t: in_tile(i * FFN_TILES + t)
    x_specs = [pl.BlockSpec((TM, D_MODEL), functools.partial(lambda i, t: (tile_of(i, t), 0), t=t))
               for t in range(FFN_TILES)]
    mod_specs = [pl.BlockSpec((None, 9, D_MODEL),
                              functools.partial(lambda i, t: (layer * MOD_ROWS + _mod_row(tile_of(i, t)), 0, 0), t=t))
                 for t in range(FFN_TILES)]
    return pl.pallas_call(
        functools.partial(_ffn_kernel, sub=sub),
        grid=(out_tiles // FFN_TILES,),
        in_specs=x_specs + mod_specs + [
            _const_spec((None, None, D_MODEL, 2 * D_FF), (layer, which, 0, 0)),
            _const_spec((None, None, D_FF, D_MODEL), (layer, which, 0, 0)),
            _const_spec((None, 1, D_MODEL), (layer * 3 + sub, 0, 0)),
            _const_spec((None, 1, D_MODEL), (layer * 3 + sub, 0, 0)),
        ],
        out_specs=pl.BlockSpec((FFN_TILES * TM, D_MODEL), lambda i: (i, 0)),
        out_shape=jax.ShapeDtypeStruct((out_tiles * TM, D_MODEL), F32),
        scratch_shapes=[
            pltpu.VMEM((FFN_TILES * TM, D_MODEL), BF16),
            pltpu.VMEM((FFN_TILES * TM, D_FF), BF16),
        ],
        compiler_params=_params(("parallel",)),
        name=f"ffn{which}",
    )(*([s] * FFN_TILES), *([mods] * FFN_TILES), ffn_w_in, ffn_w_out, ln_g, ln_b)


def _cproj_kernel(x_ref, prev_ref, next_ref, mod_ref, w_ref, b_ref, cw_ref,
                  q_ref, kt_ref, v_ref, g_ref, gt_ref, xe_ref, p_ref):
    j = pl.program_id(0) % NT_B
    first = jnp.logical_or(j == 0, j == 1)
    last = jnp.logical_or(j == 0, j == NT_B - 1)
    shift = mod_ref[3:4, :]
    scale = mod_ref[4:5, :]

    def modulate(v):
        return (v * (1.0 + scale) + shift).astype(BF16)

    xe_ref[0:HALO, :] = modulate(prev_ref[...])
    xe_ref[HALO:HALO + TM, :] = modulate(x_ref[...])
    xe_ref[HALO + TM:2 * HALO + TM, :] = modulate(next_ref[...])
    xe = xe_ref[...]
    rows = lax.broadcasted_iota(jnp.int32, (TM + 2 * HALO, 1), 0)
    valid = jnp.logical_and(jnp.logical_or(rows >= HALO, jnp.logical_not(first)),
                            jnp.logical_or(rows < HALO + TM, jnp.logical_not(last)))
    p_ref[...] = jnp.where(valid, _dot(xe, w_ref[:, CP_QK:CP_V]) + b_ref[:, CP_QK:CP_V], 0.0)
    conv = (cw_ref[0:1, :] * p_ref[HALO - 1:HALO - 1 + TM, :]
            + cw_ref[1:2, :] * p_ref[HALO:HALO + TM, :]
            + cw_ref[2:3, :] * p_ref[HALO + 1:HALO + 1 + TM, :])
    qk = _silu(conv)
    q_ref[...] = qk[:, :MIX_W].astype(BF16)
    k = qk[:, MIX_W:] * MLSTM_DH ** -0.5
    for c in range(CH_T):
        for h in range(MLSTM_HEADS):
            blk = k[c * CHUNK:(c + 1) * CHUNK, h * MLSTM_DH:(h + 1) * MLSTM_DH]
            r0 = (c * MLSTM_HEADS + h) * MLSTM_DH
            kt_ref[r0:r0 + MLSTM_DH, :] = blk.T.astype(BF16)

    xm = xe_ref[HALO:HALO + TM, :]
    pvg = _dot(xm, w_ref[:, CP_V:CP_COLS]) + b_ref[:, CP_V:CP_COLS]
    v_ref[...] = pvg[:, :MIX_W].astype(BF16)
    li = pvg[:, CP_GI - CP_V:CP_GF - CP_V]
    fraw = pvg[:, CP_GF - CP_V:]
    lf = jnp.minimum(fraw, 0.0) - jnp.log1p(jnp.exp(-jnp.abs(fraw)))
    row = lax.broadcasted_iota(jnp.int32, (CHUNK, LANES), 0)
    lane = lax.broadcasted_iota(jnp.int32, (CHUNK, LANES), 1)
    fwd_lane = lane < MLSTM_HEADS
    lower_f = (lane <= row).astype(F32)
    upper_f = (lane >= row).astype(F32)
    for c in range(CH_T):
        tok = slice(c * CHUNK, (c + 1) * CHUNK)
        cum = jnp.where(fwd_lane, _dot_f32(lower_f, lf[tok]), _dot_f32(upper_f, lf[tok]))
        r = li[tok] - cum
        rf, rb = r, r
        k = 1
        while k < CHUNK:
            rf = jnp.maximum(rf, jnp.where(row >= k, pltpu.roll(rf, k, 0), -jnp.inf))
            rb = jnp.maximum(rb, jnp.where(row < CHUNK - k, pltpu.roll(rb, CHUNK - k, 0), -jnp.inf))
            k *= 2
        rmax = jnp.where(fwd_lane, rf, rb)
        g_ref[tok, :] = jnp.where(lane < GS_RMAX, cum, pltpu.roll(rmax, GS_RMAX, 1))
        gt_ref[c * SUBLANES:(c + 1) * SUBLANES, :] = r.T[:SUBLANES, :]


def _cproj_call(s, mods, w_cp, b_cp, qk_conv_w, *, layer):
    hb = TM // HALO
    n_hb = T_ALL // HALO
    return pl.pallas_call(
        _cproj_kernel,
        grid=(NT,),
        in_specs=[
            pl.BlockSpec((TM, D_MODEL), lambda i: (i, 0)),
            pl.BlockSpec((HALO, D_MODEL), lambda i: (jnp.maximum(i * hb - 1, 0), 0)),
            pl.BlockSpec((HALO, D_MODEL), lambda i: (jnp.minimum((i + 1) * hb, n_hb - 1), 0)),
            pl.BlockSpec((None, 9, D_MODEL), lambda i: (layer * MOD_ROWS + _mod_row(i), 0, 0)),
            _const_spec((None, D_MODEL, CP_COLS), (layer, 0, 0)),
            _const_spec((None, 1, CP_COLS), (layer, 0, 0)),
            _const_spec((None, QK_CONV, 2 * MIX_W), (layer, 0, 0)),
        ],
        out_specs=[
            pl.BlockSpec((TM, MIX_W), lambda i: (i, 0)),
            pl.BlockSpec((TM * MLSTM_HEADS, MLSTM_DH), lambda i: (i, 0)),
            pl.BlockSpec((TM, MIX_W), lambda i: (i, 0)),
            pl.BlockSpec((TM, LANES), lambda i: (i, 0)),
            pl.BlockSpec((CH_T * SUBLANES, CHUNK), lambda i: (i, 0)),
        ],
        out_shape=[
            jax.ShapeDtypeStruct((T_ALL, MIX_W), BF16),
            jax.ShapeDtypeStruct((T_ALL * MLSTM_HEADS, MLSTM_DH), BF16),
            jax.ShapeDtypeStruct((T_ALL, MIX_W), BF16),
            jax.ShapeDtypeStruct((T_ALL, LANES), F32),
            jax.ShapeDtypeStruct((T_ALL // CHUNK * SUBLANES, CHUNK), F32),
        ],
        scratch_shapes=[
            pltpu.VMEM((TM + 2 * HALO, D_MODEL), BF16),
            pltpu.VMEM((TM + 2 * HALO, 2 * MIX_W), F32),
        ],
        compiler_params=_params(("parallel",)),
        name="cproj",
    )(s, s, s, mods, w_cp, b_cp, qk_conv_w)


def _mlstm_kernel(qf_ref, ktf_ref, vf_ref, gf_ref, gtf_ref, qb_ref, ktb_ref, vb_ref, gb_ref, gtb_ref,
                  hf_ref, hb_ref, c_ref, m_ref):
    @pl.when(pl.program_id(0) == 0)
    def _():
        c_ref[...] = jnp.zeros(c_ref.shape, F32)
        m_ref[...] = jnp.zeros(m_ref.shape, F32)

    row = lax.broadcasted_iota(jnp.int32, (CHUNK, CHUNK), 0)
    col = lax.broadcasted_iota(jnp.int32, (CHUNK, CHUNK), 1)
    ones_col = jnp.ones((CHUNK, MLSTM_DH), BF16)
    nh = MLSTM_HEADS
    dirs = ((qf_ref, ktf_ref, vf_ref, gf_ref, gtf_ref, hf_ref, col <= row, CHUNK - 1),
            (qb_ref, ktb_ref, vb_ref, gb_ref, gtb_ref, hb_ref, col >= row, 0))
    heads = []
    for d, (q_ref, kt_ref, v_ref, g_ref, gt_ref, h_ref, seen, end_row) in enumerate(dirs):
        for b in range(BATCH):
            g = g_ref[b]
            gt = gt_ref[b]
            for h in range(nh):
                lane = d * nh + h
                hd = slice(h * MLSTM_DH, (h + 1) * MLSTM_DH)
                heads.append(dict(
                    idx=(d * BATCH + b) * nh + h, seen=seen, end_row=end_row, out=(h_ref, b, hd),
                    cum=g[:, lane:lane + 1], rmax=g[:, GS_RMAX + lane:GS_RMAX + lane + 1], r=gt[lane:lane + 1, :],
                    q=q_ref[b, :, hd], kt=kt_ref[b, hd, :], v=v_ref[b, :, hd]))

    for hd in heads:
        hd["qk"] = _dot(hd["q"], hd["kt"])
        hd["m_prev"] = m_ref[hd["idx"], 0:1, 0:1]
        hd["c_old"] = c_ref[hd["idx"]]
        hd["v_aug"] = jnp.concatenate([hd["v"], ones_col], axis=1)
    for hd in heads:
        e = hd["end_row"]
        top = jnp.maximum(hd["m_prev"], hd["rmax"][e:e + 1, :])
        w_row = jnp.exp(hd["r"] - top)
        ktw = (hd["kt"].astype(F32) * w_row).astype(BF16)
        c_ref[hd["idx"]] = jnp.exp(hd["m_prev"] - top) * hd["c_old"] + _dot(ktw, hd["v_aug"])
        m_ref[hd["idx"]] = jnp.broadcast_to(hd["cum"][e:e + 1, :] + top, m_ref.shape[1:])
    for hd in heads:
        m_prev = hd["m_prev"]
        top = jnp.broadcast_to(jnp.maximum(m_prev, hd["rmax"]), (CHUNK, CHUNK))
        s = hd["qk"] * jnp.exp(jnp.where(hd["seen"], hd["r"] - top, -jnp.inf))
        qw = hd["q"].astype(F32) * jnp.exp(m_prev - top)
        lhs = jnp.concatenate([s.astype(BF16), qw.astype(BF16)], axis=1)
        rhs = jnp.concatenate([hd["v_aug"], hd["c_old"].astype(BF16)], axis=0)
        hd["na"] = _dot(lhs, rhs)
        hd["floor"] = jnp.exp(-(hd["cum"] + top))
    for hd in heads:
        den = jnp.maximum(jnp.abs(hd["na"][:, MLSTM_DH:]), hd["floor"])
        h_ref, b, cols = hd["out"]
        h_ref[b, :, cols] = hd["na"][:, :MLSTM_DH] / den


def _mlstm_call(q, kt, v, g, gt):
    fwd = lambda s: s
    bwd = lambda s: jnp.where(s < CTX_CH, CTX_CH - 1 - s, NCH_B + CTX_CH - 1 - s)
    q3 = q.reshape(BATCH, SEQ_B, MIX_W)
    kt3 = kt.reshape(BATCH, SEQ_B * MLSTM_HEADS, MLSTM_DH)
    v3 = v.reshape(BATCH, SEQ_B, MIX_W)
    g3 = g.reshape(BATCH, SEQ_B, LANES)
    gt3 = gt.reshape(BATCH, NCH_B * SUBLANES, CHUNK)

    def specs(order):
        return [
            pl.BlockSpec((BATCH, CHUNK, MIX_W), lambda s: (0, order(s), 0)),
            pl.BlockSpec((BATCH, CHUNK * MLSTM_HEADS, MLSTM_DH), lambda s: (0, order(s), 0)),
            pl.BlockSpec((BATCH, CHUNK, MIX_W), lambda s: (0, order(s), 0)),
            pl.BlockSpec((BATCH, CHUNK, LANES), lambda s: (0, order(s), 0)),
            pl.BlockSpec((BATCH, SUBLANES, CHUNK), lambda s: (0, order(s), 0)),
        ]

    n_state = 2 * BATCH * MLSTM_HEADS
    hf, hb = pl.pallas_call(
        _mlstm_kernel,
        grid=(NCH_B,),
        in_specs=specs(fwd) + specs(bwd),
        out_specs=[
            pl.BlockSpec((BATCH, CHUNK, MIX_W), lambda s: (0, fwd(s), 0)),
            pl.BlockSpec((BATCH, CHUNK, MIX_W), lambda s: (0, bwd(s), 0)),
        ],
        out_shape=[jax.ShapeDtypeStruct((BATCH, SEQ_B, MIX_W), F32)] * 2,
        scratch_shapes=[
            pltpu.VMEM((n_state, MLSTM_DH, 2 * MLSTM_DH), F32),
            pltpu.VMEM((n_state, SUBLANES, LANES), F32),
        ],
        compiler_params=_params(("arbitrary",)),
        name="mlstm",
    )(q3, kt3, v3, g3, gt3, q3, kt3, v3, g3, gt3)
    return hf.reshape(T_ALL, MIX_W), hb.reshape(T_ALL, MIX_W)


VEC_GMLP_G, VEC_GMLP_B, VEC_CONV_B, VEC_CONV_G, VEC_CONV_LB, VEC_MLSTM_G, VEC_POOL_S = range(7)
VEC_ROWS = 8


def _mix_kernel(x_ref, prev_ref, next_ref, mod_ref, hf_ref, hb_ref, w_ref, b_ref, vec_ref, ws_ref, bst_ref,
                cw_ref, pw_ref, wbr_ref, wout_ref, g_ref, beta_ref, o_ref, xe_ref, a_ref, d_ref, ash_ref, conv_ref):
    j = pl.program_id(0) % NT_B
    is_ctx = j == 0
    first = jnp.logical_or(is_ctx, j == 1)
    last = jnp.logical_or(is_ctx, j == NT_B - 1)
    shift = mod_ref[3:4, :]
    scale = mod_ref[4:5, :]
    gate = mod_ref[5:6, :]

    def modulate(v):
        return (v * (1.0 + scale) + shift).astype(BF16)

    def vec(r):
        return vec_ref[r:r + 1, :]

    x = x_ref[...]
    xe_ref[0:HALO, :] = modulate(prev_ref[...])
    xe_ref[HALO:HALO + TM, :] = modulate(x)
    xe_ref[HALO + TM:2 * HALO + TM, :] = modulate(next_ref[...])
    xe = xe_ref[...]
    xm = xe_ref[HALO:HALO + TM, :]
    rows = lax.broadcasted_iota(jnp.int32, (TM + 2 * HALO, 1), 0)
    valid = jnp.logical_and(jnp.logical_or(rows >= HALO, jnp.logical_not(first)),
                            jnp.logical_or(rows < HALO + TM, jnp.logical_not(last)))

    def branch_gate(i):
        lo = MX_G + i * D_MODEL
        return _sigmoid(_dot(xm, w_ref[:, lo:lo + D_MODEL]) + b_ref[:, lo:lo + D_MODEL])

    def merged(i, y):
        return branch_gate(i) * _dot(y.astype(BF16), wbr_ref[i])

    pa = jax.nn.gelu(_dot(xm, w_ref[:, MX_A:MX_B]) + b_ref[:, MX_A:MX_B])
    u = pa[:, :MIX_W]
    vn = _ln(pa[:, MIX_W:], vec(VEC_GMLP_G), vec(VEC_GMLP_B)).astype(BF16)
    gd = MIX_W // GMLP_GROUPS
    z_rows = []
    for c in range(CH_T):
        z_cols = []
        for gi in range(GMLP_GROUPS):
            blk = vn[c * CHUNK:(c + 1) * CHUNK, gi * gd:(gi + 1) * gd]
            z_cols.append(_dot(ws_ref[gi], blk) + bst_ref[:, gi:gi + 1])
        z_rows.append(jnp.concatenate(z_cols, axis=1))
    acc = merged(0, u * jnp.concatenate(z_rows, axis=0))

    pb = _dot(xe, w_ref[:, MX_B:MX_O]) + b_ref[:, MX_B:MX_O]
    a_ref[...] = jnp.where(valid, pb[:, :MIX_W] * _sigmoid(pb[:, MIX_W:]), 0.0)
    sh_rows = TM + 2 * HALO - SUBLANES
    for s in range(1, SUBLANES):
        ash_ref[s - 1, 0:sh_rows, :] = a_ref[s:s + sh_rows, :]
    for cb in range(MIX_W // LANES):
        cols = slice(cb * LANES, (cb + 1) * LANES)
        for rb in range(TM // CONV_RB):
            part = jnp.zeros((CONV_RB, LANES), F32)
            for k in range(CONV_W):
                q8, s = divmod(HALO - CONV_W // 2 + k, SUBLANES)
                r0 = q8 * SUBLANES + rb * CONV_RB
                src = a_ref[r0:r0 + CONV_RB, cols] if s == 0 else ash_ref[s - 1, r0:r0 + CONV_RB, cols]
                part = part + cw_ref[k:k + 1, cols] * src
            conv_ref[rb * CONV_RB:(rb + 1) * CONV_RB, cols] = part
    yb = _silu(_ln(conv_ref[...] + vec(VEC_CONV_B), vec(VEC_CONV_G), vec(VEC_CONV_LB)))
    acc = acc + merged(1, yb)

    og = _sigmoid(_dot(xm, w_ref[:, MX_O:MX_D]) + b_ref[:, MX_O:MX_D])
    hsum = hf_ref[...] + hb_ref[...]
    hn = []
    for h in range(MLSTM_HEADS):
        hh = hsum[:, h * MLSTM_DH:(h + 1) * MLSTM_DH]
        mu = jnp.mean(hh, axis=-1, keepdims=True)
        hc = hh - mu
        hn.append(hc * lax.rsqrt(jnp.mean(hc * hc, axis=-1, keepdims=True) + LN_EPS))
    yc = og * (jnp.concatenate(hn, axis=1) * vec(VEC_MLSTM_G))
    acc = acc + merged(2, yc)

    d_ref[...] = jnp.where(valid, _dot(xe, w_ref[:, MX_D:MX_G]) + b_ref[:, MX_D:MX_G], 0.0)
    pos = jnp.where(is_ctx, 0, (j - 1) * TM) + lax.broadcasted_iota(jnp.int32, (TM, 1), 0)
    n_seq = jnp.where(is_ctx, CTX_LEN, SEQ)
    yd = []
    for gi, win in enumerate(POOL_WINDOWS):
        lo, hi = win // 2, win - 1 - win // 2
        cols = slice(gi * POOL_GD, (gi + 1) * POOL_GD)
        wsum = d_ref[HALO - lo:HALO - lo + TM, cols]
        for k in range(-lo + 1, hi + 1):
            wsum = wsum + d_ref[HALO + k:HALO + k + TM, cols]
        cnt = (jnp.minimum(pos + hi + 1, n_seq) - jnp.maximum(pos - lo, 0)).astype(F32)
        diff = wsum / cnt - d_ref[HALO:HALO + TM, cols]
        yd.append(_dot(diff.astype(BF16), pw_ref[gi]))
    acc = acc + merged(3, jnp.concatenate(yd, axis=1) * vec(VEC_POOL_S))

    y = _dot(acc.astype(BF16), wout_ref[...])
    o_ref[...] = _ln(ALPHA * x + gate * y, g_ref[...], beta_ref[...])


def _mix_call(s, mods, hf, hb, w_mx, b_mx, vec512, gmlp_ws, gmlp_bst, conv_w, pool_w, w_branch, w_out,
              ln_g, ln_b, *, layer):
    hb_per_tile = TM // HALO
    n_hb = T_ALL // HALO
    return pl.pallas_call(
        _mix_kernel,
        grid=(NT,),
        in_specs=[
            pl.BlockSpec((TM, D_MODEL), lambda i: (i, 0)),
            pl.BlockSpec((HALO, D_MODEL), lambda i: (jnp.maximum(i * hb_per_tile - 1, 0), 0)),
            pl.BlockSpec((HALO, D_MODEL), lambda i: (jnp.minimum((i + 1) * hb_per_tile, n_hb - 1), 0)),
            pl.BlockSpec((None, 9, D_MODEL), lambda i: (layer * MOD_ROWS + _mod_row(i), 0, 0)),
            pl.BlockSpec((TM, MIX_W), lambda i: (i, 0)),
            pl.BlockSpec((TM, MIX_W), lambda i: (i, 0)),
            _const_spec((None, D_MODEL, MX_COLS), (layer, 0, 0)),
            _const_spec((None, 1, MX_COLS), (layer, 0, 0)),
            _const_spec((None, VEC_ROWS, MIX_W), (layer, 0, 0)),
            _const_spec((None, GMLP_GROUPS, CHUNK, CHUNK), (layer, 0, 0, 0)),
            _const_spec((None, CHUNK, GMLP_GROUPS), (layer, 0, 0)),
            _const_spec((None, CONV_W, MIX_W), (layer, 0, 0)),
            _const_spec((None, len(POOL_WINDOWS), POOL_GD, POOL_GD), (layer, 0, 0, 0)),
            _const_spec((None, N_BRANCH, MIX_W, D_MODEL), (layer, 0, 0, 0)),
            _const_spec((None, D_MODEL, D_MODEL), (layer, 0, 0)),
            _const_spec((None, 1, D_MODEL), (layer * 3 + 1, 0, 0)),
            _const_spec((None, 1, D_MODEL), (layer * 3 + 1, 0, 0)),
        ],
        out_specs=pl.BlockSpec((TM, D_MODEL), lambda i: (i, 0)),
        out_shape=jax.ShapeDtypeStruct((T_ALL, D_MODEL), F32),
        scratch_shapes=[
            pltpu.VMEM((TM + 2 * HALO, D_MODEL), BF16),
            pltpu.VMEM((TM + 2 * HALO, MIX_W), F32),
            pltpu.VMEM((TM + 2 * HALO, MIX_W), F32),
            pltpu.VMEM((SUBLANES - 1, TM + 2 * HALO, MIX_W), F32),
            pltpu.VMEM((TM, MIX_W), F32),
        ],
        compiler_params=_params(("parallel",)),
        name="mix",
    )(s, s, s, mods, hf, hb, w_mx, b_mx, vec512, gmlp_ws, gmlp_bst, conv_w, pool_w, w_branch, w_out, ln_g, ln_b)


def _repack_cols(w, pieces):
    parts = [jnp.zeros(w.shape[:-1] + (p,), w.dtype) if isinstance(p, int) else w[..., p[0]:p[1]] for p in pieces]
    return jnp.concatenate(parts, axis=-1)


def kernel(x, c, ctx, c_ctx, w_ada, b_ada, ln_g, ln_b, ffn_w_in, ffn_w_out, w_in, b_in, gmlp_ln_g, gmlp_ln_b,
           gmlp_ws, gmlp_bs, conv_w, conv_b, conv_ln_g, conv_ln_b, qk_conv_w, mlstm_ln_g, pool_w, pool_scale,
           w_branch, w_out):
    mix_pieces = [(OFF_A, OFF_C), (OFF_C_O, OFF_D), (OFF_D, OFF_G), (OFF_G, IN_COLS)]
    nh, og = MLSTM_HEADS, OFF_C_GATES
    gate_pad = LANES - 2 * nh
    cp_pieces = [(OFF_C, OFF_C_GATES),
                 (og, og + nh), (og + 2 * nh, og + 3 * nh), gate_pad,
                 (og + nh, og + 2 * nh), (og + 3 * nh, og + 4 * nh), gate_pad]
    w_mx = _repack_cols(w_in, mix_pieces).astype(BF16)
    b_mx = _repack_cols(b_in, mix_pieces).reshape(DEPTH, 1, MX_COLS)
    w_cp = _repack_cols(w_in, cp_pieces).astype(BF16)
    b_cp = _repack_cols(b_in, cp_pieces).reshape(DEPTH, 1, CP_COLS)
    ffn_w_in_b = ffn_w_in.astype(BF16)
    ffn_w_out_b = ffn_w_out.astype(BF16)
    w_branch_b = w_branch.astype(BF16)
    w_out_b = w_out.astype(BF16)
    gmlp_ws_b = gmlp_ws.astype(BF16)
    pool_w_b = pool_w.astype(BF16)
    gmlp_bst = jnp.swapaxes(gmlp_bs, 1, 2)
    vec512 = jnp.stack([gmlp_ln_g, gmlp_ln_b, conv_b, conv_ln_g, conv_ln_b, mlstm_ln_g, pool_scale,
                        jnp.zeros_like(pool_scale)], axis=1)
    ln_g3 = ln_g.reshape(DEPTH * 3, 1, D_MODEL)
    ln_b3 = ln_b.reshape(DEPTH * 3, 1, D_MODEL)
    c_t = jnp.concatenate([c, c_ctx[None], jnp.zeros((MOD_ROWS - BATCH - 1, D_MODEL), F32)], axis=0).T

    mods = _mods_call(c_t, w_ada, b_ada).reshape(DEPTH * MOD_ROWS, 9, D_MODEL)
    s = _embed_call(x, ctx)
    for l in range(DEPTH):
        last = l == DEPTH - 1
        s = _ffn_call(s, mods, ffn_w_in_b, ffn_w_out_b, ln_g3, ln_b3, layer=l, which=0, final=False)
        q, kt, v, g, gt = _cproj_call(s, mods, w_cp, b_cp, qk_conv_w, layer=l)
        hf, hb = _mlstm_call(q, kt, v, g, gt)
        s = _mix_call(s, mods, hf, hb, w_mx, b_mx, vec512, gmlp_ws_b, gmlp_bst, conv_w, pool_w_b, w_branch_b,
                      w_out_b, ln_g3, ln_b3, layer=l)
        s = _ffn_call(s, mods, ffn_w_in_b, ffn_w_out_b, ln_g3, ln_b3, layer=l, which=1, final=last)
    return s.reshape(BATCH, SEQ, D_MODEL)
```

```python
import functools
import math

import jax
import jax.numpy as jnp
from jax import lax
from jax.experimental import pallas as pl
from jax.experimental.pallas import tpu as pltpu

D_MODEL = 1024
BATCH = 2
SEQ = 8192
DEPTH = 4
GRID_W = 64
CTX_LEN = 256
MIX_W = D_MODEL // 2
N_BRANCH = 4
CHUNK = 128
GMLP_GROUPS = 4
CONV_W = 31
MLSTM_HEADS = 4
MLSTM_DH = MIX_W // MLSTM_HEADS
QK_CONV = 3
POOL_WINDOWS = (2, 4, 8, 16)
POOL_GD = MIX_W // len(POOL_WINDOWS)
D_FF = 128 * ((8 * D_MODEL // 3 + 127) // 128)
ALPHA = (2 * DEPTH) ** 0.25
LN_EPS = 1e-6
FFN_RES = 0.5

OFF_A = 0
OFF_B = OFF_A + 2 * MIX_W
OFF_C = OFF_B + 2 * MIX_W
OFF_C_GATES = OFF_C + 3 * MIX_W
OFF_C_O = OFF_C_GATES + 4 * MLSTM_HEADS
OFF_D = OFF_C_O + MIX_W
OFF_G = OFF_D + MIX_W
IN_COLS = OFF_G + N_BRANCH * D_MODEL

LANES = 128
SUBLANES = 8
BF16_ROWS = 16
TM = 256
SEQ_B = CTX_LEN + SEQ
T_ALL = BATCH * SEQ_B
NT_B = SEQ_B // TM
NT = BATCH * NT_B
NCH_B = SEQ_B // CHUNK
CH_T = TM // CHUNK
CTX_CH = CTX_LEN // CHUNK
HALO = 16
MXU_N = 256
LANE_PAD = LANES
FF_CH = MXU_N
FFN_TILES = 2
CONV_RB = 64
GS_RMAX = 2 * MLSTM_HEADS
MOD_ROWS = 8
CTX_MOD_ROW = BATCH
VMEM_LIMIT = 56 * 1024 * 1024

MX_A = 0
MX_B = MX_A + 2 * MIX_W
MX_O = MX_B + 2 * MIX_W
MX_D = MX_O + MIX_W
MX_G = MX_D + MIX_W
MX_COLS = MX_G + N_BRANCH * D_MODEL
CP_QK = 0
CP_V = 2 * MIX_W
CP_GI = 3 * MIX_W
CP_GF = CP_GI + LANES
CP_COLS = CP_GF + LANES

F32 = jnp.float32
BF16 = jnp.bfloat16


def _dot(a, b):
    return jnp.dot(a, b, preferred_element_type=F32)


def _dot_f32(a, b):
    return jnp.dot(a, b, preferred_element_type=F32, precision=lax.Precision.HIGHEST)


def _ln(r, g, b):
    mu = jnp.mean(r, axis=-1, keepdims=True)
    xc = r - mu
    var = jnp.mean(xc * xc, axis=-1, keepdims=True)
    return xc * lax.rsqrt(var + LN_EPS) * g + b


def _sigmoid(x):
    return 0.5 * jnp.tanh(0.5 * x) + 0.5


def _silu(x):
    hx = 0.5 * x
    return hx * jnp.tanh(hx) + hx


def _mod_row(i):
    return jnp.where(i % NT_B == 0, CTX_MOD_ROW, i // NT_B)


def _const_spec(block, index):
    return pl.BlockSpec(block, lambda *_: index, pipeline_mode=pl.Buffered(1))


def _params(sem, flags=None):
    return pltpu.CompilerParams(dimension_semantics=sem, vmem_limit_bytes=VMEM_LIMIT, flags=flags)


def _mods_kernel(ct_ref, w_ref, b_ref, o_ref):
    s = _silu(ct_ref[...])
    w = w_ref[...]
    o_ref[...] = jnp.zeros(o_ref.shape, F32)
    for r in range(BATCH + 1):
        o_ref[r:r + 1, :] = jnp.sum(s[:, r:r + 1] * w, axis=0, keepdims=True) + b_ref[...]


def _mods_call(c_t, w_ada, b_ada):
    tn = D_MODEL
    n_col = w_ada.shape[-1] // tn
    return pl.pallas_call(
        _mods_kernel,
        grid=(DEPTH, n_col),
        in_specs=[
            pl.BlockSpec((D_MODEL, MOD_ROWS), lambda l, n: (0, 0)),
            pl.BlockSpec((None, D_MODEL, tn), lambda l, n: (l, 0, n)),
            pl.BlockSpec((None, 1, tn), lambda l, n: (l, 0, n)),
        ],
        out_specs=pl.BlockSpec((None, MOD_ROWS, tn), lambda l, n: (l, 0, n)),
        out_shape=jax.ShapeDtypeStruct((DEPTH, MOD_ROWS, w_ada.shape[-1]), F32),
        compiler_params=_params(("parallel", "parallel")),
        name="mods",
    )(c_t, w_ada, b_ada.reshape(DEPTH, 1, -1))


def _embed_kernel(x_ref, ctx_ref, o_ref):
    j = pl.program_id(0) % NT_B

    @pl.when(j == 0)
    def _():
        o_ref[...] = ctx_ref[...]

    @pl.when(j > 0)
    def _():
        quarter = D_MODEL // 4
        grid_rows = TM // GRID_W
        k = lax.broadcasted_iota(jnp.int32, (1, quarter), 1).astype(F32)
        freqs = jnp.exp(-math.log(10000.0) * k / quarter)
        r = ((j - 1) * grid_rows + lax.broadcasted_iota(jnp.int32, (SUBLANES, 1), 0)).astype(F32)
        col = lax.broadcasted_iota(jnp.int32, (GRID_W, 1), 0).astype(F32)
        er = r * freqs
        ec = col * freqs
        sin_r, cos_r, sin_c, cos_c = jnp.sin(er), jnp.cos(er), jnp.sin(ec), jnp.cos(ec)
        for q in range(grid_rows):
            rows = slice(q * GRID_W, (q + 1) * GRID_W)
            o_ref[rows, 0 * quarter:1 * quarter] = x_ref[rows, 0 * quarter:1 * quarter] + sin_r[q:q + 1, :]
            o_ref[rows, 1 * quarter:2 * quarter] = x_ref[rows, 1 * quarter:2 * quarter] + cos_r[q:q + 1, :]
            o_ref[rows, 2 * quarter:3 * quarter] = x_ref[rows, 2 * quarter:3 * quarter] + sin_c
            o_ref[rows, 3 * quarter:4 * quarter] = x_ref[rows, 3 * quarter:4 * quarter] + cos_c


def _embed_call(x, ctx):
    lat_tiles_b = SEQ // TM
    return pl.pallas_call(
        _embed_kernel,
        grid=(NT,),
        in_specs=[
            pl.BlockSpec((TM, D_MODEL), lambda i: ((i // NT_B) * lat_tiles_b + jnp.maximum(i % NT_B - 1, 0), 0)),
            pl.BlockSpec((TM, D_MODEL), lambda i: (i // NT_B, 0)),
        ],
        out_specs=pl.BlockSpec((TM, D_MODEL), lambda i: (i, 0)),
        out_shape=jax.ShapeDtypeStruct((T_ALL, D_MODEL), F32),
        compiler_params=_params(("parallel",)),
        name="embed",
    )(x.reshape(BATCH * SEQ, D_MODEL), ctx.reshape(BATCH * CTX_LEN, D_MODEL))


def _ffn_kernel(*refs, sub):
    x_refs = refs[:FFN_TILES]
    mod_refs = refs[FFN_TILES:2 * FFN_TILES]
    win_ref, wout_ref, g_ref, b_ref, o_ref, xm_ref, h_ref = refs[2 * FFN_TILES:]
    for t, (x_ref, mod_ref) in enumerate(zip(x_refs, mod_refs)):
        shift = mod_ref[3 * sub + 0:3 * sub + 1, :]
        scale = mod_ref[3 * sub + 1:3 * sub + 2, :]
        xm_ref[t * TM:(t + 1) * TM, :] = (x_ref[...] * (1.0 + scale) + shift).astype(BF16)
    xm = xm_ref[...]
    for c in range(D_FF // FF_CH):
        a1 = _dot(xm, win_ref[:, c * FF_CH:(c + 1) * FF_CH])
        a2 = _dot(xm, win_ref[:, D_FF + c * FF_CH:D_FF + (c + 1) * FF_CH])
        h_ref[:, c * FF_CH:(c + 1) * FF_CH] = (_silu(a1) * a2).astype(BF16)
    hid = h_ref[...]
    y = jnp.concatenate([_dot(hid, wout_ref[:, n * MXU_N:(n + 1) * MXU_N]) for n in range(D_MODEL // MXU_N)], axis=1)
    for t, (x_ref, mod_ref) in enumerate(zip(x_refs, mod_refs)):
        gate = mod_ref[3 * sub + 2:3 * sub + 3, :]
        r = ALPHA * x_ref[...] + (FFN_RES * gate) * y[t * TM:(t + 1) * TM, :]
        o_ref[t * TM:(t + 1) * TM, :] = _ln(r, g_ref[...], b_ref[...])


def _ffn_call(s, mods, ffn_w_in, ffn_w_out, ln_g, ln_b, *, layer, which, final):
    sub = 2 * which
    if final:
        lat_tiles_b = SEQ // TM
        out_tiles = BATCH * lat_tiles_b
        in_tile = lambda t: (t // lat_tiles_b) * NT_B + 1 + t % lat_tiles_b
    else:
        out_tiles = NT
        in_tile = lambda t: t
    tile_of = lambda i, t: in_tile(i * FFN_TILES + t)
    x_specs = [pl.BlockSpec((TM, D_MODEL), functools.partial(lambda i, t: (tile_of(i, t), 0), t=t))
               for t in range(FFN_TILES)]
    mod_specs = [pl.BlockSpec((None, 9, D_MODEL),
                              functools.partial(lambda i, t: (layer * MOD_ROWS + _mod_row(tile_of(i, t)), 0, 0), t=t))
                 for t in range(FFN_TILES)]
    return pl.pallas_call(
        functools.partial(_ffn_kernel, sub=sub),
        grid=(out_tiles // FFN_TILES,),
        in_specs=x_specs + mod_specs + [
            _const_spec((None, None, D_MODEL, 2 * D_FF), (layer, which, 0, 0)),
            _const_spec((None, None, D_FF, D_MODEL + LANE_PAD), (layer, which, 0, 0)),
            _const_spec((None, 1, D_MODEL), (layer * 3 + sub, 0, 0)),
            _const_spec((None, 1, D_MODEL), (layer * 3 + sub, 0, 0)),
        ],
        out_specs=pl.BlockSpec((FFN_TILES * TM, D_MODEL), lambda i: (i, 0)),
        out_shape=jax.ShapeDtypeStruct((out_tiles * TM, D_MODEL), F32),
        scratch_shapes=[
            pltpu.VMEM((FFN_TILES * TM, D_MODEL), BF16),
            pltpu.VMEM((FFN_TILES * TM, D_FF), BF16),
        ],
        compiler_params=_params(("parallel",)),
        name=f"ffn{which}",
    )(*([s] * FFN_TILES), *([mods] * FFN_TILES), ffn_w_in, ffn_w_out, ln_g, ln_b)


def _cproj_kernel(x_ref, prev_ref, next_ref, mod_ref, w_ref, b_ref, cw_ref,
                  q_ref, kt_ref, v_ref, g_ref, gt_ref, xe_ref, p_ref):
    j = pl.program_id(0) % NT_B
    first = jnp.logical_or(j == 0, j == 1)
    last = jnp.logical_or(j == 0, j == NT_B - 1)
    shift = mod_ref[3:4, :]
    scale = mod_ref[4:5, :]

    def modulate(v):
        return (v * (1.0 + scale) + shift).astype(BF16)

    xe_ref[0:HALO, :] = modulate(prev_ref[...])
    xe_ref[HALO:HALO + TM, :] = modulate(x_ref[...])
    xe_ref[HALO + TM:2 * HALO + TM, :] = modulate(next_ref[...])
    xe = xe_ref[...]
    rows = lax.broadcasted_iota(jnp.int32, (TM + 2 * HALO, 1), 0)
    valid = jnp.logical_and(jnp.logical_or(rows >= HALO, jnp.logical_not(first)),
                            jnp.logical_or(rows < HALO + TM, jnp.logical_not(last)))
    p_ref[...] = jnp.where(valid, _dot(xe, w_ref[:, CP_QK:CP_V]) + b_ref[:, CP_QK:CP_V], 0.0)
    conv = (cw_ref[0:1, :] * p_ref[HALO - 1:HALO - 1 + TM, :]
            + cw_ref[1:2, :] * p_ref[HALO:HALO + TM, :]
            + cw_ref[2:3, :] * p_ref[HALO + 1:HALO + 1 + TM, :])
    qk = _silu(conv)
    q_ref[...] = qk[:, :MIX_W].astype(BF16)
    k = qk[:, MIX_W:] * MLSTM_DH ** -0.5
    for c in range(CH_T):
        for h in range(MLSTM_HEADS):
            blk = k[c * CHUNK:(c + 1) * CHUNK, h * MLSTM_DH:(h + 1) * MLSTM_DH]
            r0 = (c * MLSTM_HEADS + h) * MLSTM_DH
            kt_ref[r0:r0 + MLSTM_DH, :] = blk.T.astype(BF16)

    xm = xe_ref[HALO:HALO + TM, :]
    pvg = _dot(xm, w_ref[:, CP_V:CP_COLS]) + b_ref[:, CP_V:CP_COLS]
    v_ref[...] = pvg[:, :MIX_W].astype(BF16)
    li = pvg[:, CP_GI - CP_V:CP_GF - CP_V]
    fraw = pvg[:, CP_GF - CP_V:]
    lf = jnp.minimum(fraw, 0.0) - jnp.log1p(jnp.exp(-jnp.abs(fraw)))
    row = lax.broadcasted_iota(jnp.int32, (CHUNK, LANES), 0)
    lane = lax.broadcasted_iota(jnp.int32, (CHUNK, LANES), 1)
    fwd_lane = lane < MLSTM_HEADS
    lower_f = (lane <= row).astype(F32)
    upper_f = (lane >= row).astype(F32)
    for c in range(CH_T):
        tok = slice(c * CHUNK, (c + 1) * CHUNK)
        cum = jnp.where(fwd_lane, _dot_f32(lower_f, lf[tok]), _dot_f32(upper_f, lf[tok]))
        r = li[tok] - cum
        rf, rb = r, r
        k = 1
        while k < CHUNK:
            rf = jnp.maximum(rf, jnp.where(row >= k, pltpu.roll(rf, k, 0), -jnp.inf))
            rb = jnp.maximum(rb, jnp.where(row < CHUNK - k, pltpu.roll(rb, CHUNK - k, 0), -jnp.inf))
            k *= 2
        rmax = jnp.where(fwd_lane, rf, rb)
        g_ref[tok, :] = jnp.where(lane < GS_RMAX, cum, pltpu.roll(rmax, GS_RMAX, 1))
        gt_ref[c * SUBLANES:(c + 1) * SUBLANES, :] = r.T[:SUBLANES, :]


def _cproj_call(s, mods, w_cp, b_cp, qk_conv_w, *, layer):
    hb = TM // HALO
    n_hb = T_ALL // HALO
    return pl.pallas_call(
        _cproj_kernel,
        grid=(NT,),
        in_specs=[
            pl.BlockSpec((TM, D_MODEL), lambda i: (i, 0)),
            pl.BlockSpec((HALO, D_MODEL), lambda i: (jnp.maximum(i * hb - 1, 0), 0)),
            pl.BlockSpec((HALO, D_MODEL), lambda i: (jnp.minimum((i + 1) * hb, n_hb - 1), 0)),
            pl.BlockSpec((None, 9, D_MODEL), lambda i: (layer * MOD_ROWS + _mod_row(i), 0, 0)),
            _const_spec((None, D_MODEL, CP_COLS), (layer, 0, 0)),
            _const_spec((None, 1, CP_COLS), (layer, 0, 0)),
            _const_spec((None, QK_CONV, 2 * MIX_W), (layer, 0, 0)),
        ],
        out_specs=[
            pl.BlockSpec((TM, MIX_W), lambda i: (i, 0)),
            pl.BlockSpec((TM * MLSTM_HEADS, MLSTM_DH), lambda i: (i, 0)),
            pl.BlockSpec((TM, MIX_W), lambda i: (i, 0)),
            pl.BlockSpec((TM, LANES), lambda i: (i, 0)),
            pl.BlockSpec((CH_T * SUBLANES, CHUNK), lambda i: (i, 0)),
        ],
        out_shape=[
            jax.ShapeDtypeStruct((T_ALL, MIX_W), BF16),
            jax.ShapeDtypeStruct((T_ALL * MLSTM_HEADS, MLSTM_DH), BF16),
            jax.ShapeDtypeStruct((T_ALL, MIX_W), BF16),
            jax.ShapeDtypeStruct((T_ALL, LANES), F32),
            jax.ShapeDtypeStruct((T_ALL // CHUNK * SUBLANES, CHUNK), F32),
        ],
        scratch_shapes=[
            pltpu.VMEM((TM + 2 * HALO, D_MODEL), BF16),
            pltpu.VMEM((TM + 2 * HALO, 2 * MIX_W), F32),
        ],
        compiler_params=_params(("parallel",)),
        name="cproj",
    )(s, s, s, mods, w_cp, b_cp, qk_conv_w)


def _mlstm_kernel(qf_ref, ktf_ref, vf_ref, gf_ref, gtf_ref, qb_ref, ktb_ref, vb_ref, gb_ref, gtb_ref,
                  hf_ref, hb_ref, c_ref, m_ref):
    @pl.when(pl.program_id(0) == 0)
    def _():
        c_ref[...] = jnp.zeros(c_ref.shape, F32)
        m_ref[...] = jnp.zeros(m_ref.shape, F32)

    row = lax.broadcasted_iota(jnp.int32, (CHUNK, CHUNK), 0)
    col = lax.broadcasted_iota(jnp.int32, (CHUNK, CHUNK), 1)
    ones_col = jnp.ones((CHUNK, MLSTM_DH), BF16)
    nh = MLSTM_HEADS
    dirs = ((qf_ref, ktf_ref, vf_ref, gf_ref, gtf_ref, hf_ref, col <= row, CHUNK - 1),
            (qb_ref, ktb_ref, vb_ref, gb_ref, gtb_ref, hb_ref, col >= row, 0))
    heads = []
    for d, (q_ref, kt_ref, v_ref, g_ref, gt_ref, h_ref, seen, end_row) in enumerate(dirs):
        for b in range(BATCH):
            g = g_ref[b]
            gt = gt_ref[b]
            for h in range(nh):
                lane = d * nh + h
                hd = slice(h * MLSTM_DH, (h + 1) * MLSTM_DH)
                heads.append(dict(
                    idx=(d * BATCH + b) * nh + h, seen=seen, end_row=end_row, out=(h_ref, b, hd),
                    cum=g[:, lane:lane + 1], rmax=g[:, GS_RMAX + lane:GS_RMAX + lane + 1], r=gt[lane:lane + 1, :],
                    q=q_ref[b, :, hd], kt=kt_ref[b, hd, :], v=v_ref[b, :, hd]))

    for hd in heads:
        hd["qk"] = _dot(hd["q"], hd["kt"])
        hd["m_prev"] = m_ref[hd["idx"], 0:1, 0:1]
        hd["c_old"] = c_ref[hd["idx"]]
        hd["v_aug"] = jnp.concatenate([hd["v"], ones_col], axis=1)
    for hd in heads:
        e = hd["end_row"]
        top = jnp.maximum(hd["m_prev"], hd["rmax"][e:e + 1, :])
        w_row = jnp.exp(hd["r"] - top)
        ktw = (hd["kt"].astype(F32) * w_row).astype(BF16)
        c_ref[hd["idx"]] = jnp.exp(hd["m_prev"] - top) * hd["c_old"] + _dot(ktw, hd["v_aug"])
        m_ref[hd["idx"]] = jnp.broadcast_to(hd["cum"][e:e + 1, :] + top, m_ref.shape[1:])
    for hd in heads:
        m_prev = hd["m_prev"]
        top = jnp.broadcast_to(jnp.maximum(m_prev, hd["rmax"]), (CHUNK, CHUNK))
        s = hd["qk"] * jnp.exp(jnp.where(hd["seen"], hd["r"] - top, -jnp.inf))
        qw = hd["q"].astype(F32) * jnp.exp(m_prev - top)
        lhs = jnp.concatenate([s.astype(BF16), qw.astype(BF16)], axis=1)
        rhs = jnp.concatenate([hd["v_aug"], hd["c_old"].astype(BF16)], axis=0)
        hd["na"] = _dot(lhs, rhs)
        hd["floor"] = jnp.exp(-(hd["cum"] + top))
    for hd in heads:
        den = jnp.maximum(jnp.abs(hd["na"][:, MLSTM_DH:]), hd["floor"])
        h_ref, b, cols = hd["out"]
        h_ref[b, :, cols] = hd["na"][:, :MLSTM_DH] / den


def _mlstm_call(q, kt, v, g, gt):
    fwd = lambda s: s
    bwd = lambda s: jnp.where(s < CTX_CH, CTX_CH - 1 - s, NCH_B + CTX_CH - 1 - s)
    q3 = q.reshape(BATCH, SEQ_B, MIX_W)
    kt3 = kt.reshape(BATCH, SEQ_B * MLSTM_HEADS, MLSTM_DH)
    v3 = v.reshape(BATCH, SEQ_B, MIX_W)
    g3 = g.reshape(BATCH, SEQ_B, LANES)
    gt3 = gt.reshape(BATCH, NCH_B * SUBLANES, CHUNK)

    def specs(order):
        return [
            pl.BlockSpec((BATCH, CHUNK, MIX_W), lambda s: (0, order(s), 0)),
            pl.BlockSpec((BATCH, CHUNK * MLSTM_HEADS, MLSTM_DH), lambda s: (0, order(s), 0)),
            pl.BlockSpec((BATCH, CHUNK, MIX_W), lambda s: (0, order(s), 0)),
            pl.BlockSpec((BATCH, CHUNK, LANES), lambda s: (0, order(s), 0)),
            pl.BlockSpec((BATCH, SUBLANES, CHUNK), lambda s: (0, order(s), 0)),
        ]

    n_state = 2 * BATCH * MLSTM_HEADS
    hf, hb = pl.pallas_call(
        _mlstm_kernel,
        grid=(NCH_B,),
        in_specs=specs(fwd) + specs(bwd),
        out_specs=[
            pl.BlockSpec((BATCH, CHUNK, MIX_W), lambda s: (0, fwd(s), 0)),
            pl.BlockSpec((BATCH, CHUNK, MIX_W), lambda s: (0, bwd(s), 0)),
        ],
        out_shape=[jax.ShapeDtypeStruct((BATCH, SEQ_B, MIX_W), F32)] * 2,
        scratch_shapes=[
            pltpu.VMEM((n_state, MLSTM_DH, 2 * MLSTM_DH), F32),
            pltpu.VMEM((n_state, SUBLANES, LANES), F32),
        ],
        compiler_params=_params(("arbitrary",)),
        name="mlstm",
    )(q3, kt3, v3, g3, gt3, q3, kt3, v3, g3, gt3)
    return hf.reshape(T_ALL, MIX_W), hb.reshape(T_ALL, MIX_W)


VEC_GMLP_G, VEC_GMLP_B, VEC_CONV_B, VEC_CONV_G, VEC_CONV_LB, VEC_MLSTM_G, VEC_POOL_S = range(7)
VEC_ROWS = 8


def _mix_kernel(x_ref, prev_ref, next_ref, mod_ref, hf_ref, hb_ref, w_ref, b_ref, vec_ref, ws_ref, bst_ref,
                cw_ref, pw_ref, wbr_ref, wout_ref, g_ref, beta_ref, o_ref, xe_ref, a_ref, d_ref, ash_ref, conv_ref,
                gate_ref):
    j = pl.program_id(0) % NT_B
    is_ctx = j == 0
    first = jnp.logical_or(is_ctx, j == 1)
    last = jnp.logical_or(is_ctx, j == NT_B - 1)
    shift = mod_ref[3:4, :]
    scale = mod_ref[4:5, :]
    gate = mod_ref[5:6, :]

    def modulate(v):
        return (v * (1.0 + scale) + shift).astype(BF16)

    def vec(r):
        return vec_ref[r:r + 1, :]

    x = x_ref[...]
    xe_ref[0:HALO, :] = modulate(prev_ref[...])
    xe_ref[HALO:HALO + TM, :] = modulate(x)
    xe_ref[HALO + TM:2 * HALO + TM, :] = modulate(next_ref[...])
    xe = xe_ref[...]
    xm = xe_ref[HALO:HALO + TM, :]
    rows = lax.broadcasted_iota(jnp.int32, (TM + 2 * HALO, 1), 0)
    valid = jnp.logical_and(jnp.logical_or(rows >= HALO, jnp.logical_not(first)),
                            jnp.logical_or(rows < HALO + TM, jnp.logical_not(last)))

    def proj(lhs, lo, width):
        return jnp.concatenate([_dot(lhs, w_ref[:, c:c + MXU_N]) + b_ref[:, c:c + MXU_N]
                                for c in range(lo, lo + width, MXU_N)], axis=1)

    pa = jax.nn.gelu(proj(xm, MX_A, 2 * MIX_W))
    u = pa[:, :MIX_W]
    vn = _ln(pa[:, MIX_W:], vec(VEC_GMLP_G), vec(VEC_GMLP_B)).astype(BF16)
    gd = MIX_W // GMLP_GROUPS
    z_rows = []
    for c in range(CH_T):
        z_cols = []
        for gi in range(GMLP_GROUPS):
            blk = vn[c * CHUNK:(c + 1) * CHUNK, gi * gd:(gi + 1) * gd]
            z_cols.append(_dot(ws_ref[gi], blk) + bst_ref[:, gi:gi + 1])
        z_rows.append(jnp.concatenate(z_cols, axis=1))
    ya = (u * jnp.concatenate(z_rows, axis=0)).astype(BF16)

    pb = proj(xe, MX_B, 2 * MIX_W)
    a_ref[...] = jnp.where(valid, pb[:, :MIX_W] * _sigmoid(pb[:, MIX_W:]), 0.0)
    sh_rows = TM + 2 * HALO - SUBLANES
    for s in range(1, SUBLANES):
        ash_ref[s - 1, 0:sh_rows, :] = a_ref[s:s + sh_rows, :]
    gate_chunks = [(i, n) for i in range(N_BRANCH) for n in range(D_MODEL // MXU_N)]
    conv_blocks = [(cb, rb) for cb in range(MIX_W // LANES) for rb in range(TM // CONV_RB)]
    zero_row = jnp.zeros((1, LANES), F32)
    for step in range(max(len(gate_chunks), len(conv_blocks))):
        if step < len(gate_chunks):
            i, n = gate_chunks[step]
            lo = i * D_MODEL + n * MXU_N
            gv = jnp.tanh(0.5 * proj(xm, MX_G + lo, MXU_N))
            gate_ref[:, lo:lo + MXU_N] = gv
            bits = lax.bitcast_convert_type(gv[0:1, 0:LANES], jnp.uint32)
            zero_row = lax.shift_right_logical(lax.shift_right_logical(bits, jnp.uint32(16)),
                                               jnp.uint32(16)).astype(F32)
        if step < len(conv_blocks):
            cb, rb = conv_blocks[step]
            cols = slice(cb * LANES, (cb + 1) * LANES)
            part = jnp.zeros((CONV_RB, LANES), F32) + zero_row
            for k in range(CONV_W):
                q8, s = divmod(HALO - CONV_W // 2 + k, SUBLANES)
                r0 = q8 * SUBLANES + rb * CONV_RB
                src = a_ref[r0:r0 + CONV_RB, cols] if s == 0 else ash_ref[s - 1, r0:r0 + CONV_RB, cols]
                part = part + cw_ref[k:k + 1, cols] * src
            conv_ref[rb * CONV_RB:(rb + 1) * CONV_RB, cols] = part
    yb = _silu(_ln(conv_ref[...] + vec(VEC_CONV_B), vec(VEC_CONV_G), vec(VEC_CONV_LB))).astype(BF16)

    og = _sigmoid(proj(xm, MX_O, MIX_W))
    hsum = hf_ref[...] + hb_ref[...]
    hn = []
    for h in range(MLSTM_HEADS):
        hh = hsum[:, h * MLSTM_DH:(h + 1) * MLSTM_DH]
        mu = jnp.mean(hh, axis=-1, keepdims=True)
        hc = hh - mu
        hn.append(hc * lax.rsqrt(jnp.mean(hc * hc, axis=-1, keepdims=True) + LN_EPS))
    yc = (og * (jnp.concatenate(hn, axis=1) * vec(VEC_MLSTM_G))).astype(BF16)

    d_ref[...] = jnp.where(valid, proj(xe, MX_D, MIX_W), 0.0)
    pos = jnp.where(is_ctx, 0, (j - 1) * TM) + lax.broadcasted_iota(jnp.int32, (TM, 1), 0)
    n_seq = jnp.where(is_ctx, CTX_LEN, SEQ)
    yd = []
    for gi, win in enumerate(POOL_WINDOWS):
        lo, hi = win // 2, win - 1 - win // 2
        cols = slice(gi * POOL_GD, (gi + 1) * POOL_GD)
        wsum = d_ref[HALO - lo:HALO - lo + TM, cols]
        for k in range(-lo + 1, hi + 1):
            wsum = wsum + d_ref[HALO + k:HALO + k + TM, cols]
        cnt = (jnp.minimum(pos + hi + 1, n_seq) - jnp.maximum(pos - lo, 0)).astype(F32)
        diff = wsum / cnt - d_ref[HALO:HALO + TM, cols]
        yd.append(_dot(diff.astype(BF16), pw_ref[gi]))
    yd = (jnp.concatenate(yd, axis=1) * vec(VEC_POOL_S)).astype(BF16)

    merged = []
    for n in range(D_MODEL // MXU_N):
        cols = slice(n * MXU_N, (n + 1) * MXU_N)
        acc = None
        for i, yi in enumerate((ya, yb, yc, yd)):
            lo = i * D_MODEL + n * MXU_N
            p = _dot(yi, wbr_ref[i, :, cols])
            term = gate_ref[:, lo:lo + MXU_N] * p + p
            acc = term if acc is None else acc + term
        merged.append((0.5 * acc).astype(BF16))
    merged = jnp.concatenate(merged, axis=1)
    y = jnp.concatenate([_dot(merged, wout_ref[:, n * MXU_N:(n + 1) * MXU_N]) for n in range(D_MODEL // MXU_N)],
                        axis=1)
    o_ref[...] = _ln(ALPHA * x + gate * y, g_ref[...], beta_ref[...])


def _mix_call(s, mods, hf, hb, w_mx, b_mx, vec512, gmlp_ws, gmlp_bst, conv_w, pool_w, w_branch, w_out,
              ln_g, ln_b, *, layer):
    hb_per_tile = TM // HALO
    n_hb = T_ALL // HALO
    return pl.pallas_call(
        _mix_kernel,
        grid=(NT,),
        in_specs=[
            pl.BlockSpec((TM, D_MODEL), lambda i: (i, 0)),
            pl.BlockSpec((HALO, D_MODEL), lambda i: (jnp.maximum(i * hb_per_tile - 1, 0), 0)),
            pl.BlockSpec((HALO, D_MODEL), lambda i: (jnp.minimum((i + 1) * hb_per_tile, n_hb - 1), 0)),
            pl.BlockSpec((None, 9, D_MODEL), lambda i: (layer * MOD_ROWS + _mod_row(i), 0, 0)),
            pl.BlockSpec((TM, MIX_W), lambda i: (i, 0)),
            pl.BlockSpec((TM, MIX_W), lambda i: (i, 0)),
            _const_spec((None, D_MODEL, MX_COLS + LANE_PAD), (layer, 0, 0)),
            _const_spec((None, 1, MX_COLS), (layer, 0, 0)),
            _const_spec((None, VEC_ROWS, MIX_W), (layer, 0, 0)),
            _const_spec((None, GMLP_GROUPS, CHUNK, CHUNK), (layer, 0, 0, 0)),
            _const_spec((None, CHUNK, GMLP_GROUPS), (layer, 0, 0)),
            _const_spec((None, CONV_W, MIX_W), (layer, 0, 0)),
            _const_spec((None, len(POOL_WINDOWS), POOL_GD, POOL_GD), (layer, 0, 0, 0)),
            _const_spec((None, N_BRANCH, MIX_W, D_MODEL + LANE_PAD), (layer, 0, 0, 0)),
            _const_spec((None, D_MODEL, D_MODEL + LANE_PAD), (layer, 0, 0)),
            _const_spec((None, 1, D_MODEL), (layer * 3 + 1, 0, 0)),
            _const_spec((None, 1, D_MODEL), (layer * 3 + 1, 0, 0)),
        ],
        out_specs=pl.BlockSpec((TM, D_MODEL), lambda i: (i, 0)),
        out_shape=jax.ShapeDtypeStruct((T_ALL, D_MODEL), F32),
        scratch_shapes=[
            pltpu.VMEM((TM + 2 * HALO, D_MODEL), BF16),
            pltpu.VMEM((TM + 2 * HALO, MIX_W), F32),
            pltpu.VMEM((TM + 2 * HALO, MIX_W), F32),
            pltpu.VMEM((SUBLANES - 1, TM + 2 * HALO, MIX_W), F32),
            pltpu.VMEM((TM, MIX_W), F32),
            pltpu.VMEM((TM, N_BRANCH * D_MODEL), F32),
        ],
        compiler_params=_params(("parallel",)),
        name="mix",
    )(s, s, s, mods, hf, hb, w_mx, b_mx, vec512, gmlp_ws, gmlp_bst, conv_w, pool_w, w_branch, w_out, ln_g, ln_b)


def _repack_cols(w, pieces):
    parts = [jnp.zeros(w.shape[:-1] + (p,), w.dtype) if isinstance(p, int) else w[..., p[0]:p[1]] for p in pieces]
    return jnp.concatenate(parts, axis=-1)


def kernel(x, c, ctx, c_ctx, w_ada, b_ada, ln_g, ln_b, ffn_w_in, ffn_w_out, w_in, b_in, gmlp_ln_g, gmlp_ln_b,
           gmlp_ws, gmlp_bs, conv_w, conv_b, conv_ln_g, conv_ln_b, qk_conv_w, mlstm_ln_g, pool_w, pool_scale,
           w_branch, w_out):
    mix_pieces = [(OFF_A, OFF_C), (OFF_C_O, OFF_D), (OFF_D, OFF_G), (OFF_G, IN_COLS)]
    nh, og = MLSTM_HEADS, OFF_C_GATES
    gate_pad = LANES - 2 * nh
    cp_pieces = [(OFF_C, OFF_C_GATES),
                 (og, og + nh), (og + 2 * nh, og + 3 * nh), gate_pad,
                 (og + nh, og + 2 * nh), (og + 3 * nh, og + 4 * nh), gate_pad]
    w_mx = _repack_cols(w_in, mix_pieces + [LANE_PAD]).astype(BF16)
    b_mx = _repack_cols(b_in, mix_pieces).reshape(DEPTH, 1, MX_COLS)
    w_cp = _repack_cols(w_in, cp_pieces).astype(BF16)
    b_cp = _repack_cols(b_in, cp_pieces).reshape(DEPTH, 1, CP_COLS)
    ffn_w_in_b = ffn_w_in.astype(BF16)
    ffn_w_out_b = _repack_cols(ffn_w_out, [(0, D_MODEL), LANE_PAD]).astype(BF16)
    w_branch_b = _repack_cols(w_branch, [(0, D_MODEL), LANE_PAD]).astype(BF16)
    w_out_b = _repack_cols(w_out, [(0, D_MODEL), LANE_PAD]).astype(BF16)
    gmlp_ws_b = gmlp_ws.astype(BF16)
    pool_w_b = pool_w.astype(BF16)
    gmlp_bst = jnp.swapaxes(gmlp_bs, 1, 2)
    vec512 = jnp.stack([gmlp_ln_g, gmlp_ln_b, conv_b, conv_ln_g, conv_ln_b, mlstm_ln_g, pool_scale,
                        jnp.zeros_like(pool_scale)], axis=1)
    ln_g3 = ln_g.reshape(DEPTH * 3, 1, D_MODEL)
    ln_b3 = ln_b.reshape(DEPTH * 3, 1, D_MODEL)
    c_t = jnp.concatenate([c, c_ctx[None], jnp.zeros((MOD_ROWS - BATCH - 1, D_MODEL), F32)], axis=0).T

    mods = _mods_call(c_t, w_ada, b_ada).reshape(DEPTH * MOD_ROWS, 9, D_MODEL)
    s = _embed_call(x, ctx)
    for l in range(DEPTH):
        last = l == DEPTH - 1
        s = _ffn_call(s, mods, ffn_w_in_b, ffn_w_out_b, ln_g3, ln_b3, layer=l, which=0, final=False)
        q, kt, v, g, gt = _cproj_call(s, mods, w_cp, b_cp, qk_conv_w, layer=l)
        hf, hb = _mlstm_call(q, kt, v, g, gt)
        s = _mix_call(s, mods, hf, hb, w_mx, b_mx, vec512, gmlp_ws_b, gmlp_bst, conv_w, pool_w_b, w_branch_b,
                      w_out_b, ln_g3, ln_b3, layer=l)
        s = _ffn_call(s, mods, ffn_w_in_b, ffn_w_out_b, ln_g3, ln_b3, layer=l, which=1, final=last)
    return s.reshape(BATCH, SEQ, D_MODEL)
```

```python
import functools
import math

import jax
import jax.numpy as jnp
from jax import lax
from jax.experimental import pallas as pl
from jax.experimental.pallas import tpu as pltpu

D_MODEL = 1024
BATCH = 2
SEQ = 8192
DEPTH = 4
GRID_W = 64
CTX_LEN = 256
MIX_W = D_MODEL // 2
N_BRANCH = 4
CHUNK = 128
GMLP_GROUPS = 4
CONV_W = 31
MLSTM_HEADS = 4
MLSTM_DH = MIX_W // MLSTM_HEADS
QK_CONV = 3
POOL_WINDOWS = (2, 4, 8, 16)
POOL_GD = MIX_W // len(POOL_WINDOWS)
D_FF = 128 * ((8 * D_MODEL // 3 + 127) // 128)
ALPHA = (2 * DEPTH) ** 0.25
LN_EPS = 1e-6
FFN_RES = 0.5

OFF_A = 0
OFF_B = OFF_A + 2 * MIX_W
OFF_C = OFF_B + 2 * MIX_W
OFF_C_GATES = OFF_C + 3 * MIX_W
OFF_C_O = OFF_C_GATES + 4 * MLSTM_HEADS
OFF_D = OFF_C_O + MIX_W
OFF_G = OFF_D + MIX_W
IN_COLS = OFF_G + N_BRANCH * D_MODEL

LANES = 128
SUBLANES = 8
BF16_ROWS = 16
TM = 256
SEQ_B = CTX_LEN + SEQ
T_ALL = BATCH * SEQ_B
NT_B = SEQ_B // TM
NT = BATCH * NT_B
NCH_B = SEQ_B // CHUNK
CH_T = TM // CHUNK
CTX_CH = CTX_LEN // CHUNK
HALO = 16
MXU_N = 256
LANE_PAD = LANES
FF_CH = MXU_N
FFN_TILES = 2
CONV_RB = 64
CHAIN_LAG = 3
GS_RMAX = 2 * MLSTM_HEADS
MOD_ROWS = 8
CTX_MOD_ROW = BATCH
VMEM_LIMIT = 56 * 1024 * 1024

MX_A = 0
MX_B = MX_A + 2 * MIX_W
MX_O = MX_B + 2 * MIX_W
MX_D = MX_O + MIX_W
MX_G = MX_D + MIX_W
MX_COLS = MX_G + N_BRANCH * D_MODEL
CP_QK = 0
CP_V = 2 * MIX_W
CP_GI = 3 * MIX_W
CP_GF = CP_GI + LANES
CP_COLS = CP_GF + LANES

F32 = jnp.float32
BF16 = jnp.bfloat16


def _dot(a, b):
    return jnp.dot(a, b, preferred_element_type=F32)


def _dot_f32(a, b):
    return jnp.dot(a, b, preferred_element_type=F32, precision=lax.Precision.HIGHEST)


def _ln(r, g, b):
    mu = jnp.mean(r, axis=-1, keepdims=True)
    xc = r - mu
    var = jnp.mean(xc * xc, axis=-1, keepdims=True)
    return xc * lax.rsqrt(var + LN_EPS) * g + b


def _sigmoid(x):
    return 0.5 * jnp.tanh(0.5 * x) + 0.5


def _silu(x):
    hx = 0.5 * x
    return hx * jnp.tanh(hx) + hx


def _mod_row(i):
    return jnp.where(i % NT_B == 0, CTX_MOD_ROW, i // NT_B)


def _const_spec(block, index):
    return pl.BlockSpec(block, lambda *_: index, pipeline_mode=pl.Buffered(1))


def _params(sem, flags=None):
    return pltpu.CompilerParams(dimension_semantics=sem, vmem_limit_bytes=VMEM_LIMIT, flags=flags)


def _mods_kernel(ct_ref, w_ref, b_ref, o_ref):
    s = _silu(ct_ref[...])
    w = w_ref[...]
    o_ref[...] = jnp.zeros(o_ref.shape, F32)
    for r in range(BATCH + 1):
        o_ref[r:r + 1, :] = jnp.sum(s[:, r:r + 1] * w, axis=0, keepdims=True) + b_ref[...]


def _mods_call(c_t, w_ada, b_ada):
    tn = D_MODEL
    n_col = w_ada.shape[-1] // tn
    return pl.pallas_call(
        _mods_kernel,
        grid=(DEPTH, n_col),
        in_specs=[
            pl.BlockSpec((D_MODEL, MOD_ROWS), lambda l, n: (0, 0)),
            pl.BlockSpec((None, D_MODEL, tn), lambda l, n: (l, 0, n)),
            pl.BlockSpec((None, 1, tn), lambda l, n: (l, 0, n)),
        ],
        out_specs=pl.BlockSpec((None, MOD_ROWS, tn), lambda l, n: (l, 0, n)),
        out_shape=jax.ShapeDtypeStruct((DEPTH, MOD_ROWS, w_ada.shape[-1]), F32),
        compiler_params=_params(("parallel", "parallel")),
        name="mods",
    )(c_t, w_ada, b_ada.reshape(DEPTH, 1, -1))


def _embed_kernel(x_ref, ctx_ref, o_ref):
    j = pl.program_id(0) % NT_B

    @pl.when(j == 0)
    def _():
        o_ref[...] = ctx_ref[...]

    @pl.when(j > 0)
    def _():
        quarter = D_MODEL // 4
        grid_rows = TM // GRID_W
        k = lax.broadcasted_iota(jnp.int32, (1, quarter), 1).astype(F32)
        freqs = jnp.exp(-math.log(10000.0) * k / quarter)
        r = ((j - 1) * grid_rows + lax.broadcasted_iota(jnp.int32, (SUBLANES, 1), 0)).astype(F32)
        col = lax.broadcasted_iota(jnp.int32, (GRID_W, 1), 0).astype(F32)
        er = r * freqs
        ec = col * freqs
        sin_r, cos_r, sin_c, cos_c = jnp.sin(er), jnp.cos(er), jnp.sin(ec), jnp.cos(ec)
        for q in range(grid_rows):
            rows = slice(q * GRID_W, (q + 1) * GRID_W)
            o_ref[rows, 0 * quarter:1 * quarter] = x_ref[rows, 0 * quarter:1 * quarter] + sin_r[q:q + 1, :]
            o_ref[rows, 1 * quarter:2 * quarter] = x_ref[rows, 1 * quarter:2 * quarter] + cos_r[q:q + 1, :]
            o_ref[rows, 2 * quarter:3 * quarter] = x_ref[rows, 2 * quarter:3 * quarter] + sin_c
            o_ref[rows, 3 * quarter:4 * quarter] = x_ref[rows, 3 * quarter:4 * quarter] + cos_c


def _embed_call(x, ctx):
    lat_tiles_b = SEQ // TM
    return pl.pallas_call(
        _embed_kernel,
        grid=(NT,),
        in_specs=[
            pl.BlockSpec((TM, D_MODEL), lambda i: ((i // NT_B) * lat_tiles_b + jnp.maximum(i % NT_B - 1, 0), 0)),
            pl.BlockSpec((TM, D_MODEL), lambda i: (i // NT_B, 0)),
        ],
        out_specs=pl.BlockSpec((TM, D_MODEL), lambda i: (i, 0)),
        out_shape=jax.ShapeDtypeStruct((T_ALL, D_MODEL), F32),
        compiler_params=_params(("parallel",)),
        name="embed",
    )(x.reshape(BATCH * SEQ, D_MODEL), ctx.reshape(BATCH * CTX_LEN, D_MODEL))


def _ffn_kernel(*refs, sub):
    x_refs = refs[:FFN_TILES]
    mod_refs = refs[FFN_TILES:2 * FFN_TILES]
    win_ref, wout_ref, g_ref, b_ref, o_ref, xm_ref, h_ref = refs[2 * FFN_TILES:]
    for t, (x_ref, mod_ref) in enumerate(zip(x_refs, mod_refs)):
        shift = mod_ref[3 * sub + 0:3 * sub + 1, :]
        scale = mod_ref[3 * sub + 1:3 * sub + 2, :]
        xm_ref[t * TM:(t + 1) * TM, :] = (x_ref[...] * (1.0 + scale) + shift).astype(BF16)
    xm = xm_ref[...]
    for c in range(D_FF // FF_CH):
        a1 = _dot(xm, win_ref[:, c * FF_CH:(c + 1) * FF_CH])
        a2 = _dot(xm, win_ref[:, D_FF + c * FF_CH:D_FF + (c + 1) * FF_CH])
        h_ref[:, c * FF_CH:(c + 1) * FF_CH] = (_silu(a1) * a2).astype(BF16)
    for t, (x_ref, mod_ref) in enumerate(zip(x_refs, mod_refs)):
        hid = h_ref[t * TM:(t + 1) * TM, :]
        y = jnp.concatenate([_dot(hid, wout_ref[:, n * MXU_N:(n + 1) * MXU_N]) for n in range(D_MODEL // MXU_N)],
                            axis=1)
        gate = mod_ref[3 * sub + 2:3 * sub + 3, :]
        r = ALPHA * x_ref[...] + (FFN_RES * gate) * y
        o_ref[t * TM:(t + 1) * TM, :] = _ln(r, g_ref[...], b_ref[...])


def _ffn_call(s, mods, ffn_w_in, ffn_w_out, ln_g, ln_b, *, layer, which, final):
    sub = 2 * which
    if final:
        lat_tiles_b = SEQ // TM
        out_tiles = BATCH * lat_tiles_b
        in_tile = lambda t: (t // lat_tiles_b) * NT_B + 1 + t % lat_tiles_b
    else:
        out_tiles = NT
        in_tile = lambda t: t
    tile_of = lambda i, t: in_tile(i * FFN_TILES + t)
    x_specs = [pl.BlockSpec((TM, D_MODEL), functools.partial(lambda i, t: (tile_of(i, t), 0), t=t))
               for t in range(FFN_TILES)]
    mod_specs = [pl.BlockSpec((None, 9, D_MODEL),
                              functools.partial(lambda i, t: (layer * MOD_ROWS + _mod_row(tile_of(i, t)), 0, 0), t=t))
                 for t in range(FFN_TILES)]
    return pl.pallas_call(
        functools.partial(_ffn_kernel, sub=sub),
        grid=(out_tiles // FFN_TILES,),
        in_specs=x_specs + mod_specs + [
            _const_spec((None, None, D_MODEL, 2 * D_FF), (layer, which, 0, 0)),
            _const_spec((None, None, D_FF, D_MODEL + LANE_PAD), (layer, which, 0, 0)),
            _const_spec((None, 1, D_MODEL), (layer * 3 + sub, 0, 0)),
            _const_spec((None, 1, D_MODEL), (layer * 3 + sub, 0, 0)),
        ],
        out_specs=pl.BlockSpec((FFN_TILES * TM, D_MODEL), lambda i: (i, 0)),
        out_shape=jax.ShapeDtypeStruct((out_tiles * TM, D_MODEL), F32),
        scratch_shapes=[
            pltpu.VMEM((FFN_TILES * TM, D_MODEL), BF16),
            pltpu.VMEM((FFN_TILES * TM, D_FF), BF16),
        ],
        compiler_params=_params(("parallel",)),
        name=f"ffn{which}",
    )(*([s] * FFN_TILES), *([mods] * FFN_TILES), ffn_w_in, ffn_w_out, ln_g, ln_b)


def _cproj_kernel(x_ref, prev_ref, next_ref, mod_ref, w_ref, b_ref, cw_ref,
                  q_ref, kt_ref, v_ref, g_ref, gt_ref, xe_ref, p_ref):
    j = pl.program_id(0) % NT_B
    first = jnp.logical_or(j == 0, j == 1)
    last = jnp.logical_or(j == 0, j == NT_B - 1)
    shift = mod_ref[3:4, :]
    scale = mod_ref[4:5, :]

    def modulate(v):
        return (v * (1.0 + scale) + shift).astype(BF16)

    xe_ref[0:HALO, :] = modulate(prev_ref[...])
    xe_ref[HALO:HALO + TM, :] = modulate(x_ref[...])
    xe_ref[HALO + TM:2 * HALO + TM, :] = modulate(next_ref[...])
    xe = xe_ref[...]
    rows = lax.broadcasted_iota(jnp.int32, (TM + 2 * HALO, 1), 0)
    valid = jnp.logical_and(jnp.logical_or(rows >= HALO, jnp.logical_not(first)),
                            jnp.logical_or(rows < HALO + TM, jnp.logical_not(last)))
    p_ref[...] = jnp.where(valid, _dot(xe, w_ref[:, CP_QK:CP_V]) + b_ref[:, CP_QK:CP_V], 0.0)
    conv = (cw_ref[0:1, :] * p_ref[HALO - 1:HALO - 1 + TM, :]
            + cw_ref[1:2, :] * p_ref[HALO:HALO + TM, :]
            + cw_ref[2:3, :] * p_ref[HALO + 1:HALO + 1 + TM, :])
    qk = _silu(conv)
    q_ref[...] = qk[:, :MIX_W].astype(BF16)
    k = qk[:, MIX_W:] * MLSTM_DH ** -0.5
    for c in range(CH_T):
        for h in range(MLSTM_HEADS):
            blk = k[c * CHUNK:(c + 1) * CHUNK, h * MLSTM_DH:(h + 1) * MLSTM_DH]
            r0 = (c * MLSTM_HEADS + h) * MLSTM_DH
            kt_ref[r0:r0 + MLSTM_DH, :] = blk.T.astype(BF16)

    xm = xe_ref[HALO:HALO + TM, :]
    pvg = _dot(xm, w_ref[:, CP_V:CP_COLS]) + b_ref[:, CP_V:CP_COLS]
    v_ref[...] = pvg[:, :MIX_W].astype(BF16)
    li = pvg[:, CP_GI - CP_V:CP_GF - CP_V]
    fraw = pvg[:, CP_GF - CP_V:]
    lf = jnp.minimum(fraw, 0.0) - jnp.log1p(jnp.exp(-jnp.abs(fraw)))
    row = lax.broadcasted_iota(jnp.int32, (CHUNK, LANES), 0)
    lane = lax.broadcasted_iota(jnp.int32, (CHUNK, LANES), 1)
    fwd_lane = lane < MLSTM_HEADS
    lower_f = (lane <= row).astype(F32)
    upper_f = (lane >= row).astype(F32)
    for c in range(CH_T):
        tok = slice(c * CHUNK, (c + 1) * CHUNK)
        cum = jnp.where(fwd_lane, _dot_f32(lower_f, lf[tok]), _dot_f32(upper_f, lf[tok]))
        r = li[tok] - cum
        rf, rb = r, r
        k = 1
        while k < CHUNK:
            rf = jnp.maximum(rf, jnp.where(row >= k, pltpu.roll(rf, k, 0), -jnp.inf))
            rb = jnp.maximum(rb, jnp.where(row < CHUNK - k, pltpu.roll(rb, CHUNK - k, 0), -jnp.inf))
            k *= 2
        rmax = jnp.where(fwd_lane, rf, rb)
        g_ref[tok, :] = jnp.where(lane < GS_RMAX, cum, pltpu.roll(rmax, GS_RMAX, 1))
        gt_ref[c * SUBLANES:(c + 1) * SUBLANES, :] = r.T[:SUBLANES, :]


def _cproj_call(s, mods, w_cp, b_cp, qk_conv_w, *, layer):
    hb = TM // HALO
    n_hb = T_ALL // HALO
    return pl.pallas_call(
        _cproj_kernel,
        grid=(NT,),
        in_specs=[
            pl.BlockSpec((TM, D_MODEL), lambda i: (i, 0)),
            pl.BlockSpec((HALO, D_MODEL), lambda i: (jnp.maximum(i * hb - 1, 0), 0)),
            pl.BlockSpec((HALO, D_MODEL), lambda i: (jnp.minimum((i + 1) * hb, n_hb - 1), 0)),
            pl.BlockSpec((None, 9, D_MODEL), lambda i: (layer * MOD_ROWS + _mod_row(i), 0, 0)),
            _const_spec((None, D_MODEL, CP_COLS), (layer, 0, 0)),
            _const_spec((None, 1, CP_COLS), (layer, 0, 0)),
            _const_spec((None, QK_CONV, 2 * MIX_W), (layer, 0, 0)),
        ],
        out_specs=[
            pl.BlockSpec((TM, MIX_W), lambda i: (i, 0)),
            pl.BlockSpec((TM * MLSTM_HEADS, MLSTM_DH), lambda i: (i, 0)),
            pl.BlockSpec((TM, MIX_W), lambda i: (i, 0)),
            pl.BlockSpec((TM, LANES), lambda i: (i, 0)),
            pl.BlockSpec((CH_T * SUBLANES, CHUNK), lambda i: (i, 0)),
        ],
        out_shape=[
            jax.ShapeDtypeStruct((T_ALL, MIX_W), BF16),
            jax.ShapeDtypeStruct((T_ALL * MLSTM_HEADS, MLSTM_DH), BF16),
            jax.ShapeDtypeStruct((T_ALL, MIX_W), BF16),
            jax.ShapeDtypeStruct((T_ALL, LANES), F32),
            jax.ShapeDtypeStruct((T_ALL // CHUNK * SUBLANES, CHUNK), F32),
        ],
        scratch_shapes=[
            pltpu.VMEM((TM + 2 * HALO, D_MODEL), BF16),
            pltpu.VMEM((TM + 2 * HALO, 2 * MIX_W), F32),
        ],
        compiler_params=_params(("parallel",)),
        name="cproj",
    )(s, s, s, mods, w_cp, b_cp, qk_conv_w)


def _mlstm_kernel(qf_ref, ktf_ref, vf_ref, gf_ref, gtf_ref, qb_ref, ktb_ref, vb_ref, gb_ref, gtb_ref,
                  hf_ref, hb_ref, c_ref, m_ref):
    @pl.when(pl.program_id(0) == 0)
    def _():
        c_ref[...] = jnp.zeros(c_ref.shape, F32)
        m_ref[...] = jnp.zeros(m_ref.shape, F32)

    row = lax.broadcasted_iota(jnp.int32, (CHUNK, CHUNK), 0)
    col = lax.broadcasted_iota(jnp.int32, (CHUNK, CHUNK), 1)
    ones_col = jnp.ones((CHUNK, MLSTM_DH), BF16)
    nh = MLSTM_HEADS
    dirs = ((qf_ref, ktf_ref, vf_ref, gf_ref, gtf_ref, hf_ref, col <= row, CHUNK - 1),
            (qb_ref, ktb_ref, vb_ref, gb_ref, gtb_ref, hb_ref, col >= row, 0))
    heads = []
    for d, (q_ref, kt_ref, v_ref, g_ref, gt_ref, h_ref, seen, end_row) in enumerate(dirs):
        for b in range(BATCH):
            g = g_ref[b]
            gt = gt_ref[b]
            for h in range(nh):
                lane = d * nh + h
                hd = slice(h * MLSTM_DH, (h + 1) * MLSTM_DH)
                heads.append(dict(
                    idx=(d * BATCH + b) * nh + h, seen=seen, end_row=end_row, out=(h_ref, b, hd),
                    cum=g[:, lane:lane + 1], rmax=g[:, GS_RMAX + lane:GS_RMAX + lane + 1], r=gt[lane:lane + 1, :],
                    q=q_ref[b, :, hd], kt=kt_ref[b, hd, :], v=v_ref[b, :, hd]))

    for hd in heads:
        hd["qk"] = _dot(hd["q"], hd["kt"])
        hd["m_prev"] = m_ref[hd["idx"], 0:1, 0:1]
        hd["c_old"] = c_ref[hd["idx"]]
        hd["v_aug"] = jnp.concatenate([hd["v"], ones_col], axis=1)
    for hd in heads:
        e = hd["end_row"]
        top = jnp.maximum(hd["m_prev"], hd["rmax"][e:e + 1, :])
        w_row = jnp.exp(hd["r"] - top)
        ktw = (hd["kt"].astype(F32) * w_row).astype(BF16)
        c_ref[hd["idx"]] = jnp.exp(hd["m_prev"] - top) * hd["c_old"] + _dot(ktw, hd["v_aug"])
        m_ref[hd["idx"]] = jnp.broadcast_to(hd["cum"][e:e + 1, :] + top, m_ref.shape[1:])
    for hd in heads:
        m_prev = hd["m_prev"]
        top = jnp.broadcast_to(jnp.maximum(m_prev, hd["rmax"]), (CHUNK, CHUNK))
        s = hd["qk"] * jnp.exp(jnp.where(hd["seen"], hd["r"] - top, -jnp.inf))
        qw = hd["q"].astype(F32) * jnp.exp(m_prev - top)
        lhs = jnp.concatenate([s.astype(BF16), qw.astype(BF16)], axis=1)
        rhs = jnp.concatenate([hd["v_aug"], hd["c_old"].astype(BF16)], axis=0)
        hd["na"] = _dot(lhs, rhs)
        hd["floor"] = jnp.exp(-(hd["cum"] + top))
    for hd in heads:
        den = jnp.maximum(jnp.abs(hd["na"][:, MLSTM_DH:]), hd["floor"])
        h_ref, b, cols = hd["out"]
        h_ref[b, :, cols] = hd["na"][:, :MLSTM_DH] / den


def _mlstm_call(q, kt, v, g, gt):
    fwd = lambda s: s
    bwd = lambda s: jnp.where(s < CTX_CH, CTX_CH - 1 - s, NCH_B + CTX_CH - 1 - s)
    q3 = q.reshape(BATCH, SEQ_B, MIX_W)
    kt3 = kt.reshape(BATCH, SEQ_B * MLSTM_HEADS, MLSTM_DH)
    v3 = v.reshape(BATCH, SEQ_B, MIX_W)
    g3 = g.reshape(BATCH, SEQ_B, LANES)
    gt3 = gt.reshape(BATCH, NCH_B * SUBLANES, CHUNK)

    def specs(order):
        return [
            pl.BlockSpec((BATCH, CHUNK, MIX_W), lambda s: (0, order(s), 0)),
            pl.BlockSpec((BATCH, CHUNK * MLSTM_HEADS, MLSTM_DH), lambda s: (0, order(s), 0)),
            pl.BlockSpec((BATCH, CHUNK, MIX_W), lambda s: (0, order(s), 0)),
            pl.BlockSpec((BATCH, CHUNK, LANES), lambda s: (0, order(s), 0)),
            pl.BlockSpec((BATCH, SUBLANES, CHUNK), lambda s: (0, order(s), 0)),
        ]

    n_state = 2 * BATCH * MLSTM_HEADS
    hf, hb = pl.pallas_call(
        _mlstm_kernel,
        grid=(NCH_B,),
        in_specs=specs(fwd) + specs(bwd),
        out_specs=[
            pl.BlockSpec((BATCH, CHUNK, MIX_W), lambda s: (0, fwd(s), 0)),
            pl.BlockSpec((BATCH, CHUNK, MIX_W), lambda s: (0, bwd(s), 0)),
        ],
        out_shape=[jax.ShapeDtypeStruct((BATCH, SEQ_B, MIX_W), F32)] * 2,
        scratch_shapes=[
            pltpu.VMEM((n_state, MLSTM_DH, 2 * MLSTM_DH), F32),
            pltpu.VMEM((n_state, SUBLANES, LANES), F32),
        ],
        compiler_params=_params(("arbitrary",)),
        name="mlstm",
    )(q3, kt3, v3, g3, gt3, q3, kt3, v3, g3, gt3)
    return hf.reshape(T_ALL, MIX_W), hb.reshape(T_ALL, MIX_W)


VEC_GMLP_G, VEC_GMLP_B, VEC_CONV_B, VEC_CONV_G, VEC_CONV_LB, VEC_MLSTM_G, VEC_POOL_S = range(7)
VEC_ROWS = 8


def _mix_kernel(x_ref, prev_ref, next_ref, mod_ref, hf_ref, hb_ref, w_ref, b_ref, vec_ref, ws_ref, bst_ref,
                cw_ref, pw_ref, wbr_ref, wout_ref, g_ref, beta_ref, o_ref, xe_ref, a_ref, d_ref, ash_ref, conv_ref,
                gate_ref, pa_ref, po_ref, u_ref, vn_ref, yc_ref, yd_ref):
    j = pl.program_id(0) % NT_B
    is_ctx = j == 0
    first = jnp.logical_or(is_ctx, j == 1)
    last = jnp.logical_or(is_ctx, j == NT_B - 1)
    shift = mod_ref[3:4, :]
    scale = mod_ref[4:5, :]
    gate = mod_ref[5:6, :]

    def modulate(v):
        return (v * (1.0 + scale) + shift).astype(BF16)

    def vec(r):
        return vec_ref[r:r + 1, :]

    def proj(ext, lo, width):
        lhs = xe_ref[...] if ext else xe_ref[HALO:HALO + TM, :]
        return jnp.concatenate([_dot(lhs, w_ref[:, c:c + MXU_N]) + b_ref[:, c:c + MXU_N]
                                for c in range(lo, lo + width, MXU_N)], axis=1)

    xe_ref[0:HALO, :] = modulate(prev_ref[...])
    xe_ref[HALO:HALO + TM, :] = modulate(x_ref[...])
    xe_ref[HALO + TM:2 * HALO + TM, :] = modulate(next_ref[...])
    rows = lax.broadcasted_iota(jnp.int32, (TM + 2 * HALO, 1), 0)
    valid = jnp.logical_and(jnp.logical_or(rows >= HALO, jnp.logical_not(first)),
                            jnp.logical_or(rows < HALO + TM, jnp.logical_not(last)))

    pb = proj(True, MX_B, 2 * MIX_W)
    a_ref[...] = jnp.where(valid, pb[:, :MIX_W] * _sigmoid(pb[:, MIX_W:]), 0.0)
    pa_ref[...] = proj(False, MX_A, 2 * MIX_W)

    gate_chunks = [(i, n) for i in range(N_BRANCH) for n in range(D_MODEL // MXU_N)]
    sh_rows = TM + 2 * HALO - SUBLANES
    for s in range(1, SUBLANES):
        ash_ref[s - 1, 0:sh_rows, :] = a_ref[s:s + sh_rows, :]

    def zero_of(v):
        bits = lax.bitcast_convert_type(v[0:1, 0:LANES], jnp.uint32)
        return lax.shift_right_logical(lax.shift_right_logical(bits, jnp.uint32(16)), jnp.uint32(16)).astype(F32)

    def conv_block(cb, rb, z):
        cols = slice(cb * LANES, (cb + 1) * LANES)
        part = z
        for k in range(CONV_W):
            q8, s = divmod(HALO - CONV_W // 2 + k, SUBLANES)
            r0 = q8 * SUBLANES + rb * CONV_RB
            src = a_ref[r0:r0 + CONV_RB, cols] if s == 0 else ash_ref[s - 1, r0:r0 + CONV_RB, cols]
            part = part + cw_ref[k:k + 1, cols] * src
        conv_ref[rb * CONV_RB:(rb + 1) * CONV_RB, cols] = part
        return part

    conv_blocks = [(cb, rb) for cb in range(MIX_W // LANES) for rb in range(TM // CONV_RB)]
    per_step = len(conv_blocks) // len(gate_chunks)
    zero_row = jnp.zeros((1, LANES), F32)
    v_zero = [zero_row] * len(gate_chunks)
    for step, (i, n) in enumerate(gate_chunks):
        lo = i * D_MODEL + n * MXU_N
        z_in = jnp.concatenate([v_zero[step - CHAIN_LAG]] * (MXU_N // LANES), axis=1) if step >= CHAIN_LAG else 0.0
        gv = jnp.tanh(0.5 * (proj(False, MX_G + lo, MXU_N) + z_in))
        gate_ref[:, lo:lo + MXU_N] = gv
        z = zero_of(gv) + v_zero[step - 1]
        for cb, rb in conv_blocks[step * per_step:(step + 1) * per_step]:
            z = zero_of(conv_block(cb, rb, z))
        v_zero[step] = z

    pa = jax.nn.gelu(pa_ref[...])
    u_ref[...] = pa[:, :MIX_W]
    vn_ref[...] = _ln(pa[:, MIX_W:], vec(VEC_GMLP_G), vec(VEC_GMLP_B)).astype(BF16)

    d_ref[...] = jnp.where(valid, proj(True, MX_D, MIX_W), 0.0)
    pos = jnp.where(is_ctx, 0, (j - 1) * TM) + lax.broadcasted_iota(jnp.int32, (TM, 1), 0)
    n_seq = jnp.where(is_ctx, CTX_LEN, SEQ)
    for gi, win in enumerate(POOL_WINDOWS):
        lo, hi = win // 2, win - 1 - win // 2
        cols = slice(gi * POOL_GD, (gi + 1) * POOL_GD)
        wsum = d_ref[HALO - lo:HALO - lo + TM, cols]
        for k in range(-lo + 1, hi + 1):
            wsum = wsum + d_ref[HALO + k:HALO + k + TM, cols]
        cnt = (jnp.minimum(pos + hi + 1, n_seq) - jnp.maximum(pos - lo, 0)).astype(F32)
        diff = wsum / cnt - d_ref[HALO:HALO + TM, cols]
        yd_ref[:, cols] = (_dot(diff.astype(BF16), pw_ref[gi]) * vec_ref[VEC_POOL_S:VEC_POOL_S + 1, cols]
                           ).astype(BF16)

    po_ref[...] = proj(False, MX_O, MIX_W)
    for h in range(MLSTM_HEADS):
        cols = slice(h * MLSTM_DH, (h + 1) * MLSTM_DH)
        hh = hf_ref[:, cols] + hb_ref[:, cols]
        mu = jnp.mean(hh, axis=-1, keepdims=True)
        hc = hh - mu
        hn = hc * lax.rsqrt(jnp.mean(hc * hc, axis=-1, keepdims=True) + LN_EPS)
        yc_ref[:, cols] = (_sigmoid(po_ref[:, cols]) * (hn * vec_ref[VEC_MLSTM_G:VEC_MLSTM_G + 1, cols])
                           ).astype(BF16)

    gd = MIX_W // GMLP_GROUPS
    z_rows = []
    for c in range(CH_T):
        z_cols = []
        for gi in range(GMLP_GROUPS):
            blk = vn_ref[c * CHUNK:(c + 1) * CHUNK, gi * gd:(gi + 1) * gd]
            z_cols.append(_dot(ws_ref[gi], blk) + bst_ref[:, gi:gi + 1])
        z_rows.append(jnp.concatenate(z_cols, axis=1))
    ya = (u_ref[...] * jnp.concatenate(z_rows, axis=0)).astype(BF16)
    yb = _silu(_ln(conv_ref[...] + vec(VEC_CONV_B), vec(VEC_CONV_G), vec(VEC_CONV_LB))).astype(BF16)
    yc = yc_ref[...]
    yd = yd_ref[...]

    merged = []
    for n in range(D_MODEL // MXU_N):
        cols = slice(n * MXU_N, (n + 1) * MXU_N)
        acc = None
        for i, yi in enumerate((ya, yb, yc, yd)):
            lo = i * D_MODEL + n * MXU_N
            p = _dot(yi, wbr_ref[i, :, cols])
            term = gate_ref[:, lo:lo + MXU_N] * p + p
            acc = term if acc is None else acc + term
        merged.append((0.5 * acc).astype(BF16))
    merged = jnp.concatenate(merged, axis=1)
    y = jnp.concatenate([_dot(merged, wout_ref[:, n * MXU_N:(n + 1) * MXU_N]) for n in range(D_MODEL // MXU_N)],
                        axis=1)
    o_ref[...] = _ln(ALPHA * x_ref[...] + gate * y, g_ref[...], beta_ref[...])


def _mix_call(s, mods, hf, hb, w_mx, b_mx, vec512, gmlp_ws, gmlp_bst, conv_w, pool_w, w_branch, w_out,
              ln_g, ln_b, *, layer):
    hb_per_tile = TM // HALO
    n_hb = T_ALL // HALO
    return pl.pallas_call(
        _mix_kernel,
        grid=(NT,),
        in_specs=[
            pl.BlockSpec((TM, D_MODEL), lambda i: (i, 0)),
            pl.BlockSpec((HALO, D_MODEL), lambda i: (jnp.maximum(i * hb_per_tile - 1, 0), 0)),
            pl.BlockSpec((HALO, D_MODEL), lambda i: (jnp.minimum((i + 1) * hb_per_tile, n_hb - 1), 0)),
            pl.BlockSpec((None, 9, D_MODEL), lambda i: (layer * MOD_ROWS + _mod_row(i), 0, 0)),
            pl.BlockSpec((TM, MIX_W), lambda i: (i, 0)),
            pl.BlockSpec((TM, MIX_W), lambda i: (i, 0)),
            _const_spec((None, D_MODEL, MX_COLS + LANE_PAD), (layer, 0, 0)),
            _const_spec((None, 1, MX_COLS), (layer, 0, 0)),
            _const_spec((None, VEC_ROWS, MIX_W), (layer, 0, 0)),
            _const_spec((None, GMLP_GROUPS, CHUNK, CHUNK), (layer, 0, 0, 0)),
            _const_spec((None, CHUNK, GMLP_GROUPS), (layer, 0, 0)),
            _const_spec((None, CONV_W, MIX_W), (layer, 0, 0)),
            _const_spec((None, len(POOL_WINDOWS), POOL_GD, POOL_GD), (layer, 0, 0, 0)),
            _const_spec((None, N_BRANCH, MIX_W, D_MODEL + LANE_PAD), (layer, 0, 0, 0)),
            _const_spec((None, D_MODEL, D_MODEL + LANE_PAD), (layer, 0, 0)),
            _const_spec((None, 1, D_MODEL), (layer * 3 + 1, 0, 0)),
            _const_spec((None, 1, D_MODEL), (layer * 3 + 1, 0, 0)),
        ],
        out_specs=pl.BlockSpec((TM, D_MODEL), lambda i: (i, 0)),
        out_shape=jax.ShapeDtypeStruct((T_ALL, D_MODEL), F32),
        scratch_shapes=[
            pltpu.VMEM((TM + 2 * HALO, D_MODEL), BF16),
            pltpu.VMEM((TM + 2 * HALO, MIX_W), F32),
            pltpu.VMEM((TM + 2 * HALO, MIX_W), F32),
            pltpu.VMEM((SUBLANES - 1, TM + 2 * HALO, MIX_W), F32),
            pltpu.VMEM((TM, MIX_W), F32),
            pltpu.VMEM((TM, N_BRANCH * D_MODEL), F32),
            pltpu.VMEM((TM, 2 * MIX_W), F32),
            pltpu.VMEM((TM, MIX_W), F32),
            pltpu.VMEM((TM, MIX_W), F32),
            pltpu.VMEM((TM, MIX_W), BF16),
            pltpu.VMEM((TM, MIX_W), BF16),
            pltpu.VMEM((TM, MIX_W), BF16),
        ],
        compiler_params=_params(("parallel",)),
        name="mix",
    )(s, s, s, mods, hf, hb, w_mx, b_mx, vec512, gmlp_ws, gmlp_bst, conv_w, pool_w, w_branch, w_out, ln_g, ln_b)


def _repack_cols(w, pieces):
    parts = [jnp.zeros(w.shape[:-1] + (p,), w.dtype) if isinstance(p, int) else w[..., p[0]:p[1]] for p in pieces]
    return jnp.concatenate(parts, axis=-1)


def kernel(x, c, ctx, c_ctx, w_ada, b_ada, ln_g, ln_b, ffn_w_in, ffn_w_out, w_in, b_in, gmlp_ln_g, gmlp_ln_b,
           gmlp_ws, gmlp_bs, conv_w, conv_b, conv_ln_g, conv_ln_b, qk_conv_w, mlstm_ln_g, pool_w, pool_scale,
           w_branch, w_out):
    mix_pieces = [(OFF_A, OFF_C), (OFF_C_O, OFF_D), (OFF_D, OFF_G), (OFF_G, IN_COLS)]
    nh = MLSTM_HEADS
    gate_pad = LANES - 2 * nh

    def cproj_cols(w):
        g = w[..., OFF_C_GATES:OFF_C_O].reshape(w.shape[:-1] + (4, nh))
        pad = jnp.zeros(w.shape[:-1] + (gate_pad,), w.dtype)
        return jnp.concatenate([w[..., OFF_C:OFF_C_GATES], g[..., 0, :], g[..., 2, :], pad,
                                g[..., 1, :], g[..., 3, :], pad], axis=-1)
    w_mx = _repack_cols(w_in, mix_pieces + [LANE_PAD]).astype(BF16)
    b_mx = _repack_cols(b_in, mix_pieces).reshape(DEPTH, 1, MX_COLS)
    w_cp = cproj_cols(w_in).astype(BF16)
    b_cp = cproj_cols(b_in).reshape(DEPTH, 1, CP_COLS)
    ffn_w_in_b = ffn_w_in.astype(BF16)
    ffn_w_out_b = _repack_cols(ffn_w_out, [(0, D_MODEL), LANE_PAD]).astype(BF16)
    w_branch_b = _repack_cols(w_branch, [(0, D_MODEL), LANE_PAD]).astype(BF16)
    w_out_b = _repack_cols(w_out, [(0, D_MODEL), LANE_PAD]).astype(BF16)
    gmlp_ws_b = gmlp_ws.astype(BF16)
    pool_w_b = pool_w.astype(BF16)
    gmlp_bst = jnp.swapaxes(gmlp_bs, 1, 2)
    vec512 = jnp.stack([gmlp_ln_g, gmlp_ln_b, conv_b, conv_ln_g, conv_ln_b, mlstm_ln_g, pool_scale,
                        jnp.zeros_like(pool_scale)], axis=1)
    ln_g3 = ln_g.reshape(DEPTH * 3, 1, D_MODEL)
    ln_b3 = ln_b.reshape(DEPTH * 3, 1, D_MODEL)
    c_t = jnp.concatenate([c, c_ctx[None], jnp.zeros((MOD_ROWS - BATCH - 1, D_MODEL), F32)], axis=0).T

    mods = _mods_call(c_t, w_ada, b_ada).reshape(DEPTH * MOD_ROWS, 9, D_MODEL)
    s = _embed_call(x, ctx)
    for l in range(DEPTH):
        last = l == DEPTH - 1
        s = _ffn_call(s, mods, ffn_w_in_b, ffn_w_out_b, ln_g3, ln_b3, layer=l, which=0, final=False)
        q, kt, v, g, gt = _cproj_call(s, mods, w_cp, b_cp, qk_conv_w, layer=l)
        hf, hb = _mlstm_call(q, kt, v, g, gt)
        s = _mix_call(s, mods, hf, hb, w_mx, b_mx, vec512, gmlp_ws_b, gmlp_bst, conv_w, pool_w_b, w_branch_b,
                      w_out_b, ln_g3, ln_b3, layer=l)
        s = _ffn_call(s, mods, ffn_w_in_b, ffn_w_out_b, ln_g3, ln_b3, layer=l, which=1, final=last)
    return s.reshape(BATCH, SEQ, D_MODEL)
```

```python
import functools
import math

import jax
import jax.numpy as jnp
from jax import lax
from jax.experimental import pallas as pl
from jax.experimental.pallas import tpu as pltpu

D_MODEL = 1024
BATCH = 2
SEQ = 8192
DEPTH = 4
GRID_W = 64
CTX_LEN = 256
MIX_W = D_MODEL // 2
N_BRANCH = 4
CHUNK = 128
GMLP_GROUPS = 4
CONV_W = 31
MLSTM_HEADS = 4
MLSTM_DH = MIX_W // MLSTM_HEADS
QK_CONV = 3
POOL_WINDOWS = (2, 4, 8, 16)
POOL_GD = MIX_W // len(POOL_WINDOWS)
D_FF = 128 * ((8 * D_MODEL // 3 + 127) // 128)
ALPHA = (2 * DEPTH) ** 0.25
LN_EPS = 1e-6
FFN_RES = 0.5

OFF_A = 0
OFF_B = OFF_A + 2 * MIX_W
OFF_C = OFF_B + 2 * MIX_W
OFF_C_GATES = OFF_C + 3 * MIX_W
OFF_C_O = OFF_C_GATES + 4 * MLSTM_HEADS
OFF_D = OFF_C_O + MIX_W
OFF_G = OFF_D + MIX_W
IN_COLS = OFF_G + N_BRANCH * D_MODEL

LANES = 128
SUBLANES = 8
BF16_ROWS = 16
TM = 256
SEQ_B = CTX_LEN + SEQ
T_ALL = BATCH * SEQ_B
NT_B = SEQ_B // TM
NT = BATCH * NT_B
NCH_B = SEQ_B // CHUNK
CH_T = TM // CHUNK
CTX_CH = CTX_LEN // CHUNK
HALO = 16
MXU_N = 256
LANE_PAD = LANES
FF_CH = MXU_N
FFN_TILES = 2
CONV_RB = 64
CHAIN_LAG = 3
GS_RMAX = 2 * MLSTM_HEADS
MOD_ROWS = 8
CTX_MOD_ROW = BATCH
VMEM_LIMIT = 56 * 1024 * 1024

MX_A = 0
MX_B = MX_A + 2 * MIX_W
MX_O = MX_B + 2 * MIX_W
MX_D = MX_O + MIX_W
MX_G = MX_D + MIX_W
MX_COLS = MX_G + N_BRANCH * D_MODEL
CP_QK = 0
CP_V = 2 * MIX_W
CP_GI = 3 * MIX_W
CP_GF = CP_GI + LANES
CP_COLS = CP_GF + LANES

F32 = jnp.float32
BF16 = jnp.bfloat16


def _dot(a, b):
    return jnp.dot(a, b, preferred_element_type=F32)


def _dot_f32(a, b):
    return jnp.dot(a, b, preferred_element_type=F32, precision=lax.Precision.HIGHEST)


def _ln(r, g, b):
    mu = jnp.mean(r, axis=-1, keepdims=True)
    xc = r - mu
    var = jnp.mean(xc * xc, axis=-1, keepdims=True)
    return xc * lax.rsqrt(var + LN_EPS) * g + b


def _sigmoid(x):
    return 0.5 * jnp.tanh(0.5 * x) + 0.5


def _silu(x):
    hx = 0.5 * x
    return hx * jnp.tanh(hx) + hx


def _gelu_tanh(x):
    c = math.sqrt(2.0 / math.pi)
    hx = 0.5 * x
    return hx * jnp.tanh(x * (c + (c * 0.044715) * (x * x))) + hx


def _mod_row(i):
    return jnp.where(i % NT_B == 0, CTX_MOD_ROW, i // NT_B)


def _const_spec(block, index):
    return pl.BlockSpec(block, lambda *_: index, pipeline_mode=pl.Buffered(1))


def _params(sem, flags=None):
    return pltpu.CompilerParams(dimension_semantics=sem, vmem_limit_bytes=VMEM_LIMIT, flags=flags)


def _mods_kernel(ct_ref, w_ref, b_ref, o_ref):
    s = _silu(ct_ref[...])
    w = w_ref[...]
    o_ref[...] = jnp.zeros(o_ref.shape, F32)
    for r in range(BATCH + 1):
        o_ref[r:r + 1, :] = jnp.sum(s[:, r:r + 1] * w, axis=0, keepdims=True) + b_ref[...]


def _mods_call(c_t, w_ada, b_ada):
    tn = D_MODEL
    n_col = w_ada.shape[-1] // tn
    return pl.pallas_call(
        _mods_kernel,
        grid=(DEPTH, n_col),
        in_specs=[
            pl.BlockSpec((D_MODEL, MOD_ROWS), lambda l, n: (0, 0)),
            pl.BlockSpec((None, D_MODEL, tn), lambda l, n: (l, 0, n)),
            pl.BlockSpec((None, 1, tn), lambda l, n: (l, 0, n)),
        ],
        out_specs=pl.BlockSpec((None, MOD_ROWS, tn), lambda l, n: (l, 0, n)),
        out_shape=jax.ShapeDtypeStruct((DEPTH, MOD_ROWS, w_ada.shape[-1]), F32),
        compiler_params=_params(("parallel", "parallel")),
        name="mods",
    )(c_t, w_ada, b_ada.reshape(DEPTH, 1, -1))


def _embed_kernel(x_ref, ctx_ref, o_ref):
    j = pl.program_id(0) % NT_B

    @pl.when(j == 0)
    def _():
        o_ref[...] = ctx_ref[...]

    @pl.when(j > 0)
    def _():
        quarter = D_MODEL // 4
        grid_rows = TM // GRID_W
        k = lax.broadcasted_iota(jnp.int32, (1, quarter), 1).astype(F32)
        freqs = jnp.exp(-math.log(10000.0) * k / quarter)
        r = ((j - 1) * grid_rows + lax.broadcasted_iota(jnp.int32, (SUBLANES, 1), 0)).astype(F32)
        col = lax.broadcasted_iota(jnp.int32, (GRID_W, 1), 0).astype(F32)
        er = r * freqs
        ec = col * freqs
        sin_r, cos_r, sin_c, cos_c = jnp.sin(er), jnp.cos(er), jnp.sin(ec), jnp.cos(ec)
        for q in range(grid_rows):
            rows = slice(q * GRID_W, (q + 1) * GRID_W)
            o_ref[rows, 0 * quarter:1 * quarter] = x_ref[rows, 0 * quarter:1 * quarter] + sin_r[q:q + 1, :]
            o_ref[rows, 1 * quarter:2 * quarter] = x_ref[rows, 1 * quarter:2 * quarter] + cos_r[q:q + 1, :]
            o_ref[rows, 2 * quarter:3 * quarter] = x_ref[rows, 2 * quarter:3 * quarter] + sin_c
            o_ref[rows, 3 * quarter:4 * quarter] = x_ref[rows, 3 * quarter:4 * quarter] + cos_c


def _embed_call(x, ctx):
    lat_tiles_b = SEQ // TM
    return pl.pallas_call(
        _embed_kernel,
        grid=(NT,),
        in_specs=[
            pl.BlockSpec((TM, D_MODEL), lambda i: ((i // NT_B) * lat_tiles_b + jnp.maximum(i % NT_B - 1, 0), 0)),
            pl.BlockSpec((TM, D_MODEL), lambda i: (i // NT_B, 0)),
        ],
        out_specs=pl.BlockSpec((TM, D_MODEL), lambda i: (i, 0)),
        out_shape=jax.ShapeDtypeStruct((T_ALL, D_MODEL), F32),
        compiler_params=_params(("parallel",)),
        name="embed",
    )(x.reshape(BATCH * SEQ, D_MODEL), ctx.reshape(BATCH * CTX_LEN, D_MODEL))


def _ffn_kernel(*refs, sub):
    x_refs = refs[:FFN_TILES]
    mod_refs = refs[FFN_TILES:2 * FFN_TILES]
    win_ref, wout_ref, g_ref, b_ref, o_ref, xm_ref, h_ref = refs[2 * FFN_TILES:]
    for t, (x_ref, mod_ref) in enumerate(zip(x_refs, mod_refs)):
        shift = mod_ref[3 * sub + 0:3 * sub + 1, :]
        scale = mod_ref[3 * sub + 1:3 * sub + 2, :]
        xm_ref[t * TM:(t + 1) * TM, :] = (x_ref[...] * (1.0 + scale) + shift).astype(BF16)
    xm = xm_ref[...]
    for c in range(D_FF // FF_CH):
        a1 = _dot(xm, win_ref[:, c * FF_CH:(c + 1) * FF_CH])
        a2 = _dot(xm, win_ref[:, D_FF + c * FF_CH:D_FF + (c + 1) * FF_CH])
        h_ref[:, c * FF_CH:(c + 1) * FF_CH] = (_silu(a1) * a2).astype(BF16)
    for t, (x_ref, mod_ref) in enumerate(zip(x_refs, mod_refs)):
        hid = h_ref[t * TM:(t + 1) * TM, :]
        y = jnp.concatenate([_dot(hid, wout_ref[:, n * MXU_N:(n + 1) * MXU_N]) for n in range(D_MODEL // MXU_N)],
                            axis=1)
        gate = mod_ref[3 * sub + 2:3 * sub + 3, :]
        r = ALPHA * x_ref[...] + (FFN_RES * gate) * y
        o_ref[t * TM:(t + 1) * TM, :] = _ln(r, g_ref[...], b_ref[...])


def _ffn_call(s, mods, ffn_w_in, ffn_w_out, ln_g, ln_b, *, layer, which, final):
    sub = 2 * which
    if final:
        lat_tiles_b = SEQ // TM
        out_tiles = BATCH * lat_tiles_b
        in_tile = lambda t: (t // lat_tiles_b) * NT_B + 1 + t % lat_tiles_b
    else:
        out_tiles = NT
        in_tile = lambda t: t
    tile_of = lambda i, t: in_tile(i * FFN_TILES + t)
    x_specs = [pl.BlockSpec((TM, D_MODEL), functools.partial(lambda i, t: (tile_of(i, t), 0), t=t))
               for t in range(FFN_TILES)]
    mod_specs = [pl.BlockSpec((None, 9, D_MODEL),
                              functools.partial(lambda i, t: (layer * MOD_ROWS + _mod_row(tile_of(i, t)), 0, 0), t=t))
                 for t in range(FFN_TILES)]
    return pl.pallas_call(
        functools.partial(_ffn_kernel, sub=sub),
        grid=(out_tiles // FFN_TILES,),
        in_specs=x_specs + mod_specs + [
            _const_spec((None, None, D_MODEL, 2 * D_FF), (layer, which, 0, 0)),
            _const_spec((None, None, D_FF, D_MODEL + LANE_PAD), (layer, which, 0, 0)),
            _const_spec((None, 1, D_MODEL), (layer * 3 + sub, 0, 0)),
            _const_spec((None, 1, D_MODEL), (layer * 3 + sub, 0, 0)),
        ],
        out_specs=pl.BlockSpec((FFN_TILES * TM, D_MODEL), lambda i: (i, 0)),
        out_shape=jax.ShapeDtypeStruct((out_tiles * TM, D_MODEL), F32),
        scratch_shapes=[
            pltpu.VMEM((FFN_TILES * TM, D_MODEL), BF16),
            pltpu.VMEM((FFN_TILES * TM, D_FF), BF16),
        ],
        compiler_params=_params(("parallel",)),
        name=f"ffn{which}",
    )(*([s] * FFN_TILES), *([mods] * FFN_TILES), ffn_w_in, ffn_w_out, ln_g, ln_b)


def _cproj_kernel(x_ref, prev_ref, next_ref, mod_ref, w_ref, b_ref, cw_ref,
                  q_ref, kt_ref, v_ref, g_ref, gt_ref, xe_ref, p_ref):
    j = pl.program_id(0) % NT_B
    first = jnp.logical_or(j == 0, j == 1)
    last = jnp.logical_or(j == 0, j == NT_B - 1)
    shift = mod_ref[3:4, :]
    scale = mod_ref[4:5, :]

    def modulate(v):
        return (v * (1.0 + scale) + shift).astype(BF16)

    xe_ref[0:HALO, :] = modulate(prev_ref[...])
    xe_ref[HALO:HALO + TM, :] = modulate(x_ref[...])
    xe_ref[HALO + TM:2 * HALO + TM, :] = modulate(next_ref[...])
    xe = xe_ref[...]
    rows = lax.broadcasted_iota(jnp.int32, (TM + 2 * HALO, 1), 0)
    valid = jnp.logical_and(jnp.logical_or(rows >= HALO, jnp.logical_not(first)),
                            jnp.logical_or(rows < HALO + TM, jnp.logical_not(last)))
    p_ref[...] = jnp.where(valid, _dot(xe, w_ref[:, CP_QK:CP_V]) + b_ref[:, CP_QK:CP_V], 0.0)
    conv = (cw_ref[0:1, :] * p_ref[HALO - 1:HALO - 1 + TM, :]
            + cw_ref[1:2, :] * p_ref[HALO:HALO + TM, :]
            + cw_ref[2:3, :] * p_ref[HALO + 1:HALO + 1 + TM, :])
    qk = _silu(conv)
    q_ref[...] = qk[:, :MIX_W].astype(BF16)
    k = qk[:, MIX_W:] * MLSTM_DH ** -0.5
    for c in range(CH_T):
        for h in range(MLSTM_HEADS):
            blk = k[c * CHUNK:(c + 1) * CHUNK, h * MLSTM_DH:(h + 1) * MLSTM_DH]
            r0 = (c * MLSTM_HEADS + h) * MLSTM_DH
            kt_ref[r0:r0 + MLSTM_DH, :] = blk.T.astype(BF16)

    xm = xe_ref[HALO:HALO + TM, :]
    pvg = _dot(xm, w_ref[:, CP_V:CP_COLS]) + b_ref[:, CP_V:CP_COLS]
    v_ref[...] = pvg[:, :MIX_W].astype(BF16)
    li = pvg[:, CP_GI - CP_V:CP_GF - CP_V]
    fraw = pvg[:, CP_GF - CP_V:]
    lf = jnp.minimum(fraw, 0.0) - jnp.log1p(jnp.exp(-jnp.abs(fraw)))
    row = lax.broadcasted_iota(jnp.int32, (CHUNK, LANES), 0)
    lane = lax.broadcasted_iota(jnp.int32, (CHUNK, LANES), 1)
    fwd_lane = lane < MLSTM_HEADS
    lower_f = (lane <= row).astype(F32)
    upper_f = (lane >= row).astype(F32)
    for c in range(CH_T):
        tok = slice(c * CHUNK, (c + 1) * CHUNK)
        cum = jnp.where(fwd_lane, _dot_f32(lower_f, lf[tok]), _dot_f32(upper_f, lf[tok]))
        r = li[tok] - cum
        rf, rb = r, r
        k = 1
        while k < CHUNK:
            rf = jnp.maximum(rf, jnp.where(row >= k, pltpu.roll(rf, k, 0), -jnp.inf))
            rb = jnp.maximum(rb, jnp.where(row < CHUNK - k, pltpu.roll(rb, CHUNK - k, 0), -jnp.inf))
            k *= 2
        rmax = jnp.where(fwd_lane, rf, rb)
        g_ref[tok, :] = jnp.where(lane < GS_RMAX, cum, pltpu.roll(rmax, GS_RMAX, 1))
        gt_ref[c * SUBLANES:(c + 1) * SUBLANES, :] = r.T[:SUBLANES, :]


def _cproj_call(s, mods, w_cp, b_cp, qk_conv_w, *, layer):
    hb = TM // HALO
    n_hb = T_ALL // HALO
    return pl.pallas_call(
        _cproj_kernel,
        grid=(NT,),
        in_specs=[
            pl.BlockSpec((TM, D_MODEL), lambda i: (i, 0)),
            pl.BlockSpec((HALO, D_MODEL), lambda i: (jnp.maximum(i * hb - 1, 0), 0)),
            pl.BlockSpec((HALO, D_MODEL), lambda i: (jnp.minimum((i + 1) * hb, n_hb - 1), 0)),
            pl.BlockSpec((None, 9, D_MODEL), lambda i: (layer * MOD_ROWS + _mod_row(i), 0, 0)),
            _const_spec((None, D_MODEL, CP_COLS), (layer, 0, 0)),
            _const_spec((None, 1, CP_COLS), (layer, 0, 0)),
            _const_spec((None, QK_CONV, 2 * MIX_W), (layer, 0, 0)),
        ],
        out_specs=[
            pl.BlockSpec((TM, MIX_W), lambda i: (i, 0)),
            pl.BlockSpec((TM * MLSTM_HEADS, MLSTM_DH), lambda i: (i, 0)),
            pl.BlockSpec((TM, MIX_W), lambda i: (i, 0)),
            pl.BlockSpec((TM, LANES), lambda i: (i, 0)),
            pl.BlockSpec((CH_T * SUBLANES, CHUNK), lambda i: (i, 0)),
        ],
        out_shape=[
            jax.ShapeDtypeStruct((T_ALL, MIX_W), BF16),
            jax.ShapeDtypeStruct((T_ALL * MLSTM_HEADS, MLSTM_DH), BF16),
            jax.ShapeDtypeStruct((T_ALL, MIX_W), BF16),
            jax.ShapeDtypeStruct((T_ALL, LANES), F32),
            jax.ShapeDtypeStruct((T_ALL // CHUNK * SUBLANES, CHUNK), F32),
        ],
        scratch_shapes=[
            pltpu.VMEM((TM + 2 * HALO, D_MODEL), BF16),
            pltpu.VMEM((TM + 2 * HALO, 2 * MIX_W), F32),
        ],
        compiler_params=_params(("parallel",)),
        name="cproj",
    )(s, s, s, mods, w_cp, b_cp, qk_conv_w)


def _mlstm_kernel(qf_ref, ktf_ref, vf_ref, gf_ref, gtf_ref, qb_ref, ktb_ref, vb_ref, gb_ref, gtb_ref,
                  hf_ref, hb_ref, c_ref, m_ref):
    @pl.when(pl.program_id(0) == 0)
    def _():
        c_ref[...] = jnp.zeros(c_ref.shape, F32)
        m_ref[...] = jnp.zeros(m_ref.shape, F32)

    row = lax.broadcasted_iota(jnp.int32, (CHUNK, CHUNK), 0)
    col = lax.broadcasted_iota(jnp.int32, (CHUNK, CHUNK), 1)
    ones_col = jnp.ones((CHUNK, MLSTM_DH), BF16)
    nh = MLSTM_HEADS
    dirs = ((qf_ref, ktf_ref, vf_ref, gf_ref, gtf_ref, hf_ref, col <= row, CHUNK - 1),
            (qb_ref, ktb_ref, vb_ref, gb_ref, gtb_ref, hb_ref, col >= row, 0))
    heads = []
    for d, (q_ref, kt_ref, v_ref, g_ref, gt_ref, h_ref, seen, end_row) in enumerate(dirs):
        for b in range(BATCH):
            g = g_ref[b]
            gt = gt_ref[b]
            for h in range(nh):
                lane = d * nh + h
                hd = slice(h * MLSTM_DH, (h + 1) * MLSTM_DH)
                heads.append(dict(
                    idx=(d * BATCH + b) * nh + h, seen=seen, end_row=end_row, out=(h_ref, b, hd),
                    cum=g[:, lane:lane + 1], rmax=g[:, GS_RMAX + lane:GS_RMAX + lane + 1], r=gt[lane:lane + 1, :],
                    q=q_ref[b, :, hd], kt=kt_ref[b, hd, :], v=v_ref[b, :, hd]))

    for hd in heads:
        hd["qk"] = _dot(hd["q"], hd["kt"])
        hd["m_prev"] = m_ref[hd["idx"], 0:1, 0:1]
        hd["c_old"] = c_ref[hd["idx"]]
        hd["v_aug"] = jnp.concatenate([hd["v"], ones_col], axis=1)
    for hd in heads:
        e = hd["end_row"]
        top = jnp.maximum(hd["m_prev"], hd["rmax"][e:e + 1, :])
        w_row = jnp.exp(hd["r"] - top)
        ktw = (hd["kt"].astype(F32) * w_row).astype(BF16)
        c_ref[hd["idx"]] = jnp.exp(hd["m_prev"] - top) * hd["c_old"] + _dot(ktw, hd["v_aug"])
        m_ref[hd["idx"]] = jnp.broadcast_to(hd["cum"][e:e + 1, :] + top, m_ref.shape[1:])
    for hd in heads:
        m_prev = hd["m_prev"]
        top = jnp.broadcast_to(jnp.maximum(m_prev, hd["rmax"]), (CHUNK, CHUNK))
        s = hd["qk"] * jnp.exp(jnp.where(hd["seen"], hd["r"] - top, -jnp.inf))
        qw = hd["q"].astype(F32) * jnp.exp(m_prev - top)
        lhs = jnp.concatenate([s.astype(BF16), qw.astype(BF16)], axis=1)
        rhs = jnp.concatenate([hd["v_aug"], hd["c_old"].astype(BF16)], axis=0)
        hd["na"] = _dot(lhs, rhs)
        hd["floor"] = jnp.exp(-(hd["cum"] + top))
    for hd in heads:
        den = jnp.maximum(jnp.abs(hd["na"][:, MLSTM_DH:]), hd["floor"])
        h_ref, b, cols = hd["out"]
        h_ref[b, :, cols] = hd["na"][:, :MLSTM_DH] / den


def _mlstm_call(q, kt, v, g, gt):
    fwd = lambda s: s
    bwd = lambda s: jnp.where(s < CTX_CH, CTX_CH - 1 - s, NCH_B + CTX_CH - 1 - s)
    q3 = q.reshape(BATCH, SEQ_B, MIX_W)
    kt3 = kt.reshape(BATCH, SEQ_B * MLSTM_HEADS, MLSTM_DH)
    v3 = v.reshape(BATCH, SEQ_B, MIX_W)
    g3 = g.reshape(BATCH, SEQ_B, LANES)
    gt3 = gt.reshape(BATCH, NCH_B * SUBLANES, CHUNK)

    def specs(order):
        return [
            pl.BlockSpec((BATCH, CHUNK, MIX_W), lambda s: (0, order(s), 0)),
            pl.BlockSpec((BATCH, CHUNK * MLSTM_HEADS, MLSTM_DH), lambda s: (0, order(s), 0)),
            pl.BlockSpec((BATCH, CHUNK, MIX_W), lambda s: (0, order(s), 0)),
            pl.BlockSpec((BATCH, CHUNK, LANES), lambda s: (0, order(s), 0)),
            pl.BlockSpec((BATCH, SUBLANES, CHUNK), lambda s: (0, order(s), 0)),
        ]

    n_state = 2 * BATCH * MLSTM_HEADS
    hf, hb = pl.pallas_call(
        _mlstm_kernel,
        grid=(NCH_B,),
        in_specs=specs(fwd) + specs(bwd),
        out_specs=[
            pl.BlockSpec((BATCH, CHUNK, MIX_W), lambda s: (0, fwd(s), 0)),
            pl.BlockSpec((BATCH, CHUNK, MIX_W), lambda s: (0, bwd(s), 0)),
        ],
        out_shape=[jax.ShapeDtypeStruct((BATCH, SEQ_B, MIX_W), F32)] * 2,
        scratch_shapes=[
            pltpu.VMEM((n_state, MLSTM_DH, 2 * MLSTM_DH), F32),
            pltpu.VMEM((n_state, SUBLANES, LANES), F32),
        ],
        compiler_params=_params(("arbitrary",)),
        name="mlstm",
    )(q3, kt3, v3, g3, gt3, q3, kt3, v3, g3, gt3)
    return hf.reshape(T_ALL, MIX_W), hb.reshape(T_ALL, MIX_W)


VEC_GMLP_G, VEC_GMLP_B, VEC_CONV_B, VEC_CONV_G, VEC_CONV_LB, VEC_MLSTM_G, VEC_POOL_S = range(7)
VEC_ROWS = 8


def _mix_kernel(x_ref, prev_ref, next_ref, mod_ref, hf_ref, hb_ref, w_ref, b_ref, vec_ref, ws_ref, bst_ref,
                cw_ref, pw_ref, wbr_ref, wout_ref, g_ref, beta_ref, o_ref, xe_ref, a_ref, d_ref, ash_ref, conv_ref,
                gate_ref, pa_ref, po_ref, u_ref, vn_ref, yc_ref, yd_ref):
    j = pl.program_id(0) % NT_B
    is_ctx = j == 0
    first = jnp.logical_or(is_ctx, j == 1)
    last = jnp.logical_or(is_ctx, j == NT_B - 1)
    shift = mod_ref[3:4, :]
    scale = mod_ref[4:5, :]
    gate = mod_ref[5:6, :]

    def modulate(v):
        return (v * (1.0 + scale) + shift).astype(BF16)

    def vec(r):
        return vec_ref[r:r + 1, :]

    def proj(ext, lo, width, zero=None):
        lhs = xe_ref[...] if ext else xe_ref[HALO:HALO + TM, :]
        bias = lambda c: b_ref[:, c:c + MXU_N] if zero is None else b_ref[:, c:c + MXU_N] + zero
        return jnp.concatenate([_dot(lhs, w_ref[:, c:c + MXU_N]) + bias(c)
                                for c in range(lo, lo + width, MXU_N)], axis=1)

    xe_ref[0:HALO, :] = modulate(prev_ref[...])
    xe_ref[HALO:HALO + TM, :] = modulate(x_ref[...])
    xe_ref[HALO + TM:2 * HALO + TM, :] = modulate(next_ref[...])
    rows = lax.broadcasted_iota(jnp.int32, (TM + 2 * HALO, 1), 0)
    valid = jnp.logical_and(jnp.logical_or(rows >= HALO, jnp.logical_not(first)),
                            jnp.logical_or(rows < HALO + TM, jnp.logical_not(last)))

    pb = proj(True, MX_B, 2 * MIX_W)
    a_ref[...] = jnp.where(valid, pb[:, :MIX_W] * _sigmoid(pb[:, MIX_W:]), 0.0)
    pa_ref[...] = proj(False, MX_A, 2 * MIX_W)

    gate_chunks = [(i, n) for i in range(N_BRANCH) for n in range(D_MODEL // MXU_N)]
    sh_rows = TM + 2 * HALO - SUBLANES
    for s in range(1, SUBLANES):
        ash_ref[s - 1, 0:sh_rows, :] = a_ref[s:s + sh_rows, :]

    def zero_of(v):
        bits = lax.bitcast_convert_type(v[0:1, 0:LANES], jnp.uint32)
        return lax.shift_right_logical(lax.shift_right_logical(bits, jnp.uint32(16)), jnp.uint32(16)).astype(F32)

    def conv_block(cb, rb, z):
        cols = slice(cb * LANES, (cb + 1) * LANES)
        part = None
        for k in range(CONV_W):
            q8, s = divmod(HALO - CONV_W // 2 + k, SUBLANES)
            r0 = q8 * SUBLANES + rb * CONV_RB
            src = a_ref[r0:r0 + CONV_RB, cols] if s == 0 else ash_ref[s - 1, r0:r0 + CONV_RB, cols]
            term = (cw_ref[k:k + 1, cols] + z) * src
            part = term if part is None else part + term
        conv_ref[rb * CONV_RB:(rb + 1) * CONV_RB, cols] = part
        return part

    conv_blocks = [(cb, rb) for cb in range(MIX_W // LANES) for rb in range(TM // CONV_RB)]
    per_step = len(conv_blocks) // len(gate_chunks)
    zero_row = jnp.zeros((1, LANES), F32)
    v_zero = [zero_row] * len(gate_chunks)
    for step, (i, n) in enumerate(gate_chunks):
        lo = i * D_MODEL + n * MXU_N
        z_in = jnp.concatenate([v_zero[step - CHAIN_LAG]] * (MXU_N // LANES), axis=1) if step >= CHAIN_LAG else None
        gv = jnp.tanh(0.5 * proj(False, MX_G + lo, MXU_N, z_in))
        gate_ref[:, lo:lo + MXU_N] = gv
        z = zero_of(gv) + v_zero[step - 1]
        for cb, rb in conv_blocks[step * per_step:(step + 1) * per_step]:
            z = zero_of(conv_block(cb, rb, z))
        v_zero[step] = z

    pa = _gelu_tanh(pa_ref[...])
    u_ref[...] = pa[:, :MIX_W]
    vn_ref[...] = _ln(pa[:, MIX_W:], vec(VEC_GMLP_G), vec(VEC_GMLP_B)).astype(BF16)

    d_ref[...] = jnp.where(valid, proj(True, MX_D, MIX_W), 0.0)
    pos = jnp.where(is_ctx, 0, (j - 1) * TM) + lax.broadcasted_iota(jnp.int32, (TM, 1), 0)
    n_seq = jnp.where(is_ctx, CTX_LEN, SEQ)
    for gi, win in enumerate(POOL_WINDOWS):
        lo, hi = win // 2, win - 1 - win // 2
        cols = slice(gi * POOL_GD, (gi + 1) * POOL_GD)
        wsum = d_ref[HALO - lo:HALO - lo + TM, cols]
        for k in range(-lo + 1, hi + 1):
            wsum = wsum + d_ref[HALO + k:HALO + k + TM, cols]
        cnt = (jnp.minimum(pos + hi + 1, n_seq) - jnp.maximum(pos - lo, 0)).astype(F32)
        diff = wsum / cnt - d_ref[HALO:HALO + TM, cols]
        yd_ref[:, cols] = (_dot(diff.astype(BF16), pw_ref[gi]) * vec_ref[VEC_POOL_S:VEC_POOL_S + 1, cols]
                           ).astype(BF16)

    po_ref[...] = proj(False, MX_O, MIX_W)
    for h in range(MLSTM_HEADS):
        cols = slice(h * MLSTM_DH, (h + 1) * MLSTM_DH)
        hh = hf_ref[:, cols] + hb_ref[:, cols]
        mu = jnp.mean(hh, axis=-1, keepdims=True)
        hc = hh - mu
        hn = hc * lax.rsqrt(jnp.mean(hc * hc, axis=-1, keepdims=True) + LN_EPS)
        yc_ref[:, cols] = (_sigmoid(po_ref[:, cols]) * (hn * vec_ref[VEC_MLSTM_G:VEC_MLSTM_G + 1, cols])
                           ).astype(BF16)

    gd = MIX_W // GMLP_GROUPS
    z_rows = []
    for c in range(CH_T):
        z_cols = []
        for gi in range(GMLP_GROUPS):
            blk = vn_ref[c * CHUNK:(c + 1) * CHUNK, gi * gd:(gi + 1) * gd]
            z_cols.append(_dot(ws_ref[gi], blk) + bst_ref[:, gi:gi + 1])
        z_rows.append(jnp.concatenate(z_cols, axis=1))
    ya = (u_ref[...] * jnp.concatenate(z_rows, axis=0)).astype(BF16)
    yb = _silu(_ln(conv_ref[...] + vec(VEC_CONV_B), vec(VEC_CONV_G), vec(VEC_CONV_LB))).astype(BF16)
    yc = yc_ref[...]
    yd = yd_ref[...]

    merged = []
    for n in range(D_MODEL // MXU_N):
        cols = slice(n * MXU_N, (n + 1) * MXU_N)
        acc = None
        for i, yi in enumerate((ya, yb, yc, yd)):
            lo = i * D_MODEL + n * MXU_N
            p = _dot(yi, wbr_ref[i, :, cols])
            term = gate_ref[:, lo:lo + MXU_N] * p + p
            acc = term if acc is None else acc + term
        merged.append((0.5 * acc).astype(BF16))
    merged = jnp.concatenate(merged, axis=1)
    y = jnp.concatenate([_dot(merged, wout_ref[:, n * MXU_N:(n + 1) * MXU_N]) for n in range(D_MODEL // MXU_N)],
                        axis=1)
    o_ref[...] = _ln(ALPHA * x_ref[...] + gate * y, g_ref[...], beta_ref[...])


def _mix_call(s, mods, hf, hb, w_mx, b_mx, vec512, gmlp_ws, gmlp_bst, conv_w, pool_w, w_branch, w_out,
              ln_g, ln_b, *, layer):
    hb_per_tile = TM // HALO
    n_hb = T_ALL // HALO
    return pl.pallas_call(
        _mix_kernel,
        grid=(NT,),
        in_specs=[
            pl.BlockSpec((TM, D_MODEL), lambda i: (i, 0)),
            pl.BlockSpec((HALO, D_MODEL), lambda i: (jnp.maximum(i * hb_per_tile - 1, 0), 0)),
            pl.BlockSpec((HALO, D_MODEL), lambda i: (jnp.minimum((i + 1) * hb_per_tile, n_hb - 1), 0)),
            pl.BlockSpec((None, 9, D_MODEL), lambda i: (layer * MOD_ROWS + _mod_row(i), 0, 0)),
            pl.BlockSpec((TM, MIX_W), lambda i: (i, 0)),
            pl.BlockSpec((TM, MIX_W), lambda i: (i, 0)),
            _const_spec((None, D_MODEL, MX_COLS + LANE_PAD), (layer, 0, 0)),
            _const_spec((None, 1, MX_COLS), (layer, 0, 0)),
            _const_spec((None, VEC_ROWS, MIX_W), (layer, 0, 0)),
            _const_spec((None, GMLP_GROUPS, CHUNK, CHUNK), (layer, 0, 0, 0)),
            _const_spec((None, CHUNK, GMLP_GROUPS), (layer, 0, 0)),
            _const_spec((None, CONV_W, MIX_W), (layer, 0, 0)),
            _const_spec((None, len(POOL_WINDOWS), POOL_GD, POOL_GD), (layer, 0, 0, 0)),
            _const_spec((None, N_BRANCH, MIX_W, D_MODEL + LANE_PAD), (layer, 0, 0, 0)),
            _const_spec((None, D_MODEL, D_MODEL + LANE_PAD), (layer, 0, 0)),
            _const_spec((None, 1, D_MODEL), (layer * 3 + 1, 0, 0)),
            _const_spec((None, 1, D_MODEL), (layer * 3 + 1, 0, 0)),
        ],
        out_specs=pl.BlockSpec((TM, D_MODEL), lambda i: (i, 0)),
        out_shape=jax.ShapeDtypeStruct((T_ALL, D_MODEL), F32),
        scratch_shapes=[
            pltpu.VMEM((TM + 2 * HALO, D_MODEL), BF16),
            pltpu.VMEM((TM + 2 * HALO, MIX_W), F32),
            pltpu.VMEM((TM + 2 * HALO, MIX_W), F32),
            pltpu.VMEM((SUBLANES - 1, TM + 2 * HALO, MIX_W), F32),
            pltpu.VMEM((TM, MIX_W), F32),
            pltpu.VMEM((TM, N_BRANCH * D_MODEL), F32),
            pltpu.VMEM((TM, 2 * MIX_W), F32),
            pltpu.VMEM((TM, MIX_W), F32),
            pltpu.VMEM((TM, MIX_W), F32),
            pltpu.VMEM((TM, MIX_W), BF16),
            pltpu.VMEM((TM, MIX_W), BF16),
            pltpu.VMEM((TM, MIX_W), BF16),
        ],
        compiler_params=_params(("parallel",)),
        name="mix",
    )(s, s, s, mods, hf, hb, w_mx, b_mx, vec512, gmlp_ws, gmlp_bst, conv_w, pool_w, w_branch, w_out, ln_g, ln_b)


def _repack_cols(w, pieces):
    parts = [jnp.zeros(w.shape[:-1] + (p,), w.dtype) if isinstance(p, int) else w[..., p[0]:p[1]] for p in pieces]
    return jnp.concatenate(parts, axis=-1)


def kernel(x, c, ctx, c_ctx, w_ada, b_ada, ln_g, ln_b, ffn_w_in, ffn_w_out, w_in, b_in, gmlp_ln_g, gmlp_ln_b,
           gmlp_ws, gmlp_bs, conv_w, conv_b, conv_ln_g, conv_ln_b, qk_conv_w, mlstm_ln_g, pool_w, pool_scale,
           w_branch, w_out):
    mix_pieces = [(OFF_A, OFF_C), (OFF_C_O, OFF_D), (OFF_D, OFF_G), (OFF_G, IN_COLS)]
    nh = MLSTM_HEADS
    gate_pad = LANES - 2 * nh

    def cproj_cols(w):
        g = w[..., OFF_C_GATES:OFF_C_O].reshape(w.shape[:-1] + (4, nh))
        pad = jnp.zeros(w.shape[:-1] + (gate_pad,), w.dtype)
        return jnp.concatenate([w[..., OFF_C:OFF_C_GATES], g[..., 0, :], g[..., 2, :], pad,
                                g[..., 1, :], g[..., 3, :], pad], axis=-1)
    w_mx = _repack_cols(w_in, mix_pieces + [LANE_PAD]).astype(BF16)
    b_mx = _repack_cols(b_in, mix_pieces).reshape(DEPTH, 1, MX_COLS)
    w_cp = cproj_cols(w_in).astype(BF16)
    b_cp = cproj_cols(b_in).reshape(DEPTH, 1, CP_COLS)
    ffn_w_in_b = ffn_w_in.astype(BF16)
    ffn_w_out_b = _repack_cols(ffn_w_out, [(0, D_MODEL), LANE_PAD]).astype(BF16)
    w_branch_b = _repack_cols(w_branch, [(0, D_MODEL), LANE_PAD]).astype(BF16)
    w_out_b = _repack_cols(w_out, [(0, D_MODEL), LANE_PAD]).astype(BF16)
    gmlp_ws_b = gmlp_ws.astype(BF16)
    pool_w_b = pool_w.astype(BF16)
    gmlp_bst = jnp.swapaxes(gmlp_bs, 1, 2)
    vec512 = jnp.stack([gmlp_ln_g, gmlp_ln_b, conv_b, conv_ln_g, conv_ln_b, mlstm_ln_g, pool_scale,
                        jnp.zeros_like(pool_scale)], axis=1)
    ln_g3 = ln_g.reshape(DEPTH * 3, 1, D_MODEL)
    ln_b3 = ln_b.reshape(DEPTH * 3, 1, D_MODEL)
    c_t = jnp.concatenate([c, c_ctx[None], jnp.zeros((MOD_ROWS - BATCH - 1, D_MODEL), F32)], axis=0).T

    mods = _mods_call(c_t, w_ada, b_ada).reshape(DEPTH * MOD_ROWS, 9, D_MODEL)
    s = _embed_call(x, ctx)
    for l in range(DEPTH):
        last = l == DEPTH - 1
        s = _ffn_call(s, mods, ffn_w_in_b, ffn_w_out_b, ln_g3, ln_b3, layer=l, which=0, final=False)
        q, kt, v, g, gt = _cproj_call(s, mods, w_cp, b_cp, qk_conv_w, layer=l)
        hf, hb = _mlstm_call(q, kt, v, g, gt)
        s = _mix_call(s, mods, hf, hb, w_mx, b_mx, vec512, gmlp_ws_b, gmlp_bst, conv_w, pool_w_b, w_branch_b,
                      w_out_b, ln_g3, ln_b3, layer=l)
        s = _ffn_call(s, mods, ffn_w_in_b, ffn_w_out_b, ln_g3, ln_b3, layer=l, which=1, final=last)
    return s.reshape(BATCH, SEQ, D_MODEL)
```

```python
import functools
import math

import jax
import jax.numpy as jnp
from jax import lax
from jax.experimental import pallas as pl
from jax.experimental.pallas import tpu as pltpu

D_MODEL = 1024
BATCH = 2
SEQ = 8192
DEPTH = 4
GRID_W = 64
CTX_LEN = 256
MIX_W = D_MODEL // 2
N_BRANCH = 4
CHUNK = 128
GMLP_GROUPS = 4
CONV_W = 31
MLSTM_HEADS = 4
MLSTM_DH = MIX_W // MLSTM_HEADS
QK_CONV = 3
POOL_WINDOWS = (2, 4, 8, 16)
POOL_GD = MIX_W // len(POOL_WINDOWS)
D_FF = 128 * ((8 * D_MODEL // 3 + 127) // 128)
ALPHA = (2 * DEPTH) ** 0.25
LN_EPS = 1e-6
FFN_RES = 0.5

OFF_A = 0
OFF_B = OFF_A + 2 * MIX_W
OFF_C = OFF_B + 2 * MIX_W
OFF_C_GATES = OFF_C + 3 * MIX_W
OFF_C_O = OFF_C_GATES + 4 * MLSTM_HEADS
OFF_D = OFF_C_O + MIX_W
OFF_G = OFF_D + MIX_W
IN_COLS = OFF_G + N_BRANCH * D_MODEL

LANES = 128
SUBLANES = 8
BF16_ROWS = 16
TM = 256
SEQ_B = CTX_LEN + SEQ
T_ALL = BATCH * SEQ_B
NT_B = SEQ_B // TM
NT = BATCH * NT_B
NCH_B = SEQ_B // CHUNK
CH_T = TM // CHUNK
CTX_CH = CTX_LEN // CHUNK
HALO = 16
MXU_N = 256
LANE_PAD = LANES
FF_CH = MXU_N
FFN_TILES = 2
CONV_RB = 64
CHAIN_LAG = 3
GS_RMAX = 2 * MLSTM_HEADS
MOD_ROWS = 8
CTX_MOD_ROW = BATCH
VMEM_LIMIT = 56 * 1024 * 1024

MX_A = 0
MX_B = MX_A + 2 * MIX_W
MX_O = MX_B + 2 * MIX_W
MX_D = MX_O + MIX_W
MX_G = MX_D + MIX_W
MX_COLS = MX_G + N_BRANCH * D_MODEL
CP_QK = 0
CP_V = 2 * MIX_W
CP_GI = 3 * MIX_W
CP_GF = CP_GI + LANES
CP_COLS = CP_GF + LANES

F32 = jnp.float32
BF16 = jnp.bfloat16


def _dot(a, b):
    return jnp.dot(a, b, preferred_element_type=F32)


def _dot_f32(a, b):
    return jnp.dot(a, b, preferred_element_type=F32, precision=lax.Precision.HIGHEST)


def _ln(r, g, b):
    mu = jnp.mean(r, axis=-1, keepdims=True)
    xc = r - mu
    var = jnp.mean(xc * xc, axis=-1, keepdims=True)
    return xc * lax.rsqrt(var + LN_EPS) * g + b


def _sigmoid(x):
    return 0.5 * jnp.tanh(0.5 * x) + 0.5


def _silu(x):
    hx = 0.5 * x
    return hx * jnp.tanh(hx) + hx


def _gelu_tanh(x):
    c = math.sqrt(2.0 / math.pi)
    hx = 0.5 * x
    return hx * jnp.tanh(x * (c + (c * 0.044715) * (x * x))) + hx


def _mod_row(i):
    return jnp.where(i % NT_B == 0, CTX_MOD_ROW, i // NT_B)


def _const_spec(block, index):
    return pl.BlockSpec(block, lambda *_: index, pipeline_mode=pl.Buffered(1))


def _params(sem, flags=None):
    return pltpu.CompilerParams(dimension_semantics=sem, vmem_limit_bytes=VMEM_LIMIT, flags=flags)


def _mods_kernel(ct_ref, w_ref, b_ref, o_ref):
    s = _silu(ct_ref[...])
    w = w_ref[...]
    o_ref[...] = jnp.zeros(o_ref.shape, F32)
    for r in range(BATCH + 1):
        o_ref[r:r + 1, :] = jnp.sum(s[:, r:r + 1] * w, axis=0, keepdims=True) + b_ref[...]


def _mods_call(c_t, w_ada, b_ada):
    tn = D_MODEL
    n_col = w_ada.shape[-1] // tn
    return pl.pallas_call(
        _mods_kernel,
        grid=(DEPTH, n_col),
        in_specs=[
            pl.BlockSpec((D_MODEL, MOD_ROWS), lambda l, n: (0, 0)),
            pl.BlockSpec((None, D_MODEL, tn), lambda l, n: (l, 0, n)),
            pl.BlockSpec((None, 1, tn), lambda l, n: (l, 0, n)),
        ],
        out_specs=pl.BlockSpec((None, MOD_ROWS, tn), lambda l, n: (l, 0, n)),
        out_shape=jax.ShapeDtypeStruct((DEPTH, MOD_ROWS, w_ada.shape[-1]), F32),
        compiler_params=_params(("parallel", "parallel")),
        name="mods",
    )(c_t, w_ada, b_ada.reshape(DEPTH, 1, -1))


def _embed_kernel(x_ref, ctx_ref, o_ref):
    j = pl.program_id(0) % NT_B

    @pl.when(j == 0)
    def _():
        o_ref[...] = ctx_ref[...]

    @pl.when(j > 0)
    def _():
        quarter = D_MODEL // 4
        grid_rows = TM // GRID_W
        k = lax.broadcasted_iota(jnp.int32, (1, quarter), 1).astype(F32)
        freqs = jnp.exp(-math.log(10000.0) * k / quarter)
        r = ((j - 1) * grid_rows + lax.broadcasted_iota(jnp.int32, (SUBLANES, 1), 0)).astype(F32)
        col = lax.broadcasted_iota(jnp.int32, (GRID_W, 1), 0).astype(F32)
        er = r * freqs
        ec = col * freqs
        sin_r, cos_r, sin_c, cos_c = jnp.sin(er), jnp.cos(er), jnp.sin(ec), jnp.cos(ec)
        for q in range(grid_rows):
            rows = slice(q * GRID_W, (q + 1) * GRID_W)
            o_ref[rows, 0 * quarter:1 * quarter] = x_ref[rows, 0 * quarter:1 * quarter] + sin_r[q:q + 1, :]
            o_ref[rows, 1 * quarter:2 * quarter] = x_ref[rows, 1 * quarter:2 * quarter] + cos_r[q:q + 1, :]
            o_ref[rows, 2 * quarter:3 * quarter] = x_ref[rows, 2 * quarter:3 * quarter] + sin_c
            o_ref[rows, 3 * quarter:4 * quarter] = x_ref[rows, 3 * quarter:4 * quarter] + cos_c


def _embed_call(x, ctx):
    lat_tiles_b = SEQ // TM
    return pl.pallas_call(
        _embed_kernel,
        grid=(NT,),
        in_specs=[
            pl.BlockSpec((TM, D_MODEL), lambda i: ((i // NT_B) * lat_tiles_b + jnp.maximum(i % NT_B - 1, 0), 0)),
            pl.BlockSpec((TM, D_MODEL), lambda i: (i // NT_B, 0)),
        ],
        out_specs=pl.BlockSpec((TM, D_MODEL), lambda i: (i, 0)),
        out_shape=jax.ShapeDtypeStruct((T_ALL, D_MODEL), F32),
        compiler_params=_params(("parallel",)),
        name="embed",
    )(x.reshape(BATCH * SEQ, D_MODEL), ctx.reshape(BATCH * CTX_LEN, D_MODEL))


def _ffn_kernel(*refs, sub):
    x_refs = refs[:FFN_TILES]
    mod_refs = refs[FFN_TILES:2 * FFN_TILES]
    win_ref, wout_ref, g_ref, b_ref, o_ref, xm_ref, h_ref = refs[2 * FFN_TILES:]
    for t, (x_ref, mod_ref) in enumerate(zip(x_refs, mod_refs)):
        shift = mod_ref[3 * sub + 0:3 * sub + 1, :]
        scale = mod_ref[3 * sub + 1:3 * sub + 2, :]
        xm_ref[t * TM:(t + 1) * TM, :] = (x_ref[...] * (1.0 + scale) + shift).astype(BF16)
    xm = xm_ref[...]
    for c in range(D_FF // FF_CH):
        a1 = _dot(xm, win_ref[:, c * FF_CH:(c + 1) * FF_CH])
        a2 = _dot(xm, win_ref[:, D_FF + c * FF_CH:D_FF + (c + 1) * FF_CH])
        h_ref[:, c * FF_CH:(c + 1) * FF_CH] = (_silu(a1) * a2).astype(BF16)
    for t, (x_ref, mod_ref) in enumerate(zip(x_refs, mod_refs)):
        hid = h_ref[t * TM:(t + 1) * TM, :]
        y = jnp.concatenate([_dot(hid, wout_ref[:, n * MXU_N:(n + 1) * MXU_N]) for n in range(D_MODEL // MXU_N)],
                            axis=1)
        gate = mod_ref[3 * sub + 2:3 * sub + 3, :]
        r = ALPHA * x_ref[...] + (FFN_RES * gate) * y
        o_ref[t * TM:(t + 1) * TM, :] = _ln(r, g_ref[...], b_ref[...])


def _ffn_call(s, mods, ffn_w_in, ffn_w_out, ln_g, ln_b, *, layer, which, final):
    sub = 2 * which
    if final:
        lat_tiles_b = SEQ // TM
        out_tiles = BATCH * lat_tiles_b
        in_tile = lambda t: (t // lat_tiles_b) * NT_B + 1 + t % lat_tiles_b
    else:
        out_tiles = NT
        in_tile = lambda t: t
    tile_of = lambda i, t: in_tile(i * FFN_TILES + t)
    x_specs = [pl.BlockSpec((TM, D_MODEL), functools.partial(lambda i, t: (tile_of(i, t), 0), t=t))
               for t in range(FFN_TILES)]
    mod_specs = [pl.BlockSpec((None, 9, D_MODEL),
                              functools.partial(lambda i, t: (layer * MOD_ROWS + _mod_row(tile_of(i, t)), 0, 0), t=t))
                 for t in range(FFN_TILES)]
    return pl.pallas_call(
        functools.partial(_ffn_kernel, sub=sub),
        grid=(out_tiles // FFN_TILES,),
        in_specs=x_specs + mod_specs + [
            _const_spec((None, None, D_MODEL, 2 * D_FF), (layer, which, 0, 0)),
            _const_spec((None, None, D_FF, D_MODEL + LANE_PAD), (layer, which, 0, 0)),
            _const_spec((None, 1, D_MODEL), (layer * 3 + sub, 0, 0)),
            _const_spec((None, 1, D_MODEL), (layer * 3 + sub, 0, 0)),
        ],
        out_specs=pl.BlockSpec((FFN_TILES * TM, D_MODEL), lambda i: (i, 0)),
        out_shape=jax.ShapeDtypeStruct((out_tiles * TM, D_MODEL), F32),
        scratch_shapes=[
            pltpu.VMEM((FFN_TILES * TM, D_MODEL), BF16),
            pltpu.VMEM((FFN_TILES * TM, D_FF), BF16),
        ],
        compiler_params=_params(("parallel",)),
        name=f"ffn{which}",
    )(*([s] * FFN_TILES), *([mods] * FFN_TILES), ffn_w_in, ffn_w_out, ln_g, ln_b)


def _cproj_kernel(x_ref, prev_ref, next_ref, mod_ref, w_ref, b_ref, cw_ref,
                  q_ref, kt_ref, v_ref, g_ref, gt_ref, xe_ref, p_ref):
    j = pl.program_id(0) % NT_B
    first = jnp.logical_or(j == 0, j == 1)
    last = jnp.logical_or(j == 0, j == NT_B - 1)
    shift = mod_ref[3:4, :]
    scale = mod_ref[4:5, :]

    def modulate(v):
        return (v * (1.0 + scale) + shift).astype(BF16)

    xe_ref[0:HALO, :] = modulate(prev_ref[...])
    xe_ref[HALO:HALO + TM, :] = modulate(x_ref[...])
    xe_ref[HALO + TM:2 * HALO + TM, :] = modulate(next_ref[...])
    xe = xe_ref[...]
    rows = lax.broadcasted_iota(jnp.int32, (TM + 2 * HALO, 1), 0)
    valid = jnp.logical_and(jnp.logical_or(rows >= HALO, jnp.logical_not(first)),
                            jnp.logical_or(rows < HALO + TM, jnp.logical_not(last)))
    p_ref[...] = jnp.where(valid, _dot(xe, w_ref[:, CP_QK:CP_V]) + b_ref[:, CP_QK:CP_V], 0.0)
    conv = (cw_ref[0:1, :] * p_ref[HALO - 1:HALO - 1 + TM, :]
            + cw_ref[1:2, :] * p_ref[HALO:HALO + TM, :]
            + cw_ref[2:3, :] * p_ref[HALO + 1:HALO + 1 + TM, :])
    qk = _silu(conv)
    q_ref[...] = qk[:, :MIX_W].astype(BF16)
    k = qk[:, MIX_W:] * MLSTM_DH ** -0.5
    for c in range(CH_T):
        for h in range(MLSTM_HEADS):
            blk = k[c * CHUNK:(c + 1) * CHUNK, h * MLSTM_DH:(h + 1) * MLSTM_DH]
            r0 = (c * MLSTM_HEADS + h) * MLSTM_DH
            kt_ref[r0:r0 + MLSTM_DH, :] = blk.T.astype(BF16)

    xm = xe_ref[HALO:HALO + TM, :]
    pvg = _dot(xm, w_ref[:, CP_V:CP_COLS]) + b_ref[:, CP_V:CP_COLS]
    v_ref[...] = pvg[:, :MIX_W].astype(BF16)
    li = pvg[:, CP_GI - CP_V:CP_GF - CP_V]
    fraw = pvg[:, CP_GF - CP_V:]
    lf = jnp.minimum(fraw, 0.0) - jnp.log1p(jnp.exp(-jnp.abs(fraw)))
    row = lax.broadcasted_iota(jnp.int32, (CHUNK, LANES), 0)
    lane = lax.broadcasted_iota(jnp.int32, (CHUNK, LANES), 1)
    fwd_lane = lane < MLSTM_HEADS
    lower_f = (lane <= row).astype(F32)
    for c in range(CH_T):
        tok = slice(c * CHUNK, (c + 1) * CHUNK)
        prefix = _dot_f32(lower_f, lf[tok])
        suffix = prefix[CHUNK - 1:CHUNK, :] - prefix + lf[tok]
        cum = jnp.where(fwd_lane, prefix, suffix)
        r = li[tok] - cum
        rf, rb = r, r
        k = 1
        while k < CHUNK:
            rf = jnp.maximum(rf, jnp.where(row >= k, pltpu.roll(rf, k, 0), -jnp.inf))
            rb = jnp.maximum(rb, jnp.where(row < CHUNK - k, pltpu.roll(rb, CHUNK - k, 0), -jnp.inf))
            k *= 2
        rmax = jnp.where(fwd_lane, rf, rb)
        g_ref[tok, :] = jnp.where(lane < GS_RMAX, cum, pltpu.roll(rmax, GS_RMAX, 1))
        gt_ref[c * SUBLANES:(c + 1) * SUBLANES, :] = r.T[:SUBLANES, :]


def _cproj_call(s, mods, w_cp, b_cp, qk_conv_w, *, layer):
    hb = TM // HALO
    n_hb = T_ALL // HALO
    return pl.pallas_call(
        _cproj_kernel,
        grid=(NT,),
        in_specs=[
            pl.BlockSpec((TM, D_MODEL), lambda i: (i, 0)),
            pl.BlockSpec((HALO, D_MODEL), lambda i: (jnp.maximum(i * hb - 1, 0), 0)),
            pl.BlockSpec((HALO, D_MODEL), lambda i: (jnp.minimum((i + 1) * hb, n_hb - 1), 0)),
            pl.BlockSpec((None, 9, D_MODEL), lambda i: (layer * MOD_ROWS + _mod_row(i), 0, 0)),
            _const_spec((None, D_MODEL, CP_COLS), (layer, 0, 0)),
            _const_spec((None, 1, CP_COLS), (layer, 0, 0)),
            _const_spec((None, QK_CONV, 2 * MIX_W), (layer, 0, 0)),
        ],
        out_specs=[
            pl.BlockSpec((TM, MIX_W), lambda i: (i, 0)),
            pl.BlockSpec((TM * MLSTM_HEADS, MLSTM_DH), lambda i: (i, 0)),
            pl.BlockSpec((TM, MIX_W), lambda i: (i, 0)),
            pl.BlockSpec((TM, LANES), lambda i: (i, 0)),
            pl.BlockSpec((CH_T * SUBLANES, CHUNK), lambda i: (i, 0)),
        ],
        out_shape=[
            jax.ShapeDtypeStruct((T_ALL, MIX_W), BF16),
            jax.ShapeDtypeStruct((T_ALL * MLSTM_HEADS, MLSTM_DH), BF16),
            jax.ShapeDtypeStruct((T_ALL, MIX_W), BF16),
            jax.ShapeDtypeStruct((T_ALL, LANES), F32),
            jax.ShapeDtypeStruct((T_ALL // CHUNK * SUBLANES, CHUNK), F32),
        ],
        scratch_shapes=[
            pltpu.VMEM((TM + 2 * HALO, D_MODEL), BF16),
            pltpu.VMEM((TM + 2 * HALO, 2 * MIX_W), F32),
        ],
        compiler_params=_params(("parallel",)),
        name="cproj",
    )(s, s, s, mods, w_cp, b_cp, qk_conv_w)


def _mlstm_kernel(qf_ref, ktf_ref, vf_ref, gf_ref, gtf_ref, qb_ref, ktb_ref, vb_ref, gb_ref, gtb_ref,
                  hf_ref, hb_ref, c_ref, m_ref):
    @pl.when(pl.program_id(0) == 0)
    def _():
        c_ref[...] = jnp.zeros(c_ref.shape, F32)
        m_ref[...] = jnp.zeros(m_ref.shape, F32)

    row = lax.broadcasted_iota(jnp.int32, (CHUNK, CHUNK), 0)
    col = lax.broadcasted_iota(jnp.int32, (CHUNK, CHUNK), 1)
    ones_col = jnp.ones((CHUNK, MLSTM_DH), BF16)
    nh = MLSTM_HEADS
    dirs = ((qf_ref, ktf_ref, vf_ref, gf_ref, gtf_ref, hf_ref, col <= row, CHUNK - 1),
            (qb_ref, ktb_ref, vb_ref, gb_ref, gtb_ref, hb_ref, col >= row, 0))
    heads = []
    for d, (q_ref, kt_ref, v_ref, g_ref, gt_ref, h_ref, seen, end_row) in enumerate(dirs):
        for b in range(BATCH):
            g = g_ref[b]
            gt = gt_ref[b]
            for h in range(nh):
                lane = d * nh + h
                hd = slice(h * MLSTM_DH, (h + 1) * MLSTM_DH)
                heads.append(dict(
                    idx=(d * BATCH + b) * nh + h, seen=seen, end_row=end_row, out=(h_ref, b, hd),
                    cum=g[:, lane:lane + 1], rmax=g[:, GS_RMAX + lane:GS_RMAX + lane + 1], r=gt[lane:lane + 1, :],
                    q=q_ref[b, :, hd], kt=kt_ref[b, hd, :], v=v_ref[b, :, hd]))

    for hd in heads:
        hd["qk"] = _dot(hd["q"], hd["kt"])
        hd["m_prev"] = m_ref[hd["idx"], 0:1, 0:1]
        hd["c_old"] = c_ref[hd["idx"]]
        hd["v_aug"] = jnp.concatenate([hd["v"], ones_col], axis=1)
    for hd in heads:
        e = hd["end_row"]
        top = jnp.maximum(hd["m_prev"], hd["rmax"][e:e + 1, :])
        w_row = jnp.exp(hd["r"] - top)
        ktw = (hd["kt"].astype(F32) * w_row).astype(BF16)
        c_ref[hd["idx"]] = jnp.exp(hd["m_prev"] - top) * hd["c_old"] + _dot(ktw, hd["v_aug"])
        m_ref[hd["idx"]] = jnp.broadcast_to(hd["cum"][e:e + 1, :] + top, m_ref.shape[1:])
    for hd in heads:
        m_prev = hd["m_prev"]
        top = jnp.broadcast_to(jnp.maximum(m_prev, hd["rmax"]), (CHUNK, CHUNK))
        s = hd["qk"] * jnp.exp(jnp.where(hd["seen"], hd["r"] - top, -jnp.inf))
        qw = hd["q"].astype(F32) * jnp.exp(m_prev - top)
        lhs = jnp.concatenate([s.astype(BF16), qw.astype(BF16)], axis=1)
        rhs = jnp.concatenate([hd["v_aug"], hd["c_old"].astype(BF16)], axis=0)
        hd["na"] = _dot(lhs, rhs)
        hd["floor"] = jnp.exp(-(hd["cum"] + top))
    for hd in heads:
        den = jnp.maximum(jnp.abs(hd["na"][:, MLSTM_DH:]), hd["floor"])
        h_ref, b, cols = hd["out"]
        h_ref[b, :, cols] = hd["na"][:, :MLSTM_DH] / den


def _mlstm_call(q, kt, v, g, gt):
    fwd = lambda s: s
    bwd = lambda s: jnp.where(s < CTX_CH, CTX_CH - 1 - s, NCH_B + CTX_CH - 1 - s)
    q3 = q.reshape(BATCH, SEQ_B, MIX_W)
    kt3 = kt.reshape(BATCH, SEQ_B * MLSTM_HEADS, MLSTM_DH)
    v3 = v.reshape(BATCH, SEQ_B, MIX_W)
    g3 = g.reshape(BATCH, SEQ_B, LANES)
    gt3 = gt.reshape(BATCH, NCH_B * SUBLANES, CHUNK)

    def specs(order):
        return [
            pl.BlockSpec((BATCH, CHUNK, MIX_W), lambda s: (0, order(s), 0)),
            pl.BlockSpec((BATCH, CHUNK * MLSTM_HEADS, MLSTM_DH), lambda s: (0, order(s), 0)),
            pl.BlockSpec((BATCH, CHUNK, MIX_W), lambda s: (0, order(s), 0)),
            pl.BlockSpec((BATCH, CHUNK, LANES), lambda s: (0, order(s), 0)),
            pl.BlockSpec((BATCH, SUBLANES, CHUNK), lambda s: (0, order(s), 0)),
        ]

    n_state = 2 * BATCH * MLSTM_HEADS
    hf, hb = pl.pallas_call(
        _mlstm_kernel,
        grid=(NCH_B,),
        in_specs=specs(fwd) + specs(bwd),
        out_specs=[
            pl.BlockSpec((BATCH, CHUNK, MIX_W), lambda s: (0, fwd(s), 0)),
            pl.BlockSpec((BATCH, CHUNK, MIX_W), lambda s: (0, bwd(s), 0)),
        ],
        out_shape=[jax.ShapeDtypeStruct((BATCH, SEQ_B, MIX_W), F32)] * 2,
        scratch_shapes=[
            pltpu.VMEM((n_state, MLSTM_DH, 2 * MLSTM_DH), F32),
            pltpu.VMEM((n_state, SUBLANES, LANES), F32),
        ],
        compiler_params=_params(("arbitrary",)),
        name="mlstm",
    )(q3, kt3, v3, g3, gt3, q3, kt3, v3, g3, gt3)
    return hf.reshape(T_ALL, MIX_W), hb.reshape(T_ALL, MIX_W)


VEC_GMLP_G, VEC_GMLP_B, VEC_CONV_B, VEC_CONV_G, VEC_CONV_LB, VEC_MLSTM_G, VEC_POOL_S = range(7)
VEC_ROWS = 8


def _mix_kernel(x_ref, prev_ref, next_ref, mod_ref, hf_ref, hb_ref, w_ref, b_ref, vec_ref, ws_ref, bst_ref,
                cw_ref, pw_ref, wbr_ref, wout_ref, g_ref, beta_ref, o_ref, xe_ref, a_ref, d_ref, ash_ref, conv_ref,
                gate_ref, pa_ref, po_ref, u_ref, vn_ref, yc_ref, yd_ref):
    j = pl.program_id(0) % NT_B
    is_ctx = j == 0
    first = jnp.logical_or(is_ctx, j == 1)
    last = jnp.logical_or(is_ctx, j == NT_B - 1)
    shift = mod_ref[3:4, :]
    scale = mod_ref[4:5, :]
    gate = mod_ref[5:6, :]

    def modulate(v):
        return (v * (1.0 + scale) + shift).astype(BF16)

    def vec(r):
        return vec_ref[r:r + 1, :]

    def proj(ext, lo, width, zero=None):
        lhs = xe_ref[...] if ext else xe_ref[HALO:HALO + TM, :]
        bias = lambda c: b_ref[:, c:c + MXU_N] if zero is None else b_ref[:, c:c + MXU_N] + zero
        return jnp.concatenate([_dot(lhs, w_ref[:, c:c + MXU_N]) + bias(c)
                                for c in range(lo, lo + width, MXU_N)], axis=1)

    xe_ref[0:HALO, :] = modulate(prev_ref[...])
    xe_ref[HALO:HALO + TM, :] = modulate(x_ref[...])
    xe_ref[HALO + TM:2 * HALO + TM, :] = modulate(next_ref[...])
    rows = lax.broadcasted_iota(jnp.int32, (TM + 2 * HALO, 1), 0)
    valid = jnp.logical_and(jnp.logical_or(rows >= HALO, jnp.logical_not(first)),
                            jnp.logical_or(rows < HALO + TM, jnp.logical_not(last)))

    pb = proj(True, MX_B, 2 * MIX_W)
    a_ref[...] = jnp.where(valid, pb[:, :MIX_W] * _sigmoid(pb[:, MIX_W:]), 0.0)
    pa_ref[...] = proj(False, MX_A, 2 * MIX_W)

    gate_chunks = [(i, n) for i in range(N_BRANCH) for n in range(D_MODEL // MXU_N)]
    sh_rows = TM + 2 * HALO - SUBLANES
    for s in range(1, SUBLANES):
        ash_ref[s - 1, 0:sh_rows, :] = a_ref[s:s + sh_rows, :]

    def zero_of(v):
        bits = lax.bitcast_convert_type(v[0:1, 0:LANES], jnp.uint32)
        return lax.shift_right_logical(lax.shift_right_logical(bits, jnp.uint32(16)), jnp.uint32(16)).astype(F32)

    def conv_block(cb, rb, z):
        cols = slice(cb * LANES, (cb + 1) * LANES)
        part = None
        for k in range(CONV_W):
            q8, s = divmod(HALO - CONV_W // 2 + k, SUBLANES)
            r0 = q8 * SUBLANES + rb * CONV_RB
            src = a_ref[r0:r0 + CONV_RB, cols] if s == 0 else ash_ref[s - 1, r0:r0 + CONV_RB, cols]
            term = (cw_ref[k:k + 1, cols] + z) * src
            part = term if part is None else part + term
        conv_ref[rb * CONV_RB:(rb + 1) * CONV_RB, cols] = part
        return part

    conv_blocks = [(cb, rb) for cb in range(MIX_W // LANES) for rb in range(TM // CONV_RB)]
    per_step = len(conv_blocks) // len(gate_chunks)
    zero_row = jnp.zeros((1, LANES), F32)
    v_zero = [zero_row] * len(gate_chunks)
    for step, (i, n) in enumerate(gate_chunks):
        lo = i * D_MODEL + n * MXU_N
        z_in = jnp.concatenate([v_zero[step - CHAIN_LAG]] * (MXU_N // LANES), axis=1) if step >= CHAIN_LAG else None
        gv = jnp.tanh(0.5 * proj(False, MX_G + lo, MXU_N, z_in))
        gate_ref[:, lo:lo + MXU_N] = gv
        z = zero_of(gv) + v_zero[step - 1]
        for cb, rb in conv_blocks[step * per_step:(step + 1) * per_step]:
            z = zero_of(conv_block(cb, rb, z))
        v_zero[step] = z

    pa = _gelu_tanh(pa_ref[...])
    u_ref[...] = pa[:, :MIX_W]
    vn_ref[...] = _ln(pa[:, MIX_W:], vec(VEC_GMLP_G), vec(VEC_GMLP_B)).astype(BF16)

    d_ref[...] = jnp.where(valid, proj(True, MX_D, MIX_W), 0.0)
    pos = jnp.where(is_ctx, 0, (j - 1) * TM) + lax.broadcasted_iota(jnp.int32, (TM, 1), 0)
    n_seq = jnp.where(is_ctx, CTX_LEN, SEQ)
    for gi, win in enumerate(POOL_WINDOWS):
        lo, hi = win // 2, win - 1 - win // 2
        cols = slice(gi * POOL_GD, (gi + 1) * POOL_GD)
        wsum = d_ref[HALO - lo:HALO - lo + TM, cols]
        for k in range(-lo + 1, hi + 1):
            wsum = wsum + d_ref[HALO + k:HALO + k + TM, cols]
        cnt = (jnp.minimum(pos + hi + 1, n_seq) - jnp.maximum(pos - lo, 0)).astype(F32)
        diff = wsum / cnt - d_ref[HALO:HALO + TM, cols]
        yd_ref[:, cols] = (_dot(diff.astype(BF16), pw_ref[gi]) * vec_ref[VEC_POOL_S:VEC_POOL_S + 1, cols]
                           ).astype(BF16)

    po_ref[...] = proj(False, MX_O, MIX_W)
    for h in range(MLSTM_HEADS):
        cols = slice(h * MLSTM_DH, (h + 1) * MLSTM_DH)
        hh = hf_ref[:, cols] + hb_ref[:, cols]
        mu = jnp.mean(hh, axis=-1, keepdims=True)
        hc = hh - mu
        hn = hc * lax.rsqrt(jnp.mean(hc * hc, axis=-1, keepdims=True) + LN_EPS)
        yc_ref[:, cols] = (_sigmoid(po_ref[:, cols]) * (hn * vec_ref[VEC_MLSTM_G:VEC_MLSTM_G + 1, cols])
                           ).astype(BF16)

    gd = MIX_W // GMLP_GROUPS
    z_rows = []
    for c in range(CH_T):
        z_cols = []
        for gi in range(GMLP_GROUPS):
            blk = vn_ref[c * CHUNK:(c + 1) * CHUNK, gi * gd:(gi + 1) * gd]
            z_cols.append(_dot(ws_ref[gi], blk) + bst_ref[:, gi:gi + 1])
        z_rows.append(jnp.concatenate(z_cols, axis=1))
    ya = (u_ref[...] * jnp.concatenate(z_rows, axis=0)).astype(BF16)
    yb = _silu(_ln(conv_ref[...] + vec(VEC_CONV_B), vec(VEC_CONV_G), vec(VEC_CONV_LB))).astype(BF16)
    yc = yc_ref[...]
    yd = yd_ref[...]

    merged = []
    for n in range(D_MODEL // MXU_N):
        cols = slice(n * MXU_N, (n + 1) * MXU_N)
        acc = None
        for i, yi in enumerate((ya, yb, yc, yd)):
            lo = i * D_MODEL + n * MXU_N
            p = _dot(yi, wbr_ref[i, :, cols])
            term = gate_ref[:, lo:lo + MXU_N] * p + p
            acc = term if acc is None else acc + term
        merged.append((0.5 * acc).astype(BF16))
    merged = jnp.concatenate(merged, axis=1)
    y = jnp.concatenate([_dot(merged, wout_ref[:, n * MXU_N:(n + 1) * MXU_N]) for n in range(D_MODEL // MXU_N)],
                        axis=1)
    o_ref[...] = _ln(ALPHA * x_ref[...] + gate * y, g_ref[...], beta_ref[...])


def _mix_call(s, mods, hf, hb, w_mx, b_mx, vec512, gmlp_ws, gmlp_bst, conv_w, pool_w, w_branch, w_out,
              ln_g, ln_b, *, layer):
    hb_per_tile = TM // HALO
    n_hb = T_ALL // HALO
    return pl.pallas_call(
        _mix_kernel,
        grid=(NT,),
        in_specs=[
            pl.BlockSpec((TM, D_MODEL), lambda i: (i, 0)),
            pl.BlockSpec((HALO, D_MODEL), lambda i: (jnp.maximum(i * hb_per_tile - 1, 0), 0)),
            pl.BlockSpec((HALO, D_MODEL), lambda i: (jnp.minimum((i + 1) * hb_per_tile, n_hb - 1), 0)),
            pl.BlockSpec((None, 9, D_MODEL), lambda i: (layer * MOD_ROWS + _mod_row(i), 0, 0)),
            pl.BlockSpec((TM, MIX_W), lambda i: (i, 0)),
            pl.BlockSpec((TM, MIX_W), lambda i: (i, 0)),
            _const_spec((None, D_MODEL, MX_COLS + LANE_PAD), (layer, 0, 0)),
            _const_spec((None, 1, MX_COLS), (layer, 0, 0)),
            _const_spec((None, VEC_ROWS, MIX_W), (layer, 0, 0)),
            _const_spec((None, GMLP_GROUPS, CHUNK, CHUNK), (layer, 0, 0, 0)),
            _const_spec((None, CHUNK, GMLP_GROUPS), (layer, 0, 0)),
            _const_spec((None, CONV_W, MIX_W), (layer, 0, 0)),
            _const_spec((None, len(POOL_WINDOWS), POOL_GD, POOL_GD), (layer, 0, 0, 0)),
            _const_spec((None, N_BRANCH, MIX_W, D_MODEL + LANE_PAD), (layer, 0, 0, 0)),
            _const_spec((None, D_MODEL, D_MODEL + LANE_PAD), (layer, 0, 0)),
            _const_spec((None, 1, D_MODEL), (layer * 3 + 1, 0, 0)),
            _const_spec((None, 1, D_MODEL), (layer * 3 + 1, 0, 0)),
        ],
        out_specs=pl.BlockSpec((TM, D_MODEL), lambda i: (i, 0)),
        out_shape=jax.ShapeDtypeStruct((T_ALL, D_MODEL), F32),
        scratch_shapes=[
            pltpu.VMEM((TM + 2 * HALO, D_MODEL), BF16),
            pltpu.VMEM((TM + 2 * HALO, MIX_W), F32),
            pltpu.VMEM((TM + 2 * HALO, MIX_W), F32),
            pltpu.VMEM((SUBLANES - 1, TM + 2 * HALO, MIX_W), F32),
            pltpu.VMEM((TM, MIX_W), F32),
            pltpu.VMEM((TM, N_BRANCH * D_MODEL), F32),
            pltpu.VMEM((TM, 2 * MIX_W), F32),
            pltpu.VMEM((TM, MIX_W), F32),
            pltpu.VMEM((TM, MIX_W), F32),
            pltpu.VMEM((TM, MIX_W), BF16),
            pltpu.VMEM((TM, MIX_W), BF16),
            pltpu.VMEM((TM, MIX_W), BF16),
        ],
        compiler_params=_params(("parallel",)),
        name="mix",
    )(s, s, s, mods, hf, hb, w_mx, b_mx, vec512, gmlp_ws, gmlp_bst, conv_w, pool_w, w_branch, w_out, ln_g, ln_b)


PREP_ROWS = 128


def _prep_win_kernel(w_ref, mx_ref, cp_ref):
    def cast(lo, hi):
        return w_ref[:, lo:hi].astype(BF16)

    mx_ref[:, MX_A:MX_O] = cast(OFF_A, OFF_C)
    mx_ref[:, MX_O:MX_D] = cast(OFF_C_O, OFF_D)
    mx_ref[:, MX_D:MX_G] = cast(OFF_D, OFF_G)
    mx_ref[:, MX_G:MX_COLS] = cast(OFF_G, IN_COLS)
    mx_ref[:, MX_COLS:MX_COLS + LANE_PAD] = jnp.zeros((PREP_ROWS, LANE_PAD), BF16)
    cp_ref[:, CP_QK:CP_GI] = cast(OFF_C, OFF_C_GATES)
    nh = MLSTM_HEADS
    t = w_ref[:, OFF_C_GATES:OFF_C_GATES + LANES]
    lane = lax.broadcasted_iota(jnp.int32, t.shape, 1)
    up1 = pltpu.roll(t, LANES - nh, 1)
    up2 = pltpu.roll(t, LANES - 2 * nh, 1)
    gi = jnp.where(lane < nh, t, jnp.where(lane < 2 * nh, up1, 0.0))
    gf = jnp.where(lane < nh, up1, jnp.where(lane < 2 * nh, up2, 0.0))
    cp_ref[:, CP_GI:CP_GF] = gi.astype(BF16)
    cp_ref[:, CP_GF:CP_COLS] = gf.astype(BF16)


def _prep_win_call(w_in):
    rows = DEPTH * D_MODEL
    w_mx, w_cp = pl.pallas_call(
        _prep_win_kernel,
        grid=(rows // PREP_ROWS,),
        in_specs=[pl.BlockSpec((PREP_ROWS, IN_COLS), lambda i: (i, 0))],
        out_specs=[pl.BlockSpec((PREP_ROWS, MX_COLS + LANE_PAD), lambda i: (i, 0)),
                   pl.BlockSpec((PREP_ROWS, CP_COLS), lambda i: (i, 0))],
        out_shape=[jax.ShapeDtypeStruct((rows, MX_COLS + LANE_PAD), BF16),
                   jax.ShapeDtypeStruct((rows, CP_COLS), BF16)],
        compiler_params=_params(("parallel",)),
        name="prep_win",
    )(w_in.reshape(rows, IN_COLS))
    return w_mx.reshape(DEPTH, D_MODEL, MX_COLS + LANE_PAD), w_cp.reshape(DEPTH, D_MODEL, CP_COLS)


def _pad_cast_kernel(w_ref, o_ref):
    cols = w_ref.shape[1]
    o_ref[:, 0:cols] = w_ref[...].astype(BF16)
    o_ref[:, cols:] = jnp.zeros((o_ref.shape[0], o_ref.shape[1] - cols), BF16)


def _pad_cast_call(w):
    lead, cols = w.shape[:-1], w.shape[-1]
    rows = math.prod(lead)
    rb = 512
    out = pl.pallas_call(
        _pad_cast_kernel,
        grid=(rows // rb,),
        in_specs=[pl.BlockSpec((rb, cols), lambda i: (i, 0))],
        out_specs=pl.BlockSpec((rb, cols + LANE_PAD), lambda i: (i, 0)),
        out_shape=jax.ShapeDtypeStruct((rows, cols + LANE_PAD), BF16),
        compiler_params=_params(("parallel",)),
        name="pad_cast",
    )(w.reshape(rows, cols))
    return out.reshape(lead + (cols + LANE_PAD,))


def _repack_cols(w, pieces):
    parts = [jnp.zeros(w.shape[:-1] + (p,), w.dtype) if isinstance(p, int) else w[..., p[0]:p[1]] for p in pieces]
    return jnp.concatenate(parts, axis=-1)


def kernel(x, c, ctx, c_ctx, w_ada, b_ada, ln_g, ln_b, ffn_w_in, ffn_w_out, w_in, b_in, gmlp_ln_g, gmlp_ln_b,
           gmlp_ws, gmlp_bs, conv_w, conv_b, conv_ln_g, conv_ln_b, qk_conv_w, mlstm_ln_g, pool_w, pool_scale,
           w_branch, w_out):
    mix_pieces = [(OFF_A, OFF_C), (OFF_C_O, OFF_D), (OFF_D, OFF_G), (OFF_G, IN_COLS)]
    nh = MLSTM_HEADS
    gate_pad = LANES - 2 * nh

    def cproj_cols(w):
        g = w[..., OFF_C_GATES:OFF_C_O].reshape(w.shape[:-1] + (4, nh))
        pad = jnp.zeros(w.shape[:-1] + (gate_pad,), w.dtype)
        return jnp.concatenate([w[..., OFF_C:OFF_C_GATES], g[..., 0, :], g[..., 2, :], pad,
                                g[..., 1, :], g[..., 3, :], pad], axis=-1)
    w_mx, w_cp = _prep_win_call(w_in)
    b_mx = _repack_cols(b_in, mix_pieces).reshape(DEPTH, 1, MX_COLS)
    b_cp = cproj_cols(b_in).reshape(DEPTH, 1, CP_COLS)
    ffn_w_in_b = ffn_w_in.astype(BF16)
    ffn_w_out_b = _pad_cast_call(ffn_w_out)
    w_branch_b = _pad_cast_call(w_branch)
    w_out_b = _pad_cast_call(w_out)
    gmlp_ws_b = gmlp_ws.astype(BF16)
    pool_w_b = pool_w.astype(BF16)
    gmlp_bst = jnp.swapaxes(gmlp_bs, 1, 2)
    vec512 = jnp.stack([gmlp_ln_g, gmlp_ln_b, conv_b, conv_ln_g, conv_ln_b, mlstm_ln_g, pool_scale,
                        jnp.zeros_like(pool_scale)], axis=1)
    ln_g3 = ln_g.reshape(DEPTH * 3, 1, D_MODEL)
    ln_b3 = ln_b.reshape(DEPTH * 3, 1, D_MODEL)
    c_t = jnp.concatenate([c, c_ctx[None], jnp.zeros((MOD_ROWS - BATCH - 1, D_MODEL), F32)], axis=0).T

    mods = _mods_call(c_t, w_ada, b_ada).reshape(DEPTH * MOD_ROWS, 9, D_MODEL)
    s = _embed_call(x, ctx)
    for l in range(DEPTH):
        last = l == DEPTH - 1
        s = _ffn_call(s, mods, ffn_w_in_b, ffn_w_out_b, ln_g3, ln_b3, layer=l, which=0, final=False)
        q, kt, v, g, gt = _cproj_call(s, mods, w_cp, b_cp, qk_conv_w, layer=l)
        hf, hb = _mlstm_call(q, kt, v, g, gt)
        s = _mix_call(s, mods, hf, hb, w_mx, b_mx, vec512, gmlp_ws_b, gmlp_bst, conv_w, pool_w_b, w_branch_b,
                      w_out_b, ln_g3, ln_b3, layer=l)
        s = _ffn_call(s, mods, ffn_w_in_b, ffn_w_out_b, ln_g3, ln_b3, layer=l, which=1, final=last)
    return s.reshape(BATCH, SEQ, D_MODEL)
```

```python
import functools
import math

import jax
import jax.numpy as jnp
from jax import lax
from jax.experimental import pallas as pl
from jax.experimental.pallas import tpu as pltpu

D_MODEL = 1024
BATCH = 2
SEQ = 8192
DEPTH = 4
GRID_W = 64
CTX_LEN = 256
MIX_W = D_MODEL // 2
N_BRANCH = 4
CHUNK = 128
GMLP_GROUPS = 4
CONV_W = 31
MLSTM_HEADS = 4
MLSTM_DH = MIX_W // MLSTM_HEADS
QK_CONV = 3
POOL_WINDOWS = (2, 4, 8, 16)
POOL_GD = MIX_W // len(POOL_WINDOWS)
D_FF = 128 * ((8 * D_MODEL // 3 + 127) // 128)
ALPHA = (2 * DEPTH) ** 0.25
LN_EPS = 1e-6
FFN_RES = 0.5

OFF_A = 0
OFF_B = OFF_A + 2 * MIX_W
OFF_C = OFF_B + 2 * MIX_W
OFF_C_GATES = OFF_C + 3 * MIX_W
OFF_C_O = OFF_C_GATES + 4 * MLSTM_HEADS
OFF_D = OFF_C_O + MIX_W
OFF_G = OFF_D + MIX_W
IN_COLS = OFF_G + N_BRANCH * D_MODEL

LANES = 128
SUBLANES = 8
BF16_ROWS = 16
TM = 256
SEQ_B = CTX_LEN + SEQ
T_ALL = BATCH * SEQ_B
NT_B = SEQ_B // TM
NT = BATCH * NT_B
NCH_B = SEQ_B // CHUNK
CH_T = TM // CHUNK
CTX_CH = CTX_LEN // CHUNK
HALO = 16
MXU_N = 256
LANE_PAD = LANES
FF_CH = MXU_N
FFN_TILES = 2
CONV_RB = 64
CHAIN_LAG = 3
GS_RMAX = 2 * MLSTM_HEADS
MOD_ROWS = 8
CTX_MOD_ROW = BATCH
VMEM_LIMIT = 56 * 1024 * 1024

MX_A = 0
MX_B = MX_A + 2 * MIX_W
MX_O = MX_B + 2 * MIX_W
MX_D = MX_O + MIX_W
MX_G = MX_D + MIX_W
MX_COLS = MX_G + N_BRANCH * D_MODEL
CP_QK = 0
CP_V = 2 * MIX_W
CP_GI = 3 * MIX_W
CP_GF = CP_GI + LANES
CP_COLS = CP_GF + LANES

F32 = jnp.float32
BF16 = jnp.bfloat16


def _dot(a, b):
    return jnp.dot(a, b, preferred_element_type=F32)


def _dot_f32(a, b):
    return jnp.dot(a, b, preferred_element_type=F32, precision=lax.Precision.HIGHEST)


def _ln(r, g, b):
    mu = jnp.mean(r, axis=-1, keepdims=True)
    xc = r - mu
    var = jnp.mean(xc * xc, axis=-1, keepdims=True)
    return xc * lax.rsqrt(var + LN_EPS) * g + b


def _sigmoid(x):
    return 0.5 * jnp.tanh(0.5 * x) + 0.5


def _silu(x):
    hx = 0.5 * x
    return hx * jnp.tanh(hx) + hx


def _gelu_tanh(x):
    c = math.sqrt(2.0 / math.pi)
    hx = 0.5 * x
    return hx * jnp.tanh(x * (c + (c * 0.044715) * (x * x))) + hx


def _mod_row(i):
    return jnp.where(i % NT_B == 0, CTX_MOD_ROW, i // NT_B)


def _const_spec(block, index):
    return pl.BlockSpec(block, lambda *_: index, pipeline_mode=pl.Buffered(1))


def _params(sem, flags=None):
    return pltpu.CompilerParams(dimension_semantics=sem, vmem_limit_bytes=VMEM_LIMIT, flags=flags)


def _mods_kernel(ct_ref, w_ref, b_ref, o_ref):
    s = _silu(ct_ref[...])
    w = w_ref[...]
    o_ref[...] = jnp.zeros(o_ref.shape, F32)
    for r in range(BATCH + 1):
        o_ref[r:r + 1, :] = jnp.sum(s[:, r:r + 1] * w, axis=0, keepdims=True) + b_ref[...]


def _mods_call(c_t, w_ada, b_ada):
    tn = D_MODEL
    n_col = w_ada.shape[-1] // tn
    return pl.pallas_call(
        _mods_kernel,
        grid=(DEPTH, n_col),
        in_specs=[
            pl.BlockSpec((D_MODEL, MOD_ROWS), lambda l, n: (0, 0)),
            pl.BlockSpec((None, D_MODEL, tn), lambda l, n: (l, 0, n)),
            pl.BlockSpec((None, 1, tn), lambda l, n: (l, 0, n)),
        ],
        out_specs=pl.BlockSpec((None, MOD_ROWS, tn), lambda l, n: (l, 0, n)),
        out_shape=jax.ShapeDtypeStruct((DEPTH, MOD_ROWS, w_ada.shape[-1]), F32),
        compiler_params=_params(("parallel", "parallel")),
        name="mods",
    )(c_t, w_ada, b_ada.reshape(DEPTH, 1, -1))


def _embed_kernel(x_ref, ctx_ref, o_ref):
    j = pl.program_id(0) % NT_B

    @pl.when(j == 0)
    def _():
        o_ref[...] = ctx_ref[...]

    @pl.when(j > 0)
    def _():
        quarter = D_MODEL // 4
        grid_rows = TM // GRID_W
        k = lax.broadcasted_iota(jnp.int32, (1, quarter), 1).astype(F32)
        freqs = jnp.exp(-math.log(10000.0) * k / quarter)
        r = ((j - 1) * grid_rows + lax.broadcasted_iota(jnp.int32, (SUBLANES, 1), 0)).astype(F32)
        col = lax.broadcasted_iota(jnp.int32, (GRID_W, 1), 0).astype(F32)
        er = r * freqs
        ec = col * freqs
        sin_r, cos_r, sin_c, cos_c = jnp.sin(er), jnp.cos(er), jnp.sin(ec), jnp.cos(ec)
        for q in range(grid_rows):
            rows = slice(q * GRID_W, (q + 1) * GRID_W)
            o_ref[rows, 0 * quarter:1 * quarter] = x_ref[rows, 0 * quarter:1 * quarter] + sin_r[q:q + 1, :]
            o_ref[rows, 1 * quarter:2 * quarter] = x_ref[rows, 1 * quarter:2 * quarter] + cos_r[q:q + 1, :]
            o_ref[rows, 2 * quarter:3 * quarter] = x_ref[rows, 2 * quarter:3 * quarter] + sin_c
            o_ref[rows, 3 * quarter:4 * quarter] = x_ref[rows, 3 * quarter:4 * quarter] + cos_c


def _embed_call(x, ctx):
    lat_tiles_b = SEQ // TM
    return pl.pallas_call(
        _embed_kernel,
        grid=(NT,),
        in_specs=[
            pl.BlockSpec((TM, D_MODEL), lambda i: ((i // NT_B) * lat_tiles_b + jnp.maximum(i % NT_B - 1, 0), 0)),
            pl.BlockSpec((TM, D_MODEL), lambda i: (i // NT_B, 0)),
        ],
        out_specs=pl.BlockSpec((TM, D_MODEL), lambda i: (i, 0)),
        out_shape=jax.ShapeDtypeStruct((T_ALL, D_MODEL), F32),
        compiler_params=_params(("parallel",)),
        name="embed",
    )(x.reshape(BATCH * SEQ, D_MODEL), ctx.reshape(BATCH * CTX_LEN, D_MODEL))


def _ffn_kernel(*refs, sub):
    x_refs = refs[:FFN_TILES]
    mod_refs = refs[FFN_TILES:2 * FFN_TILES]
    win_ref, wout_ref, g_ref, b_ref, o_ref, xm_ref, h_ref = refs[2 * FFN_TILES:]
    for t, (x_ref, mod_ref) in enumerate(zip(x_refs, mod_refs)):
        shift = mod_ref[3 * sub + 0:3 * sub + 1, :]
        scale = mod_ref[3 * sub + 1:3 * sub + 2, :]
        xm_ref[t * TM:(t + 1) * TM, :] = (x_ref[...] * (1.0 + scale) + shift).astype(BF16)
    xm = xm_ref[...]
    for c in range(D_FF // FF_CH):
        a1 = _dot(xm, win_ref[:, c * FF_CH:(c + 1) * FF_CH])
        a2 = _dot(xm, win_ref[:, D_FF + c * FF_CH:D_FF + (c + 1) * FF_CH])
        h_ref[:, c * FF_CH:(c + 1) * FF_CH] = (_silu(a1) * a2).astype(BF16)
    for t, (x_ref, mod_ref) in enumerate(zip(x_refs, mod_refs)):
        hid = h_ref[t * TM:(t + 1) * TM, :]
        y = jnp.concatenate([_dot(hid, wout_ref[:, n * MXU_N:(n + 1) * MXU_N]) for n in range(D_MODEL // MXU_N)],
                            axis=1)
        gate = mod_ref[3 * sub + 2:3 * sub + 3, :]
        r = ALPHA * x_ref[...] + (FFN_RES * gate) * y
        o_ref[t * TM:(t + 1) * TM, :] = _ln(r, g_ref[...], b_ref[...])


def _ffn_call(s, mods, ffn_w_in, ffn_w_out, ln_g, ln_b, *, layer, which, final):
    sub = 2 * which
    if final:
        lat_tiles_b = SEQ // TM
        out_tiles = BATCH * lat_tiles_b
        in_tile = lambda t: (t // lat_tiles_b) * NT_B + 1 + t % lat_tiles_b
    else:
        out_tiles = NT
        in_tile = lambda t: t
    tile_of = lambda i, t: in_tile(i * FFN_TILES + t)
    x_specs = [pl.BlockSpec((TM, D_MODEL), functools.partial(lambda i, t: (tile_of(i, t), 0), t=t))
               for t in range(FFN_TILES)]
    mod_specs = [pl.BlockSpec((None, 9, D_MODEL),
                              functools.partial(lambda i, t: (layer * MOD_ROWS + _mod_row(tile_of(i, t)), 0, 0), t=t))
                 for t in range(FFN_TILES)]
    return pl.pallas_call(
        functools.partial(_ffn_kernel, sub=sub),
        grid=(out_tiles // FFN_TILES,),
        in_specs=x_specs + mod_specs + [
            _const_spec((None, None, D_MODEL, 2 * D_FF), (layer, which, 0, 0)),
            _const_spec((None, None, D_FF, D_MODEL + LANE_PAD), (layer, which, 0, 0)),
            _const_spec((None, 1, D_MODEL), (layer * 3 + sub, 0, 0)),
            _const_spec((None, 1, D_MODEL), (layer * 3 + sub, 0, 0)),
        ],
        out_specs=pl.BlockSpec((FFN_TILES * TM, D_MODEL), lambda i: (i, 0)),
        out_shape=jax.ShapeDtypeStruct((out_tiles * TM, D_MODEL), F32),
        scratch_shapes=[
            pltpu.VMEM((FFN_TILES * TM, D_MODEL), BF16),
            pltpu.VMEM((FFN_TILES * TM, D_FF), BF16),
        ],
        compiler_params=_params(("parallel",)),
        name=f"ffn{which}",
    )(*([s] * FFN_TILES), *([mods] * FFN_TILES), ffn_w_in, ffn_w_out, ln_g, ln_b)


def _cproj_kernel(x_ref, prev_ref, next_ref, mod_ref, w_ref, b_ref, cw_ref,
                  q_ref, kt_ref, v_ref, g_ref, gt_ref, xe_ref, p_ref):
    j = pl.program_id(0) % NT_B
    first = jnp.logical_or(j == 0, j == 1)
    last = jnp.logical_or(j == 0, j == NT_B - 1)
    shift = mod_ref[3:4, :]
    scale = mod_ref[4:5, :]

    def modulate(v):
        return (v * (1.0 + scale) + shift).astype(BF16)

    xe_ref[0:HALO, :] = modulate(prev_ref[...])
    xe_ref[HALO:HALO + TM, :] = modulate(x_ref[...])
    xe_ref[HALO + TM:2 * HALO + TM, :] = modulate(next_ref[...])
    xe = xe_ref[...]
    rows = lax.broadcasted_iota(jnp.int32, (TM + 2 * HALO, 1), 0)
    valid = jnp.logical_and(jnp.logical_or(rows >= HALO, jnp.logical_not(first)),
                            jnp.logical_or(rows < HALO + TM, jnp.logical_not(last)))
    p_ref[...] = jnp.where(valid, _dot(xe, w_ref[:, CP_QK:CP_V]) + b_ref[:, CP_QK:CP_V], 0.0)
    conv = (cw_ref[0:1, :] * p_ref[HALO - 1:HALO - 1 + TM, :]
            + cw_ref[1:2, :] * p_ref[HALO:HALO + TM, :]
            + cw_ref[2:3, :] * p_ref[HALO + 1:HALO + 1 + TM, :])
    qk = _silu(conv)
    q_ref[...] = qk[:, :MIX_W].astype(BF16)
    k = qk[:, MIX_W:] * MLSTM_DH ** -0.5
    for c in range(CH_T):
        for h in range(MLSTM_HEADS):
            blk = k[c * CHUNK:(c + 1) * CHUNK, h * MLSTM_DH:(h + 1) * MLSTM_DH]
            r0 = (c * MLSTM_HEADS + h) * MLSTM_DH
            kt_ref[r0:r0 + MLSTM_DH, :] = blk.T.astype(BF16)

    xm = xe_ref[HALO:HALO + TM, :]
    pvg = _dot(xm, w_ref[:, CP_V:CP_COLS]) + b_ref[:, CP_V:CP_COLS]
    v_ref[...] = pvg[:, :MIX_W].astype(BF16)
    li = pvg[:, CP_GI - CP_V:CP_GF - CP_V]
    fraw = pvg[:, CP_GF - CP_V:]
    lf = jnp.minimum(fraw, 0.0) - jnp.log1p(jnp.exp(-jnp.abs(fraw)))
    row = lax.broadcasted_iota(jnp.int32, (CHUNK, LANES), 0)
    lane = lax.broadcasted_iota(jnp.int32, (CHUNK, LANES), 1)
    fwd_lane = lane < MLSTM_HEADS
    lower_f = (lane <= row).astype(F32)
    for c in range(CH_T):
        tok = slice(c * CHUNK, (c + 1) * CHUNK)
        prefix = _dot_f32(lower_f, lf[tok])
        suffix = prefix[CHUNK - 1:CHUNK, :] - prefix + lf[tok]
        cum = jnp.where(fwd_lane, prefix, suffix)
        r = li[tok] - cum
        rf, rb = r, r
        k = 1
        while k < CHUNK:
            rf = jnp.maximum(rf, jnp.where(row >= k, pltpu.roll(rf, k, 0), -jnp.inf))
            rb = jnp.maximum(rb, jnp.where(row < CHUNK - k, pltpu.roll(rb, CHUNK - k, 0), -jnp.inf))
            k *= 2
        rmax = jnp.where(fwd_lane, rf, rb)
        g_ref[tok, :] = jnp.where(lane < GS_RMAX, cum, pltpu.roll(rmax, GS_RMAX, 1))
        gt_ref[c * SUBLANES:(c + 1) * SUBLANES, :] = r.T[:SUBLANES, :]


def _cproj_call(s, mods, w_cp, b_cp, qk_conv_w, *, layer):
    hb = TM // HALO
    n_hb = T_ALL // HALO
    return pl.pallas_call(
        _cproj_kernel,
        grid=(NT,),
        in_specs=[
            pl.BlockSpec((TM, D_MODEL), lambda i: (i, 0)),
            pl.BlockSpec((HALO, D_MODEL), lambda i: (jnp.maximum(i * hb - 1, 0), 0)),
            pl.BlockSpec((HALO, D_MODEL), lambda i: (jnp.minimum((i + 1) * hb, n_hb - 1), 0)),
            pl.BlockSpec((None, 9, D_MODEL), lambda i: (layer * MOD_ROWS + _mod_row(i), 0, 0)),
            _const_spec((None, D_MODEL, CP_COLS), (layer, 0, 0)),
            _const_spec((None, 1, CP_COLS), (layer, 0, 0)),
            _const_spec((None, QK_CONV, 2 * MIX_W), (layer, 0, 0)),
        ],
        out_specs=[
            pl.BlockSpec((TM, MIX_W), lambda i: (i, 0)),
            pl.BlockSpec((TM * MLSTM_HEADS, MLSTM_DH), lambda i: (i, 0)),
            pl.BlockSpec((TM, MIX_W), lambda i: (i, 0)),
            pl.BlockSpec((TM, LANES), lambda i: (i, 0)),
            pl.BlockSpec((CH_T * SUBLANES, CHUNK), lambda i: (i, 0)),
        ],
        out_shape=[
            jax.ShapeDtypeStruct((T_ALL, MIX_W), BF16),
            jax.ShapeDtypeStruct((T_ALL * MLSTM_HEADS, MLSTM_DH), BF16),
            jax.ShapeDtypeStruct((T_ALL, MIX_W), BF16),
            jax.ShapeDtypeStruct((T_ALL, LANES), F32),
            jax.ShapeDtypeStruct((T_ALL // CHUNK * SUBLANES, CHUNK), F32),
        ],
        scratch_shapes=[
            pltpu.VMEM((TM + 2 * HALO, D_MODEL), BF16),
            pltpu.VMEM((TM + 2 * HALO, 2 * MIX_W), F32),
        ],
        compiler_params=_params(("parallel",)),
        name="cproj",
    )(s, s, s, mods, w_cp, b_cp, qk_conv_w)


def _mlstm_kernel(qf_ref, ktf_ref, vf_ref, gf_ref, gtf_ref, qb_ref, ktb_ref, vb_ref, gb_ref, gtb_ref,
                  hf_ref, hb_ref, c_ref, m_ref):
    @pl.when(pl.program_id(0) == 0)
    def _():
        c_ref[...] = jnp.zeros(c_ref.shape, F32)
        m_ref[...] = jnp.zeros(m_ref.shape, F32)

    row = lax.broadcasted_iota(jnp.int32, (CHUNK, CHUNK), 0)
    col = lax.broadcasted_iota(jnp.int32, (CHUNK, CHUNK), 1)
    ones_col = jnp.ones((CHUNK, MLSTM_DH), BF16)
    nh = MLSTM_HEADS
    dirs = ((qf_ref, ktf_ref, vf_ref, gf_ref, gtf_ref, hf_ref, col <= row, CHUNK - 1),
            (qb_ref, ktb_ref, vb_ref, gb_ref, gtb_ref, hb_ref, col >= row, 0))
    heads = []
    for d, (q_ref, kt_ref, v_ref, g_ref, gt_ref, h_ref, seen, end_row) in enumerate(dirs):
        for b in range(BATCH):
            g = g_ref[b]
            gt = gt_ref[b]
            for h in range(nh):
                lane = d * nh + h
                hd = slice(h * MLSTM_DH, (h + 1) * MLSTM_DH)
                heads.append(dict(
                    idx=(d * BATCH + b) * nh + h, seen=seen, end_row=end_row, out=(h_ref, b, hd),
                    cum=g[:, lane:lane + 1], rmax=g[:, GS_RMAX + lane:GS_RMAX + lane + 1], r=gt[lane:lane + 1, :],
                    q=q_ref[b, :, hd], kt=kt_ref[b, hd, :], v=v_ref[b, :, hd]))

    for hd in heads:
        hd["qk"] = _dot(hd["q"], hd["kt"])
        hd["m_prev"] = m_ref[hd["idx"], 0:1, 0:1]
        hd["c_old"] = c_ref[hd["idx"]]
        hd["v_aug"] = jnp.concatenate([hd["v"], ones_col], axis=1)
    for hd in heads:
        e = hd["end_row"]
        top = jnp.maximum(hd["m_prev"], hd["rmax"][e:e + 1, :])
        w_row = jnp.exp(hd["r"] - top)
        ktw = (hd["kt"].astype(F32) * w_row).astype(BF16)
        c_ref[hd["idx"]] = jnp.exp(hd["m_prev"] - top) * hd["c_old"] + _dot(ktw, hd["v_aug"])
        m_ref[hd["idx"]] = jnp.broadcast_to(hd["cum"][e:e + 1, :] + top, m_ref.shape[1:])
    for hd in heads:
        m_prev = hd["m_prev"]
        top = jnp.broadcast_to(jnp.maximum(m_prev, hd["rmax"]), (CHUNK, CHUNK))
        s = hd["qk"] * jnp.exp(jnp.where(hd["seen"], hd["r"] - top, -jnp.inf))
        qw = hd["q"].astype(F32) * jnp.exp(m_prev - top)
        lhs = jnp.concatenate([s.astype(BF16), qw.astype(BF16)], axis=1)
        rhs = jnp.concatenate([hd["v_aug"], hd["c_old"].astype(BF16)], axis=0)
        hd["na"] = _dot(lhs, rhs)
        hd["floor"] = jnp.exp(-(hd["cum"] + top))
    for hd in heads:
        den = jnp.maximum(jnp.abs(hd["na"][:, MLSTM_DH:]), hd["floor"])
        h_ref, b, cols = hd["out"]
        h_ref[b, :, cols] = hd["na"][:, :MLSTM_DH] / den


def _mlstm_call(q, kt, v, g, gt):
    fwd = lambda s: s
    bwd = lambda s: jnp.where(s < CTX_CH, CTX_CH - 1 - s, NCH_B + CTX_CH - 1 - s)
    q3 = q.reshape(BATCH, SEQ_B, MIX_W)
    kt3 = kt.reshape(BATCH, SEQ_B * MLSTM_HEADS, MLSTM_DH)
    v3 = v.reshape(BATCH, SEQ_B, MIX_W)
    g3 = g.reshape(BATCH, SEQ_B, LANES)
    gt3 = gt.reshape(BATCH, NCH_B * SUBLANES, CHUNK)

    def specs(order):
        return [
            pl.BlockSpec((BATCH, CHUNK, MIX_W), lambda s: (0, order(s), 0)),
            pl.BlockSpec((BATCH, CHUNK * MLSTM_HEADS, MLSTM_DH), lambda s: (0, order(s), 0)),
            pl.BlockSpec((BATCH, CHUNK, MIX_W), lambda s: (0, order(s), 0)),
            pl.BlockSpec((BATCH, CHUNK, LANES), lambda s: (0, order(s), 0)),
            pl.BlockSpec((BATCH, SUBLANES, CHUNK), lambda s: (0, order(s), 0)),
        ]

    n_state = 2 * BATCH * MLSTM_HEADS
    hf, hb = pl.pallas_call(
        _mlstm_kernel,
        grid=(NCH_B,),
        in_specs=specs(fwd) + specs(bwd),
        out_specs=[
            pl.BlockSpec((BATCH, CHUNK, MIX_W), lambda s: (0, fwd(s), 0)),
            pl.BlockSpec((BATCH, CHUNK, MIX_W), lambda s: (0, bwd(s), 0)),
        ],
        out_shape=[jax.ShapeDtypeStruct((BATCH, SEQ_B, MIX_W), F32)] * 2,
        scratch_shapes=[
            pltpu.VMEM((n_state, MLSTM_DH, 2 * MLSTM_DH), F32),
            pltpu.VMEM((n_state, SUBLANES, LANES), F32),
        ],
        compiler_params=_params(("arbitrary",)),
        name="mlstm",
    )(q3, kt3, v3, g3, gt3, q3, kt3, v3, g3, gt3)
    return hf.reshape(T_ALL, MIX_W), hb.reshape(T_ALL, MIX_W)


VEC_GMLP_G, VEC_GMLP_B, VEC_CONV_B, VEC_CONV_G, VEC_CONV_LB, VEC_MLSTM_G, VEC_POOL_S = range(7)
VEC_ROWS = 8


def _mix_kernel(x_ref, prev_ref, next_ref, mod_ref, hf_ref, hb_ref, w_ref, b_ref, vec_ref, ws_ref, bst_ref,
                cw_ref, pw_ref, wbr_ref, wout_ref, g_ref, beta_ref, o_ref, xe_ref, a_ref, d_ref, ash_ref, conv_ref,
                gate_ref, pa_ref, po_ref, u_ref, vn_ref, yc_ref, yd_ref):
    j = pl.program_id(0) % NT_B
    is_ctx = j == 0
    first = jnp.logical_or(is_ctx, j == 1)
    last = jnp.logical_or(is_ctx, j == NT_B - 1)
    shift = mod_ref[3:4, :]
    scale = mod_ref[4:5, :]
    gate = mod_ref[5:6, :]

    def modulate(v):
        return (v * (1.0 + scale) + shift).astype(BF16)

    def vec(r):
        return vec_ref[r:r + 1, :]

    def proj(ext, lo, width, zero=None):
        lhs = xe_ref[...] if ext else xe_ref[HALO:HALO + TM, :]
        bias = lambda c: b_ref[:, c:c + MXU_N] if zero is None else b_ref[:, c:c + MXU_N] + zero
        return jnp.concatenate([_dot(lhs, w_ref[:, c:c + MXU_N]) + bias(c)
                                for c in range(lo, lo + width, MXU_N)], axis=1)

    xe_ref[0:HALO, :] = modulate(prev_ref[...])
    xe_ref[HALO:HALO + TM, :] = modulate(x_ref[...])
    xe_ref[HALO + TM:2 * HALO + TM, :] = modulate(next_ref[...])
    rows = lax.broadcasted_iota(jnp.int32, (TM + 2 * HALO, 1), 0)
    valid = jnp.logical_and(jnp.logical_or(rows >= HALO, jnp.logical_not(first)),
                            jnp.logical_or(rows < HALO + TM, jnp.logical_not(last)))

    pb = proj(True, MX_B, 2 * MIX_W)
    a_ref[...] = jnp.where(valid, pb[:, :MIX_W] * _sigmoid(pb[:, MIX_W:]), 0.0)
    pa_ref[...] = proj(False, MX_A, 2 * MIX_W)

    gate_chunks = [(i, n) for i in range(N_BRANCH) for n in range(D_MODEL // MXU_N)]
    sh_rows = TM + 2 * HALO - SUBLANES
    for s in range(1, SUBLANES):
        ash_ref[s - 1, 0:sh_rows, :] = a_ref[s:s + sh_rows, :]

    def zero_of(v):
        bits = lax.bitcast_convert_type(v[0:1, 0:LANES], jnp.uint32)
        return lax.shift_right_logical(lax.shift_right_logical(bits, jnp.uint32(16)), jnp.uint32(16)).astype(F32)

    def conv_block(cb, rb, z):
        cols = slice(cb * LANES, (cb + 1) * LANES)
        part = None
        for k in range(CONV_W):
            q8, s = divmod(HALO - CONV_W // 2 + k, SUBLANES)
            r0 = q8 * SUBLANES + rb * CONV_RB
            src = a_ref[r0:r0 + CONV_RB, cols] if s == 0 else ash_ref[s - 1, r0:r0 + CONV_RB, cols]
            term = (cw_ref[k:k + 1, cols] + z) * src
            part = term if part is None else part + term
        conv_ref[rb * CONV_RB:(rb + 1) * CONV_RB, cols] = part
        return part

    conv_blocks = [(cb, rb) for cb in range(MIX_W // LANES) for rb in range(TM // CONV_RB)]
    per_step = len(conv_blocks) // len(gate_chunks)
    zero_row = jnp.zeros((1, LANES), F32)
    v_zero = [zero_row] * len(gate_chunks)
    for step, (i, n) in enumerate(gate_chunks):
        lo = i * D_MODEL + n * MXU_N
        z_in = jnp.concatenate([v_zero[step - CHAIN_LAG]] * (MXU_N // LANES), axis=1) if step >= CHAIN_LAG else None
        gv = jnp.tanh(0.5 * proj(False, MX_G + lo, MXU_N, z_in))
        gate_ref[:, lo:lo + MXU_N] = gv
        z = zero_of(gv) + v_zero[step - 1]
        for cb, rb in conv_blocks[step * per_step:(step + 1) * per_step]:
            z = zero_of(conv_block(cb, rb, z))
        v_zero[step] = z

    pa = _gelu_tanh(pa_ref[...])
    u_ref[...] = pa[:, :MIX_W]
    vn_ref[...] = _ln(pa[:, MIX_W:], vec(VEC_GMLP_G), vec(VEC_GMLP_B)).astype(BF16)

    d_ref[...] = jnp.where(valid, proj(True, MX_D, MIX_W), 0.0)
    pos = jnp.where(is_ctx, 0, (j - 1) * TM) + lax.broadcasted_iota(jnp.int32, (TM, 1), 0)
    n_seq = jnp.where(is_ctx, CTX_LEN, SEQ)
    for gi, win in enumerate(POOL_WINDOWS):
        lo, hi = win // 2, win - 1 - win // 2
        cols = slice(gi * POOL_GD, (gi + 1) * POOL_GD)
        wsum = d_ref[HALO - lo:HALO - lo + TM, cols]
        for k in range(-lo + 1, hi + 1):
            wsum = wsum + d_ref[HALO + k:HALO + k + TM, cols]
        cnt = (jnp.minimum(pos + hi + 1, n_seq) - jnp.maximum(pos - lo, 0)).astype(F32)
        diff = wsum / cnt - d_ref[HALO:HALO + TM, cols]
        yd_ref[:, cols] = (_dot(diff.astype(BF16), pw_ref[gi]) * vec_ref[VEC_POOL_S:VEC_POOL_S + 1, cols]
                           ).astype(BF16)

    po_ref[...] = proj(False, MX_O, MIX_W)
    for h in range(MLSTM_HEADS):
        cols = slice(h * MLSTM_DH, (h + 1) * MLSTM_DH)
        hh = hf_ref[:, cols] + hb_ref[:, cols]
        mu = jnp.mean(hh, axis=-1, keepdims=True)
        hc = hh - mu
        hn = hc * lax.rsqrt(jnp.mean(hc * hc, axis=-1, keepdims=True) + LN_EPS)
        yc_ref[:, cols] = (_sigmoid(po_ref[:, cols]) * (hn * vec_ref[VEC_MLSTM_G:VEC_MLSTM_G + 1, cols])
                           ).astype(BF16)

    gd = MIX_W // GMLP_GROUPS
    z_rows = []
    for c in range(CH_T):
        z_cols = []
        for gi in range(GMLP_GROUPS):
            blk = vn_ref[c * CHUNK:(c + 1) * CHUNK, gi * gd:(gi + 1) * gd]
            z_cols.append(_dot(ws_ref[gi], blk) + bst_ref[:, gi:gi + 1])
        z_rows.append(jnp.concatenate(z_cols, axis=1))
    ya = (u_ref[...] * jnp.concatenate(z_rows, axis=0)).astype(BF16)
    yb = _silu(_ln(conv_ref[...] + vec(VEC_CONV_B), vec(VEC_CONV_G), vec(VEC_CONV_LB))).astype(BF16)
    yc = yc_ref[...]
    yd = yd_ref[...]

    merged = []
    for n in range(D_MODEL // MXU_N):
        cols = slice(n * MXU_N, (n + 1) * MXU_N)
        acc = None
        for i, yi in enumerate((ya, yb, yc, yd)):
            lo = i * D_MODEL + n * MXU_N
            p = _dot(yi, wbr_ref[i, :, cols])
            term = gate_ref[:, lo:lo + MXU_N] * p + p
            acc = term if acc is None else acc + term
        merged.append((0.5 * acc).astype(BF16))
    merged = jnp.concatenate(merged, axis=1)
    y = jnp.concatenate([_dot(merged, wout_ref[:, n * MXU_N:(n + 1) * MXU_N]) for n in range(D_MODEL // MXU_N)],
                        axis=1)
    o_ref[...] = _ln(ALPHA * x_ref[...] + gate * y, g_ref[...], beta_ref[...])


def _mix_call(s, mods, hf, hb, w_mx, b_mx, vec512, gmlp_ws, gmlp_bst, conv_w, pool_w, w_branch, w_out,
              ln_g, ln_b, *, layer):
    hb_per_tile = TM // HALO
    n_hb = T_ALL // HALO
    return pl.pallas_call(
        _mix_kernel,
        grid=(NT,),
        in_specs=[
            pl.BlockSpec((TM, D_MODEL), lambda i: (i, 0)),
            pl.BlockSpec((HALO, D_MODEL), lambda i: (jnp.maximum(i * hb_per_tile - 1, 0), 0)),
            pl.BlockSpec((HALO, D_MODEL), lambda i: (jnp.minimum((i + 1) * hb_per_tile, n_hb - 1), 0)),
            pl.BlockSpec((None, 9, D_MODEL), lambda i: (layer * MOD_ROWS + _mod_row(i), 0, 0)),
            pl.BlockSpec((TM, MIX_W), lambda i: (i, 0)),
            pl.BlockSpec((TM, MIX_W), lambda i: (i, 0)),
            _const_spec((None, D_MODEL, MX_W), (layer, 0, 0)),
            _const_spec((None, 1, MX_COLS), (layer, 0, 0)),
            _const_spec((None, VEC_ROWS, MIX_W), (layer, 0, 0)),
            _const_spec((None, GMLP_GROUPS, CHUNK, CHUNK), (layer, 0, 0, 0)),
            _const_spec((None, CHUNK, GMLP_GROUPS), (layer, 0, 0)),
            _const_spec((None, CONV_W, MIX_W), (layer, 0, 0)),
            _const_spec((None, len(POOL_WINDOWS), POOL_GD, POOL_GD), (layer, 0, 0, 0)),
            _const_spec((None, N_BRANCH, MIX_W, D_MODEL + LANE_PAD), (layer, 0, 0, 0)),
            _const_spec((None, D_MODEL, D_MODEL + LANE_PAD), (layer, 0, 0)),
            _const_spec((None, 1, D_MODEL), (layer * 3 + 1, 0, 0)),
            _const_spec((None, 1, D_MODEL), (layer * 3 + 1, 0, 0)),
        ],
        out_specs=pl.BlockSpec((TM, D_MODEL), lambda i: (i, 0)),
        out_shape=jax.ShapeDtypeStruct((T_ALL, D_MODEL), F32),
        scratch_shapes=[
            pltpu.VMEM((TM + 2 * HALO, D_MODEL), BF16),
            pltpu.VMEM((TM + 2 * HALO, MIX_W), F32),
            pltpu.VMEM((TM + 2 * HALO, MIX_W), F32),
            pltpu.VMEM((SUBLANES - 1, TM + 2 * HALO, MIX_W), F32),
            pltpu.VMEM((TM, MIX_W), F32),
            pltpu.VMEM((TM, N_BRANCH * D_MODEL), F32),
            pltpu.VMEM((TM, 2 * MIX_W), F32),
            pltpu.VMEM((TM, MIX_W), F32),
            pltpu.VMEM((TM, MIX_W), F32),
            pltpu.VMEM((TM, MIX_W), BF16),
            pltpu.VMEM((TM, MIX_W), BF16),
            pltpu.VMEM((TM, MIX_W), BF16),
        ],
        compiler_params=_params(("parallel",)),
        name="mix",
    )(s, s, s, mods, hf, hb, w_mx, b_mx, vec512, gmlp_ws, gmlp_bst, conv_w, pool_w, w_branch, w_out, ln_g, ln_b)


MX_TILE = 512
MX_PAD = MX_TILE
MX_W = MX_COLS + MX_PAD
CP_TILE = 256


def _mx_src(j):
    a, o, d = (MX_O - MX_A) // MX_TILE, (MX_D - MX_A) // MX_TILE, (MX_G - MX_A) // MX_TILE
    g_end = MX_COLS // MX_TILE
    return jnp.where(j < a, OFF_A + MX_TILE * j,
                     jnp.where(j < o, OFF_C_O + MX_TILE * (j - a),
                               jnp.where(j < d, OFF_D + MX_TILE * (j - o),
                                         jnp.where(j < g_end, OFF_G + MX_TILE * (j - d), 0))))


def _prep_mx_kernel(wt_ref, o_ref):
    j = pl.program_id(1)

    @pl.when(j < MX_COLS // MX_TILE)
    def _():
        o_ref[...] = wt_ref[0].T.astype(BF16)

    @pl.when(j >= MX_COLS // MX_TILE)
    def _():
        o_ref[...] = jnp.zeros(o_ref.shape, BF16)


def _prep_cp_kernel(wt_ref, o_ref):
    j = pl.program_id(1)
    n_plain = (CP_GI - CP_QK) // CP_TILE

    @pl.when(j < n_plain)
    def _():
        o_ref[...] = wt_ref[0].T.astype(BF16)

    @pl.when(j == n_plain)
    def _():
        nh = MLSTM_HEADS
        t = wt_ref[0, 0:LANES, :].T
        lane = lax.broadcasted_iota(jnp.int32, t.shape, 1)
        up1 = pltpu.roll(t, LANES - nh, 1)
        up2 = pltpu.roll(t, LANES - 2 * nh, 1)
        gi = jnp.where(lane < nh, t, jnp.where(lane < 2 * nh, up1, 0.0))
        gf = jnp.where(lane < nh, up1, jnp.where(lane < 2 * nh, up2, 0.0))
        o_ref[:, 0:LANES] = gi.astype(BF16)
        o_ref[:, LANES:2 * LANES] = gf.astype(BF16)


def _prep_win_call(w_in):
    wt = jnp.swapaxes(w_in, 1, 2)
    n_plain = (CP_GI - CP_QK) // CP_TILE
    w_mx = pl.pallas_call(
        _prep_mx_kernel,
        grid=(DEPTH, MX_W // MX_TILE),
        in_specs=[pl.BlockSpec((pl.Element(1), pl.Element(MX_TILE), pl.Element(D_MODEL)),
                               lambda l, j: (l, pl.multiple_of(_mx_src(j), SUBLANES), 0))],
        out_specs=pl.BlockSpec((None, D_MODEL, MX_TILE), lambda l, j: (l, 0, j)),
        out_shape=jax.ShapeDtypeStruct((DEPTH, D_MODEL, MX_W), BF16),
        compiler_params=_params(("parallel", "parallel")),
        name="prep_mx",
    )(wt)
    w_cp = pl.pallas_call(
        _prep_cp_kernel,
        grid=(DEPTH, CP_COLS // CP_TILE),
        in_specs=[pl.BlockSpec((pl.Element(1), pl.Element(CP_TILE), pl.Element(D_MODEL)),
                               lambda l, j: (l, pl.multiple_of(
                                   jnp.where(j < n_plain, OFF_C + CP_TILE * j, OFF_C_GATES), SUBLANES), 0))],
        out_specs=pl.BlockSpec((None, D_MODEL, CP_TILE), lambda l, j: (l, 0, j)),
        out_shape=jax.ShapeDtypeStruct((DEPTH, D_MODEL, CP_COLS), BF16),
        compiler_params=_params(("parallel", "parallel")),
        name="prep_cp",
    )(wt)
    return w_mx, w_cp


def _pad_cast_kernel(w_ref, o_ref):
    cols = w_ref.shape[1]
    o_ref[:, 0:cols] = w_ref[...].astype(BF16)
    o_ref[:, cols:] = jnp.zeros((o_ref.shape[0], o_ref.shape[1] - cols), BF16)


def _pad_cast_call(w):
    lead, cols = w.shape[:-1], w.shape[-1]
    rows = math.prod(lead)
    rb = 512
    out = pl.pallas_call(
        _pad_cast_kernel,
        grid=(rows // rb,),
        in_specs=[pl.BlockSpec((rb, cols), lambda i: (i, 0))],
        out_specs=pl.BlockSpec((rb, cols + LANE_PAD), lambda i: (i, 0)),
        out_shape=jax.ShapeDtypeStruct((rows, cols + LANE_PAD), BF16),
        compiler_params=_params(("parallel",)),
        name="pad_cast",
    )(w.reshape(rows, cols))
    return out.reshape(lead + (cols + LANE_PAD,))


def _repack_cols(w, pieces):
    parts = [jnp.zeros(w.shape[:-1] + (p,), w.dtype) if isinstance(p, int) else w[..., p[0]:p[1]] for p in pieces]
    return jnp.concatenate(parts, axis=-1)


def kernel(x, c, ctx, c_ctx, w_ada, b_ada, ln_g, ln_b, ffn_w_in, ffn_w_out, w_in, b_in, gmlp_ln_g, gmlp_ln_b,
           gmlp_ws, gmlp_bs, conv_w, conv_b, conv_ln_g, conv_ln_b, qk_conv_w, mlstm_ln_g, pool_w, pool_scale,
           w_branch, w_out):
    mix_pieces = [(OFF_A, OFF_C), (OFF_C_O, OFF_D), (OFF_D, OFF_G), (OFF_G, IN_COLS)]
    nh = MLSTM_HEADS
    gate_pad = LANES - 2 * nh

    def cproj_cols(w):
        g = w[..., OFF_C_GATES:OFF_C_O].reshape(w.shape[:-1] + (4, nh))
        pad = jnp.zeros(w.shape[:-1] + (gate_pad,), w.dtype)
        return jnp.concatenate([w[..., OFF_C:OFF_C_GATES], g[..., 0, :], g[..., 2, :], pad,
                                g[..., 1, :], g[..., 3, :], pad], axis=-1)
    w_mx, w_cp = _prep_win_call(w_in)
    b_mx = _repack_cols(b_in, mix_pieces).reshape(DEPTH, 1, MX_COLS)
    b_cp = cproj_cols(b_in).reshape(DEPTH, 1, CP_COLS)
    ffn_w_in_b = ffn_w_in.astype(BF16)
    ffn_w_out_b = _pad_cast_call(ffn_w_out)
    w_branch_b = _pad_cast_call(w_branch)
    w_out_b = _pad_cast_call(w_out)
    gmlp_ws_b = gmlp_ws.astype(BF16)
    pool_w_b = pool_w.astype(BF16)
    gmlp_bst = jnp.swapaxes(gmlp_bs, 1, 2)
    vec512 = jnp.stack([gmlp_ln_g, gmlp_ln_b, conv_b, conv_ln_g, conv_ln_b, mlstm_ln_g, pool_scale,
                        jnp.zeros_like(pool_scale)], axis=1)
    ln_g3 = ln_g.reshape(DEPTH * 3, 1, D_MODEL)
    ln_b3 = ln_b.reshape(DEPTH * 3, 1, D_MODEL)
    c_t = jnp.concatenate([c, c_ctx[None], jnp.zeros((MOD_ROWS - BATCH - 1, D_MODEL), F32)], axis=0).T

    mods = _mods_call(c_t, w_ada, b_ada).reshape(DEPTH * MOD_ROWS, 9, D_MODEL)
    s = _embed_call(x, ctx)
    for l in range(DEPTH):
        last = l == DEPTH - 1
        s = _ffn_call(s, mods, ffn_w_in_b, ffn_w_out_b, ln_g3, ln_b3, layer=l, which=0, final=False)
        q, kt, v, g, gt = _cproj_call(s, mods, w_cp, b_cp, qk_conv_w, layer=l)
        hf, hb = _mlstm_call(q, kt, v, g, gt)
        s = _mix_call(s, mods, hf, hb, w_mx, b_mx, vec512, gmlp_ws_b, gmlp_bst, conv_w, pool_w_b, w_branch_b,
                      w_out_b, ln_g3, ln_b3, layer=l)
        s = _ffn_call(s, mods, ffn_w_in_b, ffn_w_out_b, ln_g3, ln_b3, layer=l, which=1, final=last)
    return s.reshape(BATCH, SEQ, D_MODEL)
```

```python
import functools
import math

import jax
import jax.numpy as jnp
from jax import lax
from jax.experimental import pallas as pl
from jax.experimental.pallas import tpu as pltpu

D_MODEL = 1024
BATCH = 2
SEQ = 8192
DEPTH = 4
GRID_W = 64
CTX_LEN = 256
MIX_W = D_MODEL // 2
N_BRANCH = 4
CHUNK = 128
GMLP_GROUPS = 4
CONV_W = 31
MLSTM_HEADS = 4
MLSTM_DH = MIX_W // MLSTM_HEADS
QK_CONV = 3
POOL_WINDOWS = (2, 4, 8, 16)
POOL_GD = MIX_W // len(POOL_WINDOWS)
D_FF = 128 * ((8 * D_MODEL // 3 + 127) // 128)
ALPHA = (2 * DEPTH) ** 0.25
LN_EPS = 1e-6
FFN_RES = 0.5

OFF_A = 0
OFF_B = OFF_A + 2 * MIX_W
OFF_C = OFF_B + 2 * MIX_W
OFF_C_GATES = OFF_C + 3 * MIX_W
OFF_C_O = OFF_C_GATES + 4 * MLSTM_HEADS
OFF_D = OFF_C_O + MIX_W
OFF_G = OFF_D + MIX_W
IN_COLS = OFF_G + N_BRANCH * D_MODEL

LANES = 128
SUBLANES = 8
TM = 256
SEQ_B = CTX_LEN + SEQ
T_ALL = BATCH * SEQ_B
NT_B = SEQ_B // TM
NT = BATCH * NT_B
NCH_B = SEQ_B // CHUNK
CH_T = TM // CHUNK
CTX_CH = CTX_LEN // CHUNK
HALO = 16
MXU_N = 256
LANE_PAD = LANES
FF_CH = MXU_N
FFN_TILES = 2
CONV_RB = 64
CHAIN_LAG = 3
GS_RMAX = 2 * MLSTM_HEADS
MOD_ROWS = 8
CTX_MOD_ROW = BATCH
VMEM_LIMIT = 56 * 1024 * 1024

MX_A = 0
MX_B = MX_A + 2 * MIX_W
MX_O = MX_B + 2 * MIX_W
MX_D = MX_O + MIX_W
MX_G = MX_D + MIX_W
MX_COLS = MX_G + N_BRANCH * D_MODEL
CP_QK = 0
CP_V = 2 * MIX_W
CP_GI = 3 * MIX_W
CP_GF = CP_GI + LANES
CP_COLS = CP_GF + LANES

F32 = jnp.float32
BF16 = jnp.bfloat16


def _dot(a, b):
    return jnp.dot(a, b, preferred_element_type=F32)


def _dot_f32(a, b):
    return jnp.dot(a, b, preferred_element_type=F32, precision=lax.Precision.HIGHEST)


def _ln(r, g, b):
    mu = jnp.mean(r, axis=-1, keepdims=True)
    xc = r - mu
    var = jnp.mean(xc * xc, axis=-1, keepdims=True)
    return xc * lax.rsqrt(var + LN_EPS) * g + b


def _sigmoid(x):
    return 0.5 * jnp.tanh(0.5 * x) + 0.5


def _silu(x):
    hx = 0.5 * x
    return hx * jnp.tanh(hx) + hx


def _gelu_tanh(x):
    c = math.sqrt(2.0 / math.pi)
    hx = 0.5 * x
    return hx * jnp.tanh(x * (c + (c * 0.044715) * (x * x))) + hx


def _mod_row(i):
    return jnp.where(i % NT_B == 0, CTX_MOD_ROW, i // NT_B)


def _const_spec(block, index):
    return pl.BlockSpec(block, lambda *_: index, pipeline_mode=pl.Buffered(1))


def _params(sem):
    return pltpu.CompilerParams(dimension_semantics=sem, vmem_limit_bytes=VMEM_LIMIT)


def _mods_kernel(ct_ref, w_ref, b_ref, o_ref):
    s = _silu(ct_ref[...])
    w = w_ref[...]
    o_ref[...] = jnp.zeros(o_ref.shape, F32)
    for r in range(BATCH + 1):
        o_ref[r:r + 1, :] = jnp.sum(s[:, r:r + 1] * w, axis=0, keepdims=True) + b_ref[...]


def _mods_call(c_t, w_ada, b_ada):
    tn = D_MODEL
    n_col = w_ada.shape[-1] // tn
    return pl.pallas_call(
        _mods_kernel,
        grid=(DEPTH, n_col),
        in_specs=[
            pl.BlockSpec((D_MODEL, MOD_ROWS), lambda l, n: (0, 0)),
            pl.BlockSpec((None, D_MODEL, tn), lambda l, n: (l, 0, n)),
            pl.BlockSpec((None, 1, tn), lambda l, n: (l, 0, n)),
        ],
        out_specs=pl.BlockSpec((None, MOD_ROWS, tn), lambda l, n: (l, 0, n)),
        out_shape=jax.ShapeDtypeStruct((DEPTH, MOD_ROWS, w_ada.shape[-1]), F32),
        compiler_params=_params(("parallel", "parallel")),
        name="mods",
    )(c_t, w_ada, b_ada.reshape(DEPTH, 1, -1))


def _embed_kernel(x_ref, ctx_ref, o_ref):
    j = pl.program_id(0) % NT_B

    @pl.when(j == 0)
    def _():
        o_ref[...] = ctx_ref[...]

    @pl.when(j > 0)
    def _():
        quarter = D_MODEL // 4
        grid_rows = TM // GRID_W
        k = lax.broadcasted_iota(jnp.int32, (1, quarter), 1).astype(F32)
        freqs = jnp.exp(-math.log(10000.0) * k / quarter)
        r = ((j - 1) * grid_rows + lax.broadcasted_iota(jnp.int32, (SUBLANES, 1), 0)).astype(F32)
        col = lax.broadcasted_iota(jnp.int32, (GRID_W, 1), 0).astype(F32)
        er = r * freqs
        ec = col * freqs
        sin_r, cos_r, sin_c, cos_c = jnp.sin(er), jnp.cos(er), jnp.sin(ec), jnp.cos(ec)
        for q in range(grid_rows):
            rows = slice(q * GRID_W, (q + 1) * GRID_W)
            o_ref[rows, 0 * quarter:1 * quarter] = x_ref[rows, 0 * quarter:1 * quarter] + sin_r[q:q + 1, :]
            o_ref[rows, 1 * quarter:2 * quarter] = x_ref[rows, 1 * quarter:2 * quarter] + cos_r[q:q + 1, :]
            o_ref[rows, 2 * quarter:3 * quarter] = x_ref[rows, 2 * quarter:3 * quarter] + sin_c
            o_ref[rows, 3 * quarter:4 * quarter] = x_ref[rows, 3 * quarter:4 * quarter] + cos_c


def _embed_call(x, ctx):
    lat_tiles_b = SEQ // TM
    return pl.pallas_call(
        _embed_kernel,
        grid=(NT,),
        in_specs=[
            pl.BlockSpec((TM, D_MODEL), lambda i: ((i // NT_B) * lat_tiles_b + jnp.maximum(i % NT_B - 1, 0), 0)),
            pl.BlockSpec((TM, D_MODEL), lambda i: (i // NT_B, 0)),
        ],
        out_specs=pl.BlockSpec((TM, D_MODEL), lambda i: (i, 0)),
        out_shape=jax.ShapeDtypeStruct((T_ALL, D_MODEL), F32),
        compiler_params=_params(("parallel",)),
        name="embed",
    )(x.reshape(BATCH * SEQ, D_MODEL), ctx.reshape(BATCH * CTX_LEN, D_MODEL))


def _ffn_kernel(*refs, sub):
    x_refs = refs[:FFN_TILES]
    mod_refs = refs[FFN_TILES:2 * FFN_TILES]
    win_ref, wout_ref, g_ref, b_ref, o_ref, xm_ref, h_ref = refs[2 * FFN_TILES:]
    for t, (x_ref, mod_ref) in enumerate(zip(x_refs, mod_refs)):
        shift = mod_ref[3 * sub + 0:3 * sub + 1, :]
        scale = mod_ref[3 * sub + 1:3 * sub + 2, :]
        xm_ref[t * TM:(t + 1) * TM, :] = (x_ref[...] * (1.0 + scale) + shift).astype(BF16)
    xm = xm_ref[...]
    for c in range(D_FF // FF_CH):
        a1 = _dot(xm, win_ref[:, c * FF_CH:(c + 1) * FF_CH])
        a2 = _dot(xm, win_ref[:, D_FF + c * FF_CH:D_FF + (c + 1) * FF_CH])
        h_ref[:, c * FF_CH:(c + 1) * FF_CH] = (_silu(a1) * a2).astype(BF16)
    for t, (x_ref, mod_ref) in enumerate(zip(x_refs, mod_refs)):
        hid = h_ref[t * TM:(t + 1) * TM, :]
        y = jnp.concatenate([_dot(hid, wout_ref[:, n * MXU_N:(n + 1) * MXU_N]) for n in range(D_MODEL // MXU_N)],
                            axis=1)
        gate = mod_ref[3 * sub + 2:3 * sub + 3, :]
        r = ALPHA * x_ref[...] + (FFN_RES * gate) * y
        o_ref[t * TM:(t + 1) * TM, :] = _ln(r, g_ref[...], b_ref[...])


def _ffn_call(s, mods, ffn_w_in, ffn_w_out, ln_g, ln_b, *, layer, which, final):
    sub = 2 * which
    if final:
        lat_tiles_b = SEQ // TM
        out_tiles = BATCH * lat_tiles_b
        in_tile = lambda t: (t // lat_tiles_b) * NT_B + 1 + t % lat_tiles_b
    else:
        out_tiles = NT
        in_tile = lambda t: t
    tile_of = lambda i, t: in_tile(i * FFN_TILES + t)
    x_specs = [pl.BlockSpec((TM, D_MODEL), functools.partial(lambda i, t: (tile_of(i, t), 0), t=t))
               for t in range(FFN_TILES)]
    mod_specs = [pl.BlockSpec((None, 9, D_MODEL),
                              functools.partial(lambda i, t: (layer * MOD_ROWS + _mod_row(tile_of(i, t)), 0, 0), t=t))
                 for t in range(FFN_TILES)]
    return pl.pallas_call(
        functools.partial(_ffn_kernel, sub=sub),
        grid=(out_tiles // FFN_TILES,),
        in_specs=x_specs + mod_specs + [
            _const_spec((None, None, D_MODEL, 2 * D_FF), (layer, which, 0, 0)),
            _const_spec((None, None, D_FF, D_MODEL + LANE_PAD), (layer, which, 0, 0)),
            _const_spec((None, 1, D_MODEL), (layer * 3 + sub, 0, 0)),
            _const_spec((None, 1, D_MODEL), (layer * 3 + sub, 0, 0)),
        ],
        out_specs=pl.BlockSpec((FFN_TILES * TM, D_MODEL), lambda i: (i, 0)),
        out_shape=jax.ShapeDtypeStruct((out_tiles * TM, D_MODEL), F32),
        scratch_shapes=[
            pltpu.VMEM((FFN_TILES * TM, D_MODEL), BF16),
            pltpu.VMEM((FFN_TILES * TM, D_FF), BF16),
        ],
        compiler_params=_params(("parallel",)),
        name=f"ffn{which}",
    )(*([s] * FFN_TILES), *([mods] * FFN_TILES), ffn_w_in, ffn_w_out, ln_g, ln_b)


def _cproj_kernel(x_ref, prev_ref, next_ref, mod_ref, w_ref, b_ref, cw_ref,
                  q_ref, kt_ref, v_ref, g_ref, gt_ref, xe_ref, p_ref):
    j = pl.program_id(0) % NT_B
    first = jnp.logical_or(j == 0, j == 1)
    last = jnp.logical_or(j == 0, j == NT_B - 1)
    shift = mod_ref[3:4, :]
    scale = mod_ref[4:5, :]

    def modulate(v):
        return (v * (1.0 + scale) + shift).astype(BF16)

    xe_ref[0:HALO, :] = modulate(prev_ref[...])
    xe_ref[HALO:HALO + TM, :] = modulate(x_ref[...])
    xe_ref[HALO + TM:2 * HALO + TM, :] = modulate(next_ref[...])
    xe = xe_ref[...]
    rows = lax.broadcasted_iota(jnp.int32, (TM + 2 * HALO, 1), 0)
    valid = jnp.logical_and(jnp.logical_or(rows >= HALO, jnp.logical_not(first)),
                            jnp.logical_or(rows < HALO + TM, jnp.logical_not(last)))
    p_ref[...] = jnp.where(valid, _dot(xe, w_ref[:, CP_QK:CP_V]) + b_ref[:, CP_QK:CP_V], 0.0)
    conv = (cw_ref[0:1, :] * p_ref[HALO - 1:HALO - 1 + TM, :]
            + cw_ref[1:2, :] * p_ref[HALO:HALO + TM, :]
            + cw_ref[2:3, :] * p_ref[HALO + 1:HALO + 1 + TM, :])
    qk = _silu(conv)
    q_ref[...] = qk[:, :MIX_W].astype(BF16)
    k = qk[:, MIX_W:] * MLSTM_DH ** -0.5
    for c in range(CH_T):
        for h in range(MLSTM_HEADS):
            blk = k[c * CHUNK:(c + 1) * CHUNK, h * MLSTM_DH:(h + 1) * MLSTM_DH]
            r0 = (c * MLSTM_HEADS + h) * MLSTM_DH
            kt_ref[r0:r0 + MLSTM_DH, :] = blk.T.astype(BF16)

    xm = xe_ref[HALO:HALO + TM, :]
    pvg = _dot(xm, w_ref[:, CP_V:CP_COLS]) + b_ref[:, CP_V:CP_COLS]
    v_ref[...] = pvg[:, :MIX_W].astype(BF16)
    li = pvg[:, CP_GI - CP_V:CP_GF - CP_V]
    fraw = pvg[:, CP_GF - CP_V:]
    lf = jnp.minimum(fraw, 0.0) - jnp.log1p(jnp.exp(-jnp.abs(fraw)))
    row = lax.broadcasted_iota(jnp.int32, (CHUNK, LANES), 0)
    lane = lax.broadcasted_iota(jnp.int32, (CHUNK, LANES), 1)
    fwd_lane = lane < MLSTM_HEADS
    lower_f = (lane <= row).astype(F32)
    for c in range(CH_T):
        tok = slice(c * CHUNK, (c + 1) * CHUNK)
        prefix = _dot_f32(lower_f, lf[tok])
        suffix = prefix[CHUNK - 1:CHUNK, :] - prefix + lf[tok]
        cum = jnp.where(fwd_lane, prefix, suffix)
        r = li[tok] - cum
        rf, rb = r, r
        k = 1
        while k < CHUNK:
            rf = jnp.maximum(rf, jnp.where(row >= k, pltpu.roll(rf, k, 0), -jnp.inf))
            rb = jnp.maximum(rb, jnp.where(row < CHUNK - k, pltpu.roll(rb, CHUNK - k, 0), -jnp.inf))
            k *= 2
        rmax = jnp.where(fwd_lane, rf, rb)
        g_ref[tok, :] = jnp.where(lane < GS_RMAX, cum, pltpu.roll(rmax, GS_RMAX, 1))
        gt_ref[c * SUBLANES:(c + 1) * SUBLANES, :] = r.T[:SUBLANES, :]


def _cproj_call(s, mods, w_cp, b_cp, qk_conv_w, *, layer):
    hb = TM // HALO
    n_hb = T_ALL // HALO
    return pl.pallas_call(
        _cproj_kernel,
        grid=(NT,),
        in_specs=[
            pl.BlockSpec((TM, D_MODEL), lambda i: (i, 0)),
            pl.BlockSpec((HALO, D_MODEL), lambda i: (jnp.maximum(i * hb - 1, 0), 0)),
            pl.BlockSpec((HALO, D_MODEL), lambda i: (jnp.minimum((i + 1) * hb, n_hb - 1), 0)),
            pl.BlockSpec((None, 9, D_MODEL), lambda i: (layer * MOD_ROWS + _mod_row(i), 0, 0)),
            _const_spec((None, D_MODEL, CP_COLS), (layer, 0, 0)),
            _const_spec((None, 1, CP_COLS), (layer, 0, 0)),
            _const_spec((None, QK_CONV, 2 * MIX_W), (layer, 0, 0)),
        ],
        out_specs=[
            pl.BlockSpec((TM, MIX_W), lambda i: (i, 0)),
            pl.BlockSpec((TM * MLSTM_HEADS, MLSTM_DH), lambda i: (i, 0)),
            pl.BlockSpec((TM, MIX_W), lambda i: (i, 0)),
            pl.BlockSpec((TM, LANES), lambda i: (i, 0)),
            pl.BlockSpec((CH_T * SUBLANES, CHUNK), lambda i: (i, 0)),
        ],
        out_shape=[
            jax.ShapeDtypeStruct((T_ALL, MIX_W), BF16),
            jax.ShapeDtypeStruct((T_ALL * MLSTM_HEADS, MLSTM_DH), BF16),
            jax.ShapeDtypeStruct((T_ALL, MIX_W), BF16),
            jax.ShapeDtypeStruct((T_ALL, LANES), F32),
            jax.ShapeDtypeStruct((T_ALL // CHUNK * SUBLANES, CHUNK), F32),
        ],
        scratch_shapes=[
            pltpu.VMEM((TM + 2 * HALO, D_MODEL), BF16),
            pltpu.VMEM((TM + 2 * HALO, 2 * MIX_W), F32),
        ],
        compiler_params=_params(("parallel",)),
        name="cproj",
    )(s, s, s, mods, w_cp, b_cp, qk_conv_w)


def _mlstm_kernel(qf_ref, ktf_ref, vf_ref, gf_ref, gtf_ref, qb_ref, ktb_ref, vb_ref, gb_ref, gtb_ref,
                  hf_ref, hb_ref, c_ref, m_ref):
    @pl.when(pl.program_id(0) == 0)
    def _():
        c_ref[...] = jnp.zeros(c_ref.shape, F32)
        m_ref[...] = jnp.zeros(m_ref.shape, F32)

    row = lax.broadcasted_iota(jnp.int32, (CHUNK, CHUNK), 0)
    col = lax.broadcasted_iota(jnp.int32, (CHUNK, CHUNK), 1)
    ones_col = jnp.ones((CHUNK, MLSTM_DH), BF16)
    nh = MLSTM_HEADS
    dirs = ((qf_ref, ktf_ref, vf_ref, gf_ref, gtf_ref, hf_ref, col <= row, CHUNK - 1),
            (qb_ref, ktb_ref, vb_ref, gb_ref, gtb_ref, hb_ref, col >= row, 0))
    heads = []
    for d, (q_ref, kt_ref, v_ref, g_ref, gt_ref, h_ref, seen, end_row) in enumerate(dirs):
        for b in range(BATCH):
            g = g_ref[b]
            gt = gt_ref[b]
            for h in range(nh):
                lane = d * nh + h
                hd = slice(h * MLSTM_DH, (h + 1) * MLSTM_DH)
                heads.append(dict(
                    idx=(d * BATCH + b) * nh + h, seen=seen, end_row=end_row, out=(h_ref, b, hd),
                    cum=g[:, lane:lane + 1], rmax=g[:, GS_RMAX + lane:GS_RMAX + lane + 1], r=gt[lane:lane + 1, :],
                    q=q_ref[b, :, hd], kt=kt_ref[b, hd, :], v=v_ref[b, :, hd]))

    for hd in heads:
        hd["qk"] = _dot(hd["q"], hd["kt"])
        hd["m_prev"] = m_ref[hd["idx"], 0:1, 0:1]
        hd["c_old"] = c_ref[hd["idx"]]
        hd["v_aug"] = jnp.concatenate([hd["v"], ones_col], axis=1)
    for hd in heads:
        e = hd["end_row"]
        top = jnp.maximum(hd["m_prev"], hd["rmax"][e:e + 1, :])
        w_row = jnp.exp(hd["r"] - top)
        ktw = (hd["kt"].astype(F32) * w_row).astype(BF16)
        c_ref[hd["idx"]] = jnp.exp(hd["m_prev"] - top) * hd["c_old"] + _dot(ktw, hd["v_aug"])
        m_ref[hd["idx"]] = jnp.broadcast_to(hd["cum"][e:e + 1, :] + top, m_ref.shape[1:])
    for hd in heads:
        m_prev = hd["m_prev"]
        top = jnp.broadcast_to(jnp.maximum(m_prev, hd["rmax"]), (CHUNK, CHUNK))
        s = hd["qk"] * jnp.exp(jnp.where(hd["seen"], hd["r"] - top, -jnp.inf))
        qw = hd["q"].astype(F32) * jnp.exp(m_prev - top)
        lhs = jnp.concatenate([s.astype(BF16), qw.astype(BF16)], axis=1)
        rhs = jnp.concatenate([hd["v_aug"], hd["c_old"].astype(BF16)], axis=0)
        hd["na"] = _dot(lhs, rhs)
        hd["floor"] = jnp.exp(-(hd["cum"] + top))
    for hd in heads:
        den = jnp.maximum(jnp.abs(hd["na"][:, MLSTM_DH:]), hd["floor"])
        h_ref, b, cols = hd["out"]
        h_ref[b, :, cols] = hd["na"][:, :MLSTM_DH] / den


def _mlstm_call(q, kt, v, g, gt):
    fwd = lambda s: s
    bwd = lambda s: jnp.where(s < CTX_CH, CTX_CH - 1 - s, NCH_B + CTX_CH - 1 - s)
    q3 = q.reshape(BATCH, SEQ_B, MIX_W)
    kt3 = kt.reshape(BATCH, SEQ_B * MLSTM_HEADS, MLSTM_DH)
    v3 = v.reshape(BATCH, SEQ_B, MIX_W)
    g3 = g.reshape(BATCH, SEQ_B, LANES)
    gt3 = gt.reshape(BATCH, NCH_B * SUBLANES, CHUNK)

    def specs(order):
        return [
            pl.BlockSpec((BATCH, CHUNK, MIX_W), lambda s: (0, order(s), 0)),
            pl.BlockSpec((BATCH, CHUNK * MLSTM_HEADS, MLSTM_DH), lambda s: (0, order(s), 0)),
            pl.BlockSpec((BATCH, CHUNK, MIX_W), lambda s: (0, order(s), 0)),
            pl.BlockSpec((BATCH, CHUNK, LANES), lambda s: (0, order(s), 0)),
            pl.BlockSpec((BATCH, SUBLANES, CHUNK), lambda s: (0, order(s), 0)),
        ]

    n_state = 2 * BATCH * MLSTM_HEADS
    hf, hb = pl.pallas_call(
        _mlstm_kernel,
        grid=(NCH_B,),
        in_specs=specs(fwd) + specs(bwd),
        out_specs=[
            pl.BlockSpec((BATCH, CHUNK, MIX_W), lambda s: (0, fwd(s), 0)),
            pl.BlockSpec((BATCH, CHUNK, MIX_W), lambda s: (0, bwd(s), 0)),
        ],
        out_shape=[jax.ShapeDtypeStruct((BATCH, SEQ_B, MIX_W), F32)] * 2,
        scratch_shapes=[
            pltpu.VMEM((n_state, MLSTM_DH, 2 * MLSTM_DH), F32),
            pltpu.VMEM((n_state, SUBLANES, LANES), F32),
        ],
        compiler_params=_params(("arbitrary",)),
        name="mlstm",
    )(q3, kt3, v3, g3, gt3, q3, kt3, v3, g3, gt3)
    return hf.reshape(T_ALL, MIX_W), hb.reshape(T_ALL, MIX_W)


VEC_GMLP_G, VEC_GMLP_B, VEC_CONV_B, VEC_CONV_G, VEC_CONV_LB, VEC_MLSTM_G, VEC_POOL_S = range(7)
VEC_ROWS = 8


def _mix_kernel(x_ref, prev_ref, next_ref, mod_ref, hf_ref, hb_ref, w_ref, b_ref, vec_ref, ws_ref, bst_ref,
                cw_ref, pw_ref, wbr_ref, wout_ref, g_ref, beta_ref, o_ref, xe_ref, a_ref, d_ref, ash_ref, conv_ref,
                gate_ref, pa_ref, po_ref, u_ref, vn_ref, yc_ref, yd_ref):
    j = pl.program_id(0) % NT_B
    is_ctx = j == 0
    first = jnp.logical_or(is_ctx, j == 1)
    last = jnp.logical_or(is_ctx, j == NT_B - 1)
    shift = mod_ref[3:4, :]
    scale = mod_ref[4:5, :]
    gate = mod_ref[5:6, :]

    def modulate(v):
        return (v * (1.0 + scale) + shift).astype(BF16)

    def vec(r):
        return vec_ref[r:r + 1, :]

    def proj(ext, lo, width, zero=None):
        lhs = xe_ref[...] if ext else xe_ref[HALO:HALO + TM, :]
        bias = lambda c: b_ref[:, c:c + MXU_N] if zero is None else b_ref[:, c:c + MXU_N] + zero
        return jnp.concatenate([_dot(lhs, w_ref[:, c:c + MXU_N]) + bias(c)
                                for c in range(lo, lo + width, MXU_N)], axis=1)

    xe_ref[0:HALO, :] = modulate(prev_ref[...])
    xe_ref[HALO:HALO + TM, :] = modulate(x_ref[...])
    xe_ref[HALO + TM:2 * HALO + TM, :] = modulate(next_ref[...])
    rows = lax.broadcasted_iota(jnp.int32, (TM + 2 * HALO, 1), 0)
    valid = jnp.logical_and(jnp.logical_or(rows >= HALO, jnp.logical_not(first)),
                            jnp.logical_or(rows < HALO + TM, jnp.logical_not(last)))

    pb = proj(True, MX_B, 2 * MIX_W)
    a_ref[...] = jnp.where(valid, pb[:, :MIX_W] * _sigmoid(pb[:, MIX_W:]), 0.0)
    pa_ref[...] = proj(False, MX_A, 2 * MIX_W)

    gate_chunks = [(i, n) for i in range(N_BRANCH) for n in range(D_MODEL // MXU_N)]
    sh_rows = TM + 2 * HALO - SUBLANES
    for s in range(1, SUBLANES):
        ash_ref[s - 1, 0:sh_rows, :] = a_ref[s:s + sh_rows, :]

    def zero_of(v):
        bits = lax.bitcast_convert_type(v[0:1, 0:LANES], jnp.uint32)
        return lax.shift_right_logical(lax.shift_right_logical(bits, jnp.uint32(16)), jnp.uint32(16)).astype(F32)

    def conv_block(cb, rb, z):
        cols = slice(cb * LANES, (cb + 1) * LANES)
        part = None
        for k in range(CONV_W):
            q8, s = divmod(HALO - CONV_W // 2 + k, SUBLANES)
            r0 = q8 * SUBLANES + rb * CONV_RB
            src = a_ref[r0:r0 + CONV_RB, cols] if s == 0 else ash_ref[s - 1, r0:r0 + CONV_RB, cols]
            term = (cw_ref[k:k + 1, cols] + z) * src
            part = term if part is None else part + term
        conv_ref[rb * CONV_RB:(rb + 1) * CONV_RB, cols] = part
        return part

    conv_blocks = [(cb, rb) for cb in range(MIX_W // LANES) for rb in range(TM // CONV_RB)]
    per_step = len(conv_blocks) // len(gate_chunks)
    zero_row = jnp.zeros((1, LANES), F32)
    v_zero = [zero_row] * len(gate_chunks)
    for step, (i, n) in enumerate(gate_chunks):
        lo = i * D_MODEL + n * MXU_N
        z_in = jnp.concatenate([v_zero[step - CHAIN_LAG]] * (MXU_N // LANES), axis=1) if step >= CHAIN_LAG else None
        gv = jnp.tanh(0.5 * proj(False, MX_G + lo, MXU_N, z_in))
        gate_ref[:, lo:lo + MXU_N] = gv
        z = zero_of(gv) + v_zero[step - 1]
        for cb, rb in conv_blocks[step * per_step:(step + 1) * per_step]:
            z = zero_of(conv_block(cb, rb, z))
        v_zero[step] = z

    pa = _gelu_tanh(pa_ref[...])
    u_ref[...] = pa[:, :MIX_W]
    vn_ref[...] = _ln(pa[:, MIX_W:], vec(VEC_GMLP_G), vec(VEC_GMLP_B)).astype(BF16)

    d_ref[0:TM + 2 * HALO, :] = jnp.where(valid, proj(True, MX_D, MIX_W), 0.0)
    d_ref[TM + 2 * HALO:TM + 2 * HALO + SUBLANES, :] = jnp.zeros((SUBLANES, MIX_W), F32)
    pos = jnp.where(is_ctx, 0, (j - 1) * TM) + lax.broadcasted_iota(jnp.int32, (TM, 1), 0)
    n_seq = jnp.where(is_ctx, CTX_LEN, SEQ)

    def window_sum(cols, win):
        start = HALO - win // 2
        if win < SUBLANES:
            ws = d_ref[start:start + TM, cols]
            for k in range(1, win):
                ws = ws + d_ref[start + k:start + k + TM, cols]
            return ws
        base = start // SUBLANES * SUBLANES
        off = start - base
        up8 = lambda r: -(-r // SUBLANES) * SUBLANES
        rows = [up8(TM + off)]
        n = win
        while n > 1:
            n //= 2
            rows.append(up8(rows[-1] + n))
        rows.reverse()
        level = d_ref[base:base + rows[1], cols] + d_ref[base + 1:base + 1 + rows[1], cols]
        n, i = 2, 1
        while n < win:
            i += 1
            level = level[0:rows[i], :] + level[n:n + rows[i], :]
            n *= 2
        return level[off:off + TM, :]

    for gi, win in enumerate(POOL_WINDOWS):
        lo, hi = win // 2, win - 1 - win // 2
        cols = slice(gi * POOL_GD, (gi + 1) * POOL_GD)
        wsum = window_sum(cols, win)
        cnt = (jnp.minimum(pos + hi + 1, n_seq) - jnp.maximum(pos - lo, 0)).astype(F32)
        diff = wsum / cnt - d_ref[HALO:HALO + TM, cols]
        yd_ref[:, cols] = (_dot(diff.astype(BF16), pw_ref[gi]) * vec_ref[VEC_POOL_S:VEC_POOL_S + 1, cols]
                           ).astype(BF16)

    po_ref[...] = proj(False, MX_O, MIX_W)
    for h in range(MLSTM_HEADS):
        cols = slice(h * MLSTM_DH, (h + 1) * MLSTM_DH)
        hh = hf_ref[:, cols] + hb_ref[:, cols]
        mu = jnp.mean(hh, axis=-1, keepdims=True)
        hc = hh - mu
        hn = hc * lax.rsqrt(jnp.mean(hc * hc, axis=-1, keepdims=True) + LN_EPS)
        yc_ref[:, cols] = (_sigmoid(po_ref[:, cols]) * (hn * vec_ref[VEC_MLSTM_G:VEC_MLSTM_G + 1, cols])
                           ).astype(BF16)

    gd = MIX_W // GMLP_GROUPS
    z_rows = []
    for c in range(CH_T):
        z_cols = []
        for gi in range(GMLP_GROUPS):
            blk = vn_ref[c * CHUNK:(c + 1) * CHUNK, gi * gd:(gi + 1) * gd]
            z_cols.append(_dot(ws_ref[gi], blk) + bst_ref[:, gi:gi + 1])
        z_rows.append(jnp.concatenate(z_cols, axis=1))
    ya = (u_ref[...] * jnp.concatenate(z_rows, axis=0)).astype(BF16)
    yb = _silu(_ln(conv_ref[...] + vec(VEC_CONV_B), vec(VEC_CONV_G), vec(VEC_CONV_LB))).astype(BF16)
    yc = yc_ref[...]
    yd = yd_ref[...]

    merged = []
    for n in range(D_MODEL // MXU_N):
        cols = slice(n * MXU_N, (n + 1) * MXU_N)
        acc = None
        for i, yi in enumerate((ya, yb, yc, yd)):
            lo = i * D_MODEL + n * MXU_N
            p = _dot(yi, wbr_ref[i, :, cols])
            term = gate_ref[:, lo:lo + MXU_N] * p + p
            acc = term if acc is None else acc + term
        merged.append((0.5 * acc).astype(BF16))
    merged = jnp.concatenate(merged, axis=1)
    y = jnp.concatenate([_dot(merged, wout_ref[:, n * MXU_N:(n + 1) * MXU_N]) for n in range(D_MODEL // MXU_N)],
                        axis=1)
    o_ref[...] = _ln(ALPHA * x_ref[...] + gate * y, g_ref[...], beta_ref[...])


def _mix_call(s, mods, hf, hb, w_mx, b_mx, vec512, gmlp_ws, gmlp_bst, conv_w, pool_w, w_branch, w_out,
              ln_g, ln_b, *, layer):
    hb_per_tile = TM // HALO
    n_hb = T_ALL // HALO
    return pl.pallas_call(
        _mix_kernel,
        grid=(NT,),
        in_specs=[
            pl.BlockSpec((TM, D_MODEL), lambda i: (i, 0)),
            pl.BlockSpec((HALO, D_MODEL), lambda i: (jnp.maximum(i * hb_per_tile - 1, 0), 0)),
            pl.BlockSpec((HALO, D_MODEL), lambda i: (jnp.minimum((i + 1) * hb_per_tile, n_hb - 1), 0)),
            pl.BlockSpec((None, 9, D_MODEL), lambda i: (layer * MOD_ROWS + _mod_row(i), 0, 0)),
            pl.BlockSpec((TM, MIX_W), lambda i: (i, 0)),
            pl.BlockSpec((TM, MIX_W), lambda i: (i, 0)),
            _const_spec((None, D_MODEL, MX_W), (layer, 0, 0)),
            _const_spec((None, 1, MX_COLS), (layer, 0, 0)),
            _const_spec((None, VEC_ROWS, MIX_W), (layer, 0, 0)),
            _const_spec((None, GMLP_GROUPS, CHUNK, CHUNK), (layer, 0, 0, 0)),
            _const_spec((None, CHUNK, GMLP_GROUPS), (layer, 0, 0)),
            _const_spec((None, CONV_W, MIX_W), (layer, 0, 0)),
            _const_spec((None, len(POOL_WINDOWS), POOL_GD, POOL_GD), (layer, 0, 0, 0)),
            _const_spec((None, N_BRANCH, MIX_W, D_MODEL + LANE_PAD), (layer, 0, 0, 0)),
            _const_spec((None, D_MODEL, D_MODEL + LANE_PAD), (layer, 0, 0)),
            _const_spec((None, 1, D_MODEL), (layer * 3 + 1, 0, 0)),
            _const_spec((None, 1, D_MODEL), (layer * 3 + 1, 0, 0)),
        ],
        out_specs=pl.BlockSpec((TM, D_MODEL), lambda i: (i, 0)),
        out_shape=jax.ShapeDtypeStruct((T_ALL, D_MODEL), F32),
        scratch_shapes=[
            pltpu.VMEM((TM + 2 * HALO, D_MODEL), BF16),
            pltpu.VMEM((TM + 2 * HALO, MIX_W), F32),
            pltpu.VMEM((TM + 2 * HALO + SUBLANES, MIX_W), F32),
            pltpu.VMEM((SUBLANES - 1, TM + 2 * HALO, MIX_W), F32),
            pltpu.VMEM((TM, MIX_W), F32),
            pltpu.VMEM((TM, N_BRANCH * D_MODEL), F32),
            pltpu.VMEM((TM, 2 * MIX_W), F32),
            pltpu.VMEM((TM, MIX_W), F32),
            pltpu.VMEM((TM, MIX_W), F32),
            pltpu.VMEM((TM, MIX_W), BF16),
            pltpu.VMEM((TM, MIX_W), BF16),
            pltpu.VMEM((TM, MIX_W), BF16),
        ],
        compiler_params=_params(("parallel",)),
        name="mix",
    )(s, s, s, mods, hf, hb, w_mx, b_mx, vec512, gmlp_ws, gmlp_bst, conv_w, pool_w, w_branch, w_out, ln_g, ln_b)


MX_TILE = 512
MX_PAD = MX_TILE
MX_W = MX_COLS + MX_PAD
CP_TILE = 256


def _mx_src(j):
    a, o, d = (MX_O - MX_A) // MX_TILE, (MX_D - MX_A) // MX_TILE, (MX_G - MX_A) // MX_TILE
    g_end = MX_COLS // MX_TILE
    return jnp.where(j < a, OFF_A + MX_TILE * j,
                     jnp.where(j < o, OFF_C_O + MX_TILE * (j - a),
                               jnp.where(j < d, OFF_D + MX_TILE * (j - o),
                                         jnp.where(j < g_end, OFF_G + MX_TILE * (j - d), 0))))


def _prep_mx_kernel(wt_ref, o_ref):
    j = pl.program_id(1)

    @pl.when(j < MX_COLS // MX_TILE)
    def _():
        o_ref[...] = wt_ref[0].T.astype(BF16)

    @pl.when(j >= MX_COLS // MX_TILE)
    def _():
        o_ref[...] = jnp.zeros(o_ref.shape, BF16)


def _prep_cp_kernel(wt_ref, o_ref):
    j = pl.program_id(1)
    n_plain = (CP_GI - CP_QK) // CP_TILE

    @pl.when(j < n_plain)
    def _():
        o_ref[...] = wt_ref[0].T.astype(BF16)

    @pl.when(j == n_plain)
    def _():
        nh = MLSTM_HEADS
        t = wt_ref[0, 0:LANES, :].T
        lane = lax.broadcasted_iota(jnp.int32, t.shape, 1)
        up1 = pltpu.roll(t, LANES - nh, 1)
        up2 = pltpu.roll(t, LANES - 2 * nh, 1)
        gi = jnp.where(lane < nh, t, jnp.where(lane < 2 * nh, up1, 0.0))
        gf = jnp.where(lane < nh, up1, jnp.where(lane < 2 * nh, up2, 0.0))
        o_ref[:, 0:LANES] = gi.astype(BF16)
        o_ref[:, LANES:2 * LANES] = gf.astype(BF16)


def _prep_win_call(w_in):
    wt = jnp.swapaxes(w_in, 1, 2)
    n_plain = (CP_GI - CP_QK) // CP_TILE
    w_mx = pl.pallas_call(
        _prep_mx_kernel,
        grid=(DEPTH, MX_W // MX_TILE),
        in_specs=[pl.BlockSpec((pl.Element(1), pl.Element(MX_TILE), pl.Element(D_MODEL)),
                               lambda l, j: (l, pl.multiple_of(_mx_src(j), SUBLANES), 0))],
        out_specs=pl.BlockSpec((None, D_MODEL, MX_TILE), lambda l, j: (l, 0, j)),
        out_shape=jax.ShapeDtypeStruct((DEPTH, D_MODEL, MX_W), BF16),
        compiler_params=_params(("parallel", "parallel")),
        name="prep_mx",
    )(wt)
    w_cp = pl.pallas_call(
        _prep_cp_kernel,
        grid=(DEPTH, CP_COLS // CP_TILE),
        in_specs=[pl.BlockSpec((pl.Element(1), pl.Element(CP_TILE), pl.Element(D_MODEL)),
                               lambda l, j: (l, pl.multiple_of(
                                   jnp.where(j < n_plain, OFF_C + CP_TILE * j, OFF_C_GATES), SUBLANES), 0))],
        out_specs=pl.BlockSpec((None, D_MODEL, CP_TILE), lambda l, j: (l, 0, j)),
        out_shape=jax.ShapeDtypeStruct((DEPTH, D_MODEL, CP_COLS), BF16),
        compiler_params=_params(("parallel", "parallel")),
        name="prep_cp",
    )(wt)
    return w_mx, w_cp


def _pad_cast_kernel(w_ref, o_ref):
    cols = w_ref.shape[1]
    o_ref[:, 0:cols] = w_ref[...].astype(BF16)
    o_ref[:, cols:] = jnp.zeros((o_ref.shape[0], o_ref.shape[1] - cols), BF16)


def _pad_cast_call(w):
    lead, cols = w.shape[:-1], w.shape[-1]
    rows = math.prod(lead)
    rb = 512
    out = pl.pallas_call(
        _pad_cast_kernel,
        grid=(rows // rb,),
        in_specs=[pl.BlockSpec((rb, cols), lambda i: (i, 0))],
        out_specs=pl.BlockSpec((rb, cols + LANE_PAD), lambda i: (i, 0)),
        out_shape=jax.ShapeDtypeStruct((rows, cols + LANE_PAD), BF16),
        compiler_params=_params(("parallel",)),
        name="pad_cast",
    )(w.reshape(rows, cols))
    return out.reshape(lead + (cols + LANE_PAD,))


def _repack_cols(w, pieces):
    parts = [jnp.zeros(w.shape[:-1] + (p,), w.dtype) if isinstance(p, int) else w[..., p[0]:p[1]] for p in pieces]
    return jnp.concatenate(parts, axis=-1)


def kernel(x, c, ctx, c_ctx, w_ada, b_ada, ln_g, ln_b, ffn_w_in, ffn_w_out, w_in, b_in, gmlp_ln_g, gmlp_ln_b,
           gmlp_ws, gmlp_bs, conv_w, conv_b, conv_ln_g, conv_ln_b, qk_conv_w, mlstm_ln_g, pool_w, pool_scale,
           w_branch, w_out):
    mix_pieces = [(OFF_A, OFF_C), (OFF_C_O, OFF_D), (OFF_D, OFF_G), (OFF_G, IN_COLS)]
    nh = MLSTM_HEADS
    gate_pad = LANES - 2 * nh

    def cproj_cols(w):
        g = w[..., OFF_C_GATES:OFF_C_O].reshape(w.shape[:-1] + (4, nh))
        pad = jnp.zeros(w.shape[:-1] + (gate_pad,), w.dtype)
        return jnp.concatenate([w[..., OFF_C:OFF_C_GATES], g[..., 0, :], g[..., 2, :], pad,
                                g[..., 1, :], g[..., 3, :], pad], axis=-1)
    w_mx, w_cp = _prep_win_call(w_in)
    b_mx = _repack_cols(b_in, mix_pieces).reshape(DEPTH, 1, MX_COLS)
    b_cp = cproj_cols(b_in).reshape(DEPTH, 1, CP_COLS)
    ffn_w_in_b = ffn_w_in.astype(BF16)
    ffn_w_out_b = _pad_cast_call(ffn_w_out)
    w_branch_b = _pad_cast_call(w_branch)
    w_out_b = _pad_cast_call(w_out)
    gmlp_ws_b = gmlp_ws.astype(BF16)
    pool_w_b = pool_w.astype(BF16)
    gmlp_bst = jnp.swapaxes(gmlp_bs, 1, 2)
    vec512 = jnp.stack([gmlp_ln_g, gmlp_ln_b, conv_b, conv_ln_g, conv_ln_b, mlstm_ln_g, pool_scale,
                        jnp.zeros_like(pool_scale)], axis=1)
    ln_g3 = ln_g.reshape(DEPTH * 3, 1, D_MODEL)
    ln_b3 = ln_b.reshape(DEPTH * 3, 1, D_MODEL)
    c_t = jnp.concatenate([c, c_ctx[None], jnp.zeros((MOD_ROWS - BATCH - 1, D_MODEL), F32)], axis=0).T

    mods = _mods_call(c_t, w_ada, b_ada).reshape(DEPTH * MOD_ROWS, 9, D_MODEL)
    s = _embed_call(x, ctx)
    for l in range(DEPTH):
        last = l == DEPTH - 1
        s = _ffn_call(s, mods, ffn_w_in_b, ffn_w_out_b, ln_g3, ln_b3, layer=l, which=0, final=False)
        q, kt, v, g, gt = _cproj_call(s, mods, w_cp, b_cp, qk_conv_w, layer=l)
        hf, hb = _mlstm_call(q, kt, v, g, gt)
        s = _mix_call(s, mods, hf, hb, w_mx, b_mx, vec512, gmlp_ws_b, gmlp_bst, conv_w, pool_w_b, w_branch_b,
                      w_out_b, ln_g3, ln_b3, layer=l)
        s = _ffn_call(s, mods, ffn_w_in_b, ffn_w_out_b, ln_g3, ln_b3, layer=l, which=1, final=last)
    return s.reshape(BATCH, SEQ, D_MODEL)
```

```python
import functools
import math

import jax
import jax.numpy as jnp
from jax import lax
from jax.experimental import pallas as pl
from jax.experimental.pallas import tpu as pltpu

D_MODEL = 1024
BATCH = 2
SEQ = 8192
DEPTH = 4
GRID_W = 64
CTX_LEN = 256
MIX_W = D_MODEL // 2
N_BRANCH = 4
CHUNK = 128
GMLP_GROUPS = 4
CONV_W = 31
MLSTM_HEADS = 4
MLSTM_DH = MIX_W // MLSTM_HEADS
QK_CONV = 3
POOL_WINDOWS = (2, 4, 8, 16)
POOL_GD = MIX_W // len(POOL_WINDOWS)
D_FF = 128 * ((8 * D_MODEL // 3 + 127) // 128)
ALPHA = (2 * DEPTH) ** 0.25
LN_EPS = 1e-6
FFN_RES = 0.5

OFF_A = 0
OFF_B = OFF_A + 2 * MIX_W
OFF_C = OFF_B + 2 * MIX_W
OFF_C_GATES = OFF_C + 3 * MIX_W
OFF_C_O = OFF_C_GATES + 4 * MLSTM_HEADS
OFF_D = OFF_C_O + MIX_W
OFF_G = OFF_D + MIX_W
IN_COLS = OFF_G + N_BRANCH * D_MODEL

LANES = 128
SUBLANES = 8
TM = 256
SEQ_B = CTX_LEN + SEQ
T_ALL = BATCH * SEQ_B
NT_B = SEQ_B // TM
NT = BATCH * NT_B
NCH_B = SEQ_B // CHUNK
CH_T = TM // CHUNK
CTX_CH = CTX_LEN // CHUNK
HALO = 16
MXU_N = 256
LANE_PAD = LANES
FF_CH = MXU_N
FFN_TILES = 2
CONV_RB = 64
CHAIN_LAG = 3
GS_RMAX = 2 * MLSTM_HEADS
MOD_ROWS = 8
CTX_MOD_ROW = BATCH
VMEM_LIMIT = 56 * 1024 * 1024

MX_A = 0
MX_B = MX_A + 2 * MIX_W
MX_O = MX_B + 2 * MIX_W
MX_D = MX_O + MIX_W
MX_G = MX_D + MIX_W
MX_COLS = MX_G + N_BRANCH * D_MODEL
CP_QK = 0
CP_V = 2 * MIX_W
CP_GI = 3 * MIX_W
CP_GF = CP_GI + LANES
CP_COLS = CP_GF + LANES

F32 = jnp.float32
BF16 = jnp.bfloat16


def _dot(a, b):
    return jnp.dot(a, b, preferred_element_type=F32)


def _dot_f32(a, b):
    return jnp.dot(a, b, preferred_element_type=F32, precision=lax.Precision.HIGHEST)


def _ln(r, g, b):
    mu = jnp.mean(r, axis=-1, keepdims=True)
    xc = r - mu
    var = jnp.mean(xc * xc, axis=-1, keepdims=True)
    return xc * lax.rsqrt(var + LN_EPS) * g + b


def _sigmoid(x):
    return 0.5 * jnp.tanh(0.5 * x) + 0.5


def _silu(x):
    hx = 0.5 * x
    return hx * jnp.tanh(hx) + hx


def _gelu_tanh(x):
    c = math.sqrt(2.0 / math.pi)
    hx = 0.5 * x
    return hx * jnp.tanh(x * (c + (c * 0.044715) * (x * x))) + hx


def _mod_row(i):
    return jnp.where(i % NT_B == 0, CTX_MOD_ROW, i // NT_B)


def _const_spec(block, index):
    return pl.BlockSpec(block, lambda *_: index, pipeline_mode=pl.Buffered(1))


def _params(sem):
    return pltpu.CompilerParams(dimension_semantics=sem, vmem_limit_bytes=VMEM_LIMIT)


def _mods_kernel(ct_ref, w_ref, b_ref, o_ref):
    s = _silu(ct_ref[...])
    w = w_ref[...]
    o_ref[...] = jnp.zeros(o_ref.shape, F32)
    for r in range(BATCH + 1):
        o_ref[r:r + 1, :] = jnp.sum(s[:, r:r + 1] * w, axis=0, keepdims=True) + b_ref[...]


def _mods_call(c_t, w_ada, b_ada):
    tn = D_MODEL
    n_col = w_ada.shape[-1] // tn
    return pl.pallas_call(
        _mods_kernel,
        grid=(DEPTH, n_col),
        in_specs=[
            pl.BlockSpec((D_MODEL, MOD_ROWS), lambda l, n: (0, 0)),
            pl.BlockSpec((None, D_MODEL, tn), lambda l, n: (l, 0, n)),
            pl.BlockSpec((None, 1, tn), lambda l, n: (l, 0, n)),
        ],
        out_specs=pl.BlockSpec((None, MOD_ROWS, tn), lambda l, n: (l, 0, n)),
        out_shape=jax.ShapeDtypeStruct((DEPTH, MOD_ROWS, w_ada.shape[-1]), F32),
        compiler_params=_params(("parallel", "parallel")),
        name="mods",
    )(c_t, w_ada, b_ada.reshape(DEPTH, 1, -1))


def _embed_kernel(x_ref, ctx_ref, o_ref):
    j = pl.program_id(0) % NT_B

    @pl.when(j == 0)
    def _():
        o_ref[...] = ctx_ref[...]

    @pl.when(j > 0)
    def _():
        quarter = D_MODEL // 4
        grid_rows = TM // GRID_W
        k = lax.broadcasted_iota(jnp.int32, (1, quarter), 1).astype(F32)
        freqs = jnp.exp(-math.log(10000.0) * k / quarter)
        r = ((j - 1) * grid_rows + lax.broadcasted_iota(jnp.int32, (SUBLANES, 1), 0)).astype(F32)
        col = lax.broadcasted_iota(jnp.int32, (GRID_W, 1), 0).astype(F32)
        er = r * freqs
        ec = col * freqs
        sin_r, cos_r, sin_c, cos_c = jnp.sin(er), jnp.cos(er), jnp.sin(ec), jnp.cos(ec)
        for q in range(grid_rows):
            rows = slice(q * GRID_W, (q + 1) * GRID_W)
            o_ref[rows, 0 * quarter:1 * quarter] = x_ref[rows, 0 * quarter:1 * quarter] + sin_r[q:q + 1, :]
            o_ref[rows, 1 * quarter:2 * quarter] = x_ref[rows, 1 * quarter:2 * quarter] + cos_r[q:q + 1, :]
            o_ref[rows, 2 * quarter:3 * quarter] = x_ref[rows, 2 * quarter:3 * quarter] + sin_c
            o_ref[rows, 3 * quarter:4 * quarter] = x_ref[rows, 3 * quarter:4 * quarter] + cos_c


def _embed_call(x, ctx):
    lat_tiles_b = SEQ // TM
    return pl.pallas_call(
        _embed_kernel,
        grid=(NT,),
        in_specs=[
            pl.BlockSpec((TM, D_MODEL), lambda i: ((i // NT_B) * lat_tiles_b + jnp.maximum(i % NT_B - 1, 0), 0)),
            pl.BlockSpec((TM, D_MODEL), lambda i: (i // NT_B, 0)),
        ],
        out_specs=pl.BlockSpec((TM, D_MODEL), lambda i: (i, 0)),
        out_shape=jax.ShapeDtypeStruct((T_ALL, D_MODEL), F32),
        compiler_params=_params(("parallel",)),
        name="embed",
    )(x.reshape(BATCH * SEQ, D_MODEL), ctx.reshape(BATCH * CTX_LEN, D_MODEL))


def _ffn_kernel(*refs, sub):
    x_refs = refs[:FFN_TILES]
    mod_refs = refs[FFN_TILES:2 * FFN_TILES]
    win_ref, wout_ref, g_ref, b_ref, o_ref, xm_ref, h_ref = refs[2 * FFN_TILES:]
    for t, (x_ref, mod_ref) in enumerate(zip(x_refs, mod_refs)):
        shift = mod_ref[3 * sub + 0:3 * sub + 1, :]
        scale = mod_ref[3 * sub + 1:3 * sub + 2, :]
        xm_ref[t * TM:(t + 1) * TM, :] = (x_ref[...] * (1.0 + scale) + shift).astype(BF16)
    xm = xm_ref[...]
    for c in range(D_FF // FF_CH):
        a1 = _dot(xm, win_ref[:, c * FF_CH:(c + 1) * FF_CH])
        a2 = _dot(xm, win_ref[:, D_FF + c * FF_CH:D_FF + (c + 1) * FF_CH])
        h_ref[:, c * FF_CH:(c + 1) * FF_CH] = (_silu(a1) * a2).astype(BF16)
    for t, (x_ref, mod_ref) in enumerate(zip(x_refs, mod_refs)):
        hid = h_ref[t * TM:(t + 1) * TM, :]
        y = jnp.concatenate([_dot(hid, wout_ref[:, n * MXU_N:(n + 1) * MXU_N]) for n in range(D_MODEL // MXU_N)],
                            axis=1)
        gate = mod_ref[3 * sub + 2:3 * sub + 3, :]
        r = ALPHA * x_ref[...] + (FFN_RES * gate) * y
        o_ref[t * TM:(t + 1) * TM, :] = _ln(r, g_ref[...], b_ref[...])


def _ffn_call(s, mods, ffn_w_in, ffn_w_out, ln_g, ln_b, *, layer, which, final):
    sub = 2 * which
    if final:
        lat_tiles_b = SEQ // TM
        out_tiles = BATCH * lat_tiles_b
        in_tile = lambda t: (t // lat_tiles_b) * NT_B + 1 + t % lat_tiles_b
    else:
        out_tiles = NT
        in_tile = lambda t: t
    tile_of = lambda i, t: in_tile(i * FFN_TILES + t)
    x_specs = [pl.BlockSpec((TM, D_MODEL), functools.partial(lambda i, t: (tile_of(i, t), 0), t=t))
               for t in range(FFN_TILES)]
    mod_specs = [pl.BlockSpec((None, 9, D_MODEL),
                              functools.partial(lambda i, t: (layer * MOD_ROWS + _mod_row(tile_of(i, t)), 0, 0), t=t))
                 for t in range(FFN_TILES)]
    return pl.pallas_call(
        functools.partial(_ffn_kernel, sub=sub),
        grid=(out_tiles // FFN_TILES,),
        in_specs=x_specs + mod_specs + [
            _const_spec((None, None, D_MODEL, 2 * D_FF), (layer, which, 0, 0)),
            _const_spec((None, None, D_FF, D_MODEL + LANE_PAD), (layer, which, 0, 0)),
            _const_spec((None, 1, D_MODEL), (layer * 3 + sub, 0, 0)),
            _const_spec((None, 1, D_MODEL), (layer * 3 + sub, 0, 0)),
        ],
        out_specs=pl.BlockSpec((FFN_TILES * TM, D_MODEL), lambda i: (i, 0)),
        out_shape=jax.ShapeDtypeStruct((out_tiles * TM, D_MODEL), F32),
        scratch_shapes=[
            pltpu.VMEM((FFN_TILES * TM, D_MODEL), BF16),
            pltpu.VMEM((FFN_TILES * TM, D_FF), BF16),
        ],
        compiler_params=_params(("parallel",)),
        name=f"ffn{which}",
    )(*([s] * FFN_TILES), *([mods] * FFN_TILES), ffn_w_in, ffn_w_out, ln_g, ln_b)


def _cproj_kernel(x_ref, prev_ref, next_ref, mod_ref, w_ref, b_ref, cw_ref,
                  q_ref, kt_ref, v_ref, g_ref, gt_ref, xe_ref, p_ref):
    j = pl.program_id(0) % NT_B
    first = jnp.logical_or(j == 0, j == 1)
    last = jnp.logical_or(j == 0, j == NT_B - 1)
    shift = mod_ref[3:4, :]
    scale = mod_ref[4:5, :]

    def modulate(v):
        return (v * (1.0 + scale) + shift).astype(BF16)

    xe_ref[0:HALO, :] = modulate(prev_ref[...])
    xe_ref[HALO:HALO + TM, :] = modulate(x_ref[...])
    xe_ref[HALO + TM:2 * HALO + TM, :] = modulate(next_ref[...])
    xe = xe_ref[...]
    rows = lax.broadcasted_iota(jnp.int32, (TM + 2 * HALO, 1), 0)
    valid = jnp.logical_and(jnp.logical_or(rows >= HALO, jnp.logical_not(first)),
                            jnp.logical_or(rows < HALO + TM, jnp.logical_not(last)))
    p_ref[...] = jnp.where(valid, _dot(xe, w_ref[:, CP_QK:CP_V]) + b_ref[:, CP_QK:CP_V], 0.0)
    conv = (cw_ref[0:1, :] * p_ref[HALO - 1:HALO - 1 + TM, :]
            + cw_ref[1:2, :] * p_ref[HALO:HALO + TM, :]
            + cw_ref[2:3, :] * p_ref[HALO + 1:HALO + 1 + TM, :])
    qk = _silu(conv)
    q_ref[...] = qk[:, :MIX_W].astype(BF16)
    k = qk[:, MIX_W:] * MLSTM_DH ** -0.5
    for c in range(CH_T):
        for h in range(MLSTM_HEADS):
            blk = k[c * CHUNK:(c + 1) * CHUNK, h * MLSTM_DH:(h + 1) * MLSTM_DH]
            r0 = (c * MLSTM_HEADS + h) * MLSTM_DH
            kt_ref[r0:r0 + MLSTM_DH, :] = blk.T.astype(BF16)

    xm = xe_ref[HALO:HALO + TM, :]
    pvg = _dot(xm, w_ref[:, CP_V:CP_COLS]) + b_ref[:, CP_V:CP_COLS]
    v_ref[...] = pvg[:, :MIX_W].astype(BF16)
    li = pvg[:, CP_GI - CP_V:CP_GF - CP_V]
    fraw = pvg[:, CP_GF - CP_V:]
    lf = jnp.minimum(fraw, 0.0) - jnp.log1p(jnp.exp(-jnp.abs(fraw)))
    row = lax.broadcasted_iota(jnp.int32, (CHUNK, LANES), 0)
    lane = lax.broadcasted_iota(jnp.int32, (CHUNK, LANES), 1)
    fwd_lane = lane < MLSTM_HEADS
    lower_f = (lane <= row).astype(F32)
    for c in range(CH_T):
        tok = slice(c * CHUNK, (c + 1) * CHUNK)
        prefix = _dot_f32(lower_f, lf[tok])
        suffix = prefix[CHUNK - 1:CHUNK, :] - prefix + lf[tok]
        cum = jnp.where(fwd_lane, prefix, suffix)
        r = li[tok] - cum
        rf, rb = r, r
        k = 1
        while k < CHUNK:
            rf = jnp.maximum(rf, jnp.where(row >= k, pltpu.roll(rf, k, 0), -jnp.inf))
            rb = jnp.maximum(rb, jnp.where(row < CHUNK - k, pltpu.roll(rb, CHUNK - k, 0), -jnp.inf))
            k *= 2
        rmax = jnp.where(fwd_lane, rf, rb)
        g_ref[tok, :] = jnp.where(lane < GS_RMAX, cum, pltpu.roll(rmax, GS_RMAX, 1))
        gt_ref[c * SUBLANES:(c + 1) * SUBLANES, :] = r.T[:SUBLANES, :]


def _cproj_call(s, mods, w_cp, b_cp, qk_conv_w, *, layer):
    hb = TM // HALO
    n_hb = T_ALL // HALO
    return pl.pallas_call(
        _cproj_kernel,
        grid=(NT,),
        in_specs=[
            pl.BlockSpec((TM, D_MODEL), lambda i: (i, 0)),
            pl.BlockSpec((HALO, D_MODEL), lambda i: (jnp.maximum(i * hb - 1, 0), 0)),
            pl.BlockSpec((HALO, D_MODEL), lambda i: (jnp.minimum((i + 1) * hb, n_hb - 1), 0)),
            pl.BlockSpec((None, 9, D_MODEL), lambda i: (layer * MOD_ROWS + _mod_row(i), 0, 0)),
            _const_spec((None, D_MODEL, CP_COLS), (layer, 0, 0)),
            _const_spec((None, 1, CP_COLS), (layer, 0, 0)),
            _const_spec((None, QK_CONV, 2 * MIX_W), (layer, 0, 0)),
        ],
        out_specs=[
            pl.BlockSpec((TM, MIX_W), lambda i: (i, 0)),
            pl.BlockSpec((TM * MLSTM_HEADS, MLSTM_DH), lambda i: (i, 0)),
            pl.BlockSpec((TM, MIX_W), lambda i: (i, 0)),
            pl.BlockSpec((TM, LANES), lambda i: (i, 0)),
            pl.BlockSpec((CH_T * SUBLANES, CHUNK), lambda i: (i, 0)),
        ],
        out_shape=[
            jax.ShapeDtypeStruct((T_ALL, MIX_W), BF16),
            jax.ShapeDtypeStruct((T_ALL * MLSTM_HEADS, MLSTM_DH), BF16),
            jax.ShapeDtypeStruct((T_ALL, MIX_W), BF16),
            jax.ShapeDtypeStruct((T_ALL, LANES), F32),
            jax.ShapeDtypeStruct((T_ALL // CHUNK * SUBLANES, CHUNK), F32),
        ],
        scratch_shapes=[
            pltpu.VMEM((TM + 2 * HALO, D_MODEL), BF16),
            pltpu.VMEM((TM + 2 * HALO, 2 * MIX_W), F32),
        ],
        compiler_params=_params(("parallel",)),
        name="cproj",
    )(s, s, s, mods, w_cp, b_cp, qk_conv_w)


def _mlstm_kernel(qf_ref, ktf_ref, vf_ref, gf_ref, gtf_ref, qb_ref, ktb_ref, vb_ref, gb_ref, gtb_ref,
                  hf_ref, hb_ref, c_ref, m_ref):
    @pl.when(pl.program_id(0) == 0)
    def _():
        c_ref[...] = jnp.zeros(c_ref.shape, F32)
        m_ref[...] = jnp.zeros(m_ref.shape, F32)

    row = lax.broadcasted_iota(jnp.int32, (CHUNK, CHUNK), 0)
    col = lax.broadcasted_iota(jnp.int32, (CHUNK, CHUNK), 1)
    ones_col = jnp.ones((CHUNK, MLSTM_DH), BF16)
    nh = MLSTM_HEADS
    dirs = ((qf_ref, ktf_ref, vf_ref, gf_ref, gtf_ref, hf_ref, col <= row, CHUNK - 1),
            (qb_ref, ktb_ref, vb_ref, gb_ref, gtb_ref, hb_ref, col >= row, 0))
    heads = []
    for d, (q_ref, kt_ref, v_ref, g_ref, gt_ref, h_ref, seen, end_row) in enumerate(dirs):
        for b in range(BATCH):
            g = g_ref[b]
            gt = gt_ref[b]
            for h in range(nh):
                lane = d * nh + h
                hd = slice(h * MLSTM_DH, (h + 1) * MLSTM_DH)
                heads.append(dict(
                    idx=(d * BATCH + b) * nh + h, seen=seen, end_row=end_row, out=(h_ref, b, hd), g=g, lane=lane,
                    cum=g[:, lane:lane + 1], rmax=g[:, GS_RMAX + lane:GS_RMAX + lane + 1], r=gt[lane:lane + 1, :],
                    q=q_ref[b, :, hd], kt=kt_ref[b, hd, :], v=v_ref[b, :, hd]))

    for hd in heads:
        hd["qk"] = _dot(hd["q"], hd["kt"])
        hd["m_prev"] = m_ref[hd["idx"], 0:1, 0:1]
        hd["c_old"] = c_ref[hd["idx"]]
        hd["v_aug"] = jnp.concatenate([hd["v"], ones_col], axis=1)
    for hd in heads:
        e = hd["end_row"]
        top = jnp.maximum(hd["m_prev"], hd["rmax"][e:e + 1, :])
        w_row = jnp.exp(hd["r"] - top)
        ktw = (hd["kt"].astype(F32) * w_row).astype(BF16)
        c_ref[hd["idx"]] = jnp.exp(hd["m_prev"] - top) * hd["c_old"] + _dot(ktw, hd["v_aug"])
        m_ref[hd["idx"]] = jnp.broadcast_to(hd["cum"][e:e + 1, :] + top, m_ref.shape[1:])
    lane_row = lax.broadcasted_iota(jnp.int32, (1, LANES), 1)
    for k0 in range(0, len(heads), nh):
        group = heads[k0:k0 + nh]
        g = group[0]["g"]
        m_row = jnp.zeros((1, LANES), F32)
        for hd in group:
            m_row = jnp.where(lane_row == GS_RMAX + hd["lane"], hd["m_prev"], m_row)
        top_all = jnp.maximum(m_row, g)
        w_all = jnp.exp(m_row - top_all)
        floor_all = jnp.exp(-(pltpu.roll(g, GS_RMAX, 1) + top_all))
        for hd in group:
            ln = GS_RMAX + hd["lane"]
            hd["top"], hd["w"], hd["floor"] = (t[:, ln:ln + 1] for t in (top_all, w_all, floor_all))
    for hd in heads:
        top = jnp.broadcast_to(hd["top"], (CHUNK, CHUNK))
        s = hd["qk"] * jnp.exp(jnp.where(hd["seen"], hd["r"] - top, -jnp.inf))
        qw = hd["q"].astype(F32) * hd["w"]
        lhs = jnp.concatenate([s.astype(BF16), qw.astype(BF16)], axis=1)
        rhs = jnp.concatenate([hd["v_aug"], hd["c_old"].astype(BF16)], axis=0)
        hd["na"] = _dot(lhs, rhs)
    for hd in heads:
        den = jnp.maximum(jnp.abs(hd["na"][:, MLSTM_DH:]), hd["floor"])
        h_ref, b, cols = hd["out"]
        h_ref[b, :, cols] = hd["na"][:, :MLSTM_DH] / den


def _mlstm_call(q, kt, v, g, gt):
    fwd = lambda s: s
    bwd = lambda s: jnp.where(s < CTX_CH, CTX_CH - 1 - s, NCH_B + CTX_CH - 1 - s)
    q3 = q.reshape(BATCH, SEQ_B, MIX_W)
    kt3 = kt.reshape(BATCH, SEQ_B * MLSTM_HEADS, MLSTM_DH)
    v3 = v.reshape(BATCH, SEQ_B, MIX_W)
    g3 = g.reshape(BATCH, SEQ_B, LANES)
    gt3 = gt.reshape(BATCH, NCH_B * SUBLANES, CHUNK)

    def specs(order):
        return [
            pl.BlockSpec((BATCH, CHUNK, MIX_W), lambda s: (0, order(s), 0)),
            pl.BlockSpec((BATCH, CHUNK * MLSTM_HEADS, MLSTM_DH), lambda s: (0, order(s), 0)),
            pl.BlockSpec((BATCH, CHUNK, MIX_W), lambda s: (0, order(s), 0)),
            pl.BlockSpec((BATCH, CHUNK, LANES), lambda s: (0, order(s), 0)),
            pl.BlockSpec((BATCH, SUBLANES, CHUNK), lambda s: (0, order(s), 0)),
        ]

    n_state = 2 * BATCH * MLSTM_HEADS
    hf, hb = pl.pallas_call(
        _mlstm_kernel,
        grid=(NCH_B,),
        in_specs=specs(fwd) + specs(bwd),
        out_specs=[
            pl.BlockSpec((BATCH, CHUNK, MIX_W), lambda s: (0, fwd(s), 0)),
            pl.BlockSpec((BATCH, CHUNK, MIX_W), lambda s: (0, bwd(s), 0)),
        ],
        out_shape=[jax.ShapeDtypeStruct((BATCH, SEQ_B, MIX_W), F32)] * 2,
        scratch_shapes=[
            pltpu.VMEM((n_state, MLSTM_DH, 2 * MLSTM_DH), F32),
            pltpu.VMEM((n_state, SUBLANES, LANES), F32),
        ],
        compiler_params=_params(("arbitrary",)),
        name="mlstm",
    )(q3, kt3, v3, g3, gt3, q3, kt3, v3, g3, gt3)
    return hf.reshape(T_ALL, MIX_W), hb.reshape(T_ALL, MIX_W)


VEC_GMLP_G, VEC_GMLP_B, VEC_CONV_B, VEC_CONV_G, VEC_CONV_LB, VEC_MLSTM_G, VEC_POOL_S = range(7)
VEC_ROWS = 8


def _mix_kernel(x_ref, prev_ref, next_ref, mod_ref, hf_ref, hb_ref, w_ref, b_ref, vec_ref, ws_ref, bst_ref,
                cw_ref, pw_ref, wbr_ref, wout_ref, g_ref, beta_ref, o_ref, xe_ref, a_ref, d_ref, ash_ref, conv_ref,
                gate_ref, pa_ref, po_ref, u_ref, vn_ref, yc_ref, yd_ref):
    j = pl.program_id(0) % NT_B
    is_ctx = j == 0
    first = jnp.logical_or(is_ctx, j == 1)
    last = jnp.logical_or(is_ctx, j == NT_B - 1)
    shift = mod_ref[3:4, :]
    scale = mod_ref[4:5, :]
    gate = mod_ref[5:6, :]

    def modulate(v):
        return (v * (1.0 + scale) + shift).astype(BF16)

    def vec(r):
        return vec_ref[r:r + 1, :]

    def proj(ext, lo, width, zero=None):
        lhs = xe_ref[...] if ext else xe_ref[HALO:HALO + TM, :]
        bias = lambda c: b_ref[:, c:c + MXU_N] if zero is None else b_ref[:, c:c + MXU_N] + zero
        return jnp.concatenate([_dot(lhs, w_ref[:, c:c + MXU_N]) + bias(c)
                                for c in range(lo, lo + width, MXU_N)], axis=1)

    xe_ref[0:HALO, :] = modulate(prev_ref[...])
    xe_ref[HALO:HALO + TM, :] = modulate(x_ref[...])
    xe_ref[HALO + TM:2 * HALO + TM, :] = modulate(next_ref[...])
    rows = lax.broadcasted_iota(jnp.int32, (TM + 2 * HALO, 1), 0)
    valid = jnp.logical_and(jnp.logical_or(rows >= HALO, jnp.logical_not(first)),
                            jnp.logical_or(rows < HALO + TM, jnp.logical_not(last)))

    pb = proj(True, MX_B, 2 * MIX_W)
    a_ref[...] = jnp.where(valid, pb[:, :MIX_W] * _sigmoid(pb[:, MIX_W:]), 0.0)
    pa_ref[...] = proj(False, MX_A, 2 * MIX_W)

    gate_chunks = [(i, n) for i in range(N_BRANCH) for n in range(D_MODEL // MXU_N)]
    sh_rows = TM + 2 * HALO - SUBLANES

    def zero_of(v):
        bits = lax.bitcast_convert_type(v[0:1, 0:LANES], jnp.uint32)
        return lax.shift_right_logical(lax.shift_right_logical(bits, jnp.uint32(16)), jnp.uint32(16)).astype(F32)

    def conv_block(cb, rb, z):
        cols = slice(cb * LANES, (cb + 1) * LANES)
        part = None
        for k in range(CONV_W):
            q8, s = divmod(HALO - CONV_W // 2 + k, SUBLANES)
            r0 = q8 * SUBLANES + rb * CONV_RB
            src = a_ref[r0:r0 + CONV_RB, cols] if s == 0 else ash_ref[s - 1, r0:r0 + CONV_RB, cols]
            term = (cw_ref[k:k + 1, cols] + z) * src
            part = term if part is None else part + term
        conv_ref[rb * CONV_RB:(rb + 1) * CONV_RB, cols] = part
        return part

    conv_blocks = [(cb, rb) for cb in range(MIX_W // LANES) for rb in range(TM // CONV_RB)]
    per_step = len(conv_blocks) // len(gate_chunks)
    zero_row = jnp.zeros((1, LANES), F32)
    v_zero = [zero_row] * len(gate_chunks)
    for step, (i, n) in enumerate(gate_chunks):
        lo = i * D_MODEL + n * MXU_N
        z_in = jnp.concatenate([v_zero[step - CHAIN_LAG]] * (MXU_N // LANES), axis=1) if step >= CHAIN_LAG else None
        gv = jnp.tanh(0.5 * proj(False, MX_G + lo, MXU_N, z_in))
        gate_ref[:, lo:lo + MXU_N] = gv
        z = zero_of(gv) + v_zero[step - 1]
        for cb, rb in conv_blocks[step * per_step:(step + 1) * per_step]:
            if rb == 0:
                cols = slice(cb * LANES, (cb + 1) * LANES)
                for s in range(1, SUBLANES):
                    ash_ref[s - 1, 0:sh_rows, cols] = a_ref[s:s + sh_rows, cols]
            z = zero_of(conv_block(cb, rb, z))
        v_zero[step] = z

    pa = _gelu_tanh(pa_ref[...])
    u_ref[...] = pa[:, :MIX_W]
    vn_ref[...] = _ln(pa[:, MIX_W:], vec(VEC_GMLP_G), vec(VEC_GMLP_B)).astype(BF16)

    d_ref[0:TM + 2 * HALO, :] = jnp.where(valid, proj(True, MX_D, MIX_W), 0.0)
    d_ref[TM + 2 * HALO:TM + 2 * HALO + SUBLANES, :] = jnp.zeros((SUBLANES, MIX_W), F32)
    pos = jnp.where(is_ctx, 0, (j - 1) * TM) + lax.broadcasted_iota(jnp.int32, (TM, 1), 0)
    n_seq = jnp.where(is_ctx, CTX_LEN, SEQ)

    def window_sum(cols, win):
        start = HALO - win // 2
        if win < SUBLANES:
            ws = d_ref[start:start + TM, cols]
            for k in range(1, win):
                ws = ws + d_ref[start + k:start + k + TM, cols]
            return ws
        base = start // SUBLANES * SUBLANES
        off = start - base
        up8 = lambda r: -(-r // SUBLANES) * SUBLANES
        rows = [up8(TM + off)]
        n = win
        while n > 1:
            n //= 2
            rows.append(up8(rows[-1] + n))
        rows.reverse()
        level = d_ref[base:base + rows[1], cols] + d_ref[base + 1:base + 1 + rows[1], cols]
        n, i = 2, 1
        while n < win:
            i += 1
            level = level[0:rows[i], :] + level[n:n + rows[i], :]
            n *= 2
        return level[off:off + TM, :]

    for gi, win in enumerate(POOL_WINDOWS):
        lo, hi = win // 2, win - 1 - win // 2
        cols = slice(gi * POOL_GD, (gi + 1) * POOL_GD)
        wsum = window_sum(cols, win)
        cnt = (jnp.minimum(pos + hi + 1, n_seq) - jnp.maximum(pos - lo, 0)).astype(F32)
        diff = wsum / cnt - d_ref[HALO:HALO + TM, cols]
        yd_ref[:, cols] = (_dot(diff.astype(BF16), pw_ref[gi]) * vec_ref[VEC_POOL_S:VEC_POOL_S + 1, cols]
                           ).astype(BF16)

    po_ref[...] = proj(False, MX_O, MIX_W)
    for h in range(MLSTM_HEADS):
        cols = slice(h * MLSTM_DH, (h + 1) * MLSTM_DH)
        hh = hf_ref[:, cols] + hb_ref[:, cols]
        mu = jnp.mean(hh, axis=-1, keepdims=True)
        hc = hh - mu
        hn = hc * lax.rsqrt(jnp.mean(hc * hc, axis=-1, keepdims=True) + LN_EPS)
        yc_ref[:, cols] = (_sigmoid(po_ref[:, cols]) * (hn * vec_ref[VEC_MLSTM_G:VEC_MLSTM_G + 1, cols])
                           ).astype(BF16)

    gd = MIX_W // GMLP_GROUPS
    z_rows = []
    for c in range(CH_T):
        z_cols = []
        for gi in range(GMLP_GROUPS):
            blk = vn_ref[c * CHUNK:(c + 1) * CHUNK, gi * gd:(gi + 1) * gd]
            z_cols.append(_dot(ws_ref[gi], blk) + bst_ref[:, gi:gi + 1])
        z_rows.append(jnp.concatenate(z_cols, axis=1))
    ya = (u_ref[...] * jnp.concatenate(z_rows, axis=0)).astype(BF16)
    yb = _silu(_ln(conv_ref[...] + vec(VEC_CONV_B), vec(VEC_CONV_G), vec(VEC_CONV_LB))).astype(BF16)
    yc = yc_ref[...]
    yd = yd_ref[...]

    merged = []
    for n in range(D_MODEL // MXU_N):
        cols = slice(n * MXU_N, (n + 1) * MXU_N)
        acc = None
        for i, yi in enumerate((ya, yb, yc, yd)):
            lo = i * D_MODEL + n * MXU_N
            p = _dot(yi, wbr_ref[i, :, cols])
            term = gate_ref[:, lo:lo + MXU_N] * p + p
            acc = term if acc is None else acc + term
        merged.append((0.5 * acc).astype(BF16))
    merged = jnp.concatenate(merged, axis=1)
    y = jnp.concatenate([_dot(merged, wout_ref[:, n * MXU_N:(n + 1) * MXU_N]) for n in range(D_MODEL // MXU_N)],
                        axis=1)
    o_ref[...] = _ln(ALPHA * x_ref[...] + gate * y, g_ref[...], beta_ref[...])


def _mix_call(s, mods, hf, hb, w_mx, b_mx, vec512, gmlp_ws, gmlp_bst, conv_w, pool_w, w_branch, w_out,
              ln_g, ln_b, *, layer):
    hb_per_tile = TM // HALO
    n_hb = T_ALL // HALO
    return pl.pallas_call(
        _mix_kernel,
        grid=(NT,),
        in_specs=[
            pl.BlockSpec((TM, D_MODEL), lambda i: (i, 0)),
            pl.BlockSpec((HALO, D_MODEL), lambda i: (jnp.maximum(i * hb_per_tile - 1, 0), 0)),
            pl.BlockSpec((HALO, D_MODEL), lambda i: (jnp.minimum((i + 1) * hb_per_tile, n_hb - 1), 0)),
            pl.BlockSpec((None, 9, D_MODEL), lambda i: (layer * MOD_ROWS + _mod_row(i), 0, 0)),
            pl.BlockSpec((TM, MIX_W), lambda i: (i, 0)),
            pl.BlockSpec((TM, MIX_W), lambda i: (i, 0)),
            _const_spec((None, D_MODEL, MX_W), (layer, 0, 0)),
            _const_spec((None, 1, MX_COLS), (layer, 0, 0)),
            _const_spec((None, VEC_ROWS, MIX_W), (layer, 0, 0)),
            _const_spec((None, GMLP_GROUPS, CHUNK, CHUNK), (layer, 0, 0, 0)),
            _const_spec((None, CHUNK, GMLP_GROUPS), (layer, 0, 0)),
            _const_spec((None, CONV_W, MIX_W), (layer, 0, 0)),
            _const_spec((None, len(POOL_WINDOWS), POOL_GD, POOL_GD), (layer, 0, 0, 0)),
            _const_spec((None, N_BRANCH, MIX_W, D_MODEL + LANE_PAD), (layer, 0, 0, 0)),
            _const_spec((None, D_MODEL, D_MODEL + LANE_PAD), (layer, 0, 0)),
            _const_spec((None, 1, D_MODEL), (layer * 3 + 1, 0, 0)),
            _const_spec((None, 1, D_MODEL), (layer * 3 + 1, 0, 0)),
        ],
        out_specs=pl.BlockSpec((TM, D_MODEL), lambda i: (i, 0)),
        out_shape=jax.ShapeDtypeStruct((T_ALL, D_MODEL), F32),
        scratch_shapes=[
            pltpu.VMEM((TM + 2 * HALO, D_MODEL), BF16),
            pltpu.VMEM((TM + 2 * HALO, MIX_W), F32),
            pltpu.VMEM((TM + 2 * HALO + SUBLANES, MIX_W), F32),
            pltpu.VMEM((SUBLANES - 1, TM + 2 * HALO, MIX_W), F32),
            pltpu.VMEM((TM, MIX_W), F32),
            pltpu.VMEM((TM, N_BRANCH * D_MODEL), F32),
            pltpu.VMEM((TM, 2 * MIX_W), F32),
            pltpu.VMEM((TM, MIX_W), F32),
            pltpu.VMEM((TM, MIX_W), F32),
            pltpu.VMEM((TM, MIX_W), BF16),
            pltpu.VMEM((TM, MIX_W), BF16),
            pltpu.VMEM((TM, MIX_W), BF16),
        ],
        compiler_params=_params(("parallel",)),
        name="mix",
    )(s, s, s, mods, hf, hb, w_mx, b_mx, vec512, gmlp_ws, gmlp_bst, conv_w, pool_w, w_branch, w_out, ln_g, ln_b)


MX_TILE = 512
MX_PAD = MX_TILE
MX_W = MX_COLS + MX_PAD
CP_TILE = 256


def _mx_src(j):
    a, o, d = (MX_O - MX_A) // MX_TILE, (MX_D - MX_A) // MX_TILE, (MX_G - MX_A) // MX_TILE
    g_end = MX_COLS // MX_TILE
    return jnp.where(j < a, OFF_A + MX_TILE * j,
                     jnp.where(j < o, OFF_C_O + MX_TILE * (j - a),
                               jnp.where(j < d, OFF_D + MX_TILE * (j - o),
                                         jnp.where(j < g_end, OFF_G + MX_TILE * (j - d), 0))))


def _prep_mx_kernel(wt_ref, o_ref):
    j = pl.program_id(1)

    @pl.when(j < MX_COLS // MX_TILE)
    def _():
        o_ref[...] = wt_ref[0].T.astype(BF16)

    @pl.when(j >= MX_COLS // MX_TILE)
    def _():
        o_ref[...] = jnp.zeros(o_ref.shape, BF16)


def _prep_cp_kernel(wt_ref, o_ref):
    j = pl.program_id(1)
    n_plain = (CP_GI - CP_QK) // CP_TILE

    @pl.when(j < n_plain)
    def _():
        o_ref[...] = wt_ref[0].T.astype(BF16)

    @pl.when(j == n_plain)
    def _():
        nh = MLSTM_HEADS
        t = wt_ref[0, 0:LANES, :].T
        lane = lax.broadcasted_iota(jnp.int32, t.shape, 1)
        up1 = pltpu.roll(t, LANES - nh, 1)
        up2 = pltpu.roll(t, LANES - 2 * nh, 1)
        gi = jnp.where(lane < nh, t, jnp.where(lane < 2 * nh, up1, 0.0))
        gf = jnp.where(lane < nh, up1, jnp.where(lane < 2 * nh, up2, 0.0))
        o_ref[:, 0:LANES] = gi.astype(BF16)
        o_ref[:, LANES:2 * LANES] = gf.astype(BF16)


def _prep_win_call(w_in):
    wt = jnp.swapaxes(w_in, 1, 2)
    n_plain = (CP_GI - CP_QK) // CP_TILE
    w_mx = pl.pallas_call(
        _prep_mx_kernel,
        grid=(DEPTH, MX_W // MX_TILE),
        in_specs=[pl.BlockSpec((pl.Element(1), pl.Element(MX_TILE), pl.Element(D_MODEL)),
                               lambda l, j: (l, pl.multiple_of(_mx_src(j), SUBLANES), 0))],
        out_specs=pl.BlockSpec((None, D_MODEL, MX_TILE), lambda l, j: (l, 0, j)),
        out_shape=jax.ShapeDtypeStruct((DEPTH, D_MODEL, MX_W), BF16),
        compiler_params=_params(("parallel", "parallel")),
        name="prep_mx",
    )(wt)
    w_cp = pl.pallas_call(
        _prep_cp_kernel,
        grid=(DEPTH, CP_COLS // CP_TILE),
        in_specs=[pl.BlockSpec((pl.Element(1), pl.Element(CP_TILE), pl.Element(D_MODEL)),
                               lambda l, j: (l, pl.multiple_of(
                                   jnp.where(j < n_plain, OFF_C + CP_TILE * j, OFF_C_GATES), SUBLANES), 0))],
        out_specs=pl.BlockSpec((None, D_MODEL, CP_TILE), lambda l, j: (l, 0, j)),
        out_shape=jax.ShapeDtypeStruct((DEPTH, D_MODEL, CP_COLS), BF16),
        compiler_params=_params(("parallel", "parallel")),
        name="prep_cp",
    )(wt)
    return w_mx, w_cp


def _pad_cast_kernel(w_ref, o_ref):
    cols = w_ref.shape[1]
    o_ref[:, 0:cols] = w_ref[...].astype(BF16)
    o_ref[:, cols:] = jnp.zeros((o_ref.shape[0], o_ref.shape[1] - cols), BF16)


def _pad_cast_call(w):
    lead, cols = w.shape[:-1], w.shape[-1]
    rows = math.prod(lead)
    rb = 512
    out = pl.pallas_call(
        _pad_cast_kernel,
        grid=(rows // rb,),
        in_specs=[pl.BlockSpec((rb, cols), lambda i: (i, 0))],
        out_specs=pl.BlockSpec((rb, cols + LANE_PAD), lambda i: (i, 0)),
        out_shape=jax.ShapeDtypeStruct((rows, cols + LANE_PAD), BF16),
        compiler_params=_params(("parallel",)),
        name="pad_cast",
    )(w.reshape(rows, cols))
    return out.reshape(lead + (cols + LANE_PAD,))


def _repack_cols(w, pieces):
    parts = [jnp.zeros(w.shape[:-1] + (p,), w.dtype) if isinstance(p, int) else w[..., p[0]:p[1]] for p in pieces]
    return jnp.concatenate(parts, axis=-1)


def kernel(x, c, ctx, c_ctx, w_ada, b_ada, ln_g, ln_b, ffn_w_in, ffn_w_out, w_in, b_in, gmlp_ln_g, gmlp_ln_b,
           gmlp_ws, gmlp_bs, conv_w, conv_b, conv_ln_g, conv_ln_b, qk_conv_w, mlstm_ln_g, pool_w, pool_scale,
           w_branch, w_out):
    mix_pieces = [(OFF_A, OFF_C), (OFF_C_O, OFF_D), (OFF_D, OFF_G), (OFF_G, IN_COLS)]
    nh = MLSTM_HEADS
    gate_pad = LANES - 2 * nh

    def cproj_cols(w):
        g = w[..., OFF_C_GATES:OFF_C_O].reshape(w.shape[:-1] + (4, nh))
        pad = jnp.zeros(w.shape[:-1] + (gate_pad,), w.dtype)
        return jnp.concatenate([w[..., OFF_C:OFF_C_GATES], g[..., 0, :], g[..., 2, :], pad,
                                g[..., 1, :], g[..., 3, :], pad], axis=-1)
    w_mx, w_cp = _prep_win_call(w_in)
    b_mx = _repack_cols(b_in, mix_pieces).reshape(DEPTH, 1, MX_COLS)
    b_cp = cproj_cols(b_in).reshape(DEPTH, 1, CP_COLS)
    ffn_w_in_b = ffn_w_in.astype(BF16)
    ffn_w_out_b = _pad_cast_call(ffn_w_out)
    w_branch_b = _pad_cast_call(w_branch)
    w_out_b = _pad_cast_call(w_out)
    gmlp_ws_b = gmlp_ws.astype(BF16)
    pool_w_b = pool_w.astype(BF16)
    gmlp_bst = jnp.swapaxes(gmlp_bs, 1, 2)
    vec512 = jnp.stack([gmlp_ln_g, gmlp_ln_b, conv_b, conv_ln_g, conv_ln_b, mlstm_ln_g, pool_scale,
                        jnp.zeros_like(pool_scale)], axis=1)
    ln_g3 = ln_g.reshape(DEPTH * 3, 1, D_MODEL)
    ln_b3 = ln_b.reshape(DEPTH * 3, 1, D_MODEL)
    c_t = jnp.concatenate([c, c_ctx[None], jnp.zeros((MOD_ROWS - BATCH - 1, D_MODEL), F32)], axis=0).T

    mods = _mods_call(c_t, w_ada, b_ada).reshape(DEPTH * MOD_ROWS, 9, D_MODEL)
    s = _embed_call(x, ctx)
    for l in range(DEPTH):
        last = l == DEPTH - 1
        s = _ffn_call(s, mods, ffn_w_in_b, ffn_w_out_b, ln_g3, ln_b3, layer=l, which=0, final=False)
        q, kt, v, g, gt = _cproj_call(s, mods, w_cp, b_cp, qk_conv_w, layer=l)
        hf, hb = _mlstm_call(q, kt, v, g, gt)
        s = _mix_call(s, mods, hf, hb, w_mx, b_mx, vec512, gmlp_ws_b, gmlp_bst, conv_w, pool_w_b, w_branch_b,
                      w_out_b, ln_g3, ln_b3, layer=l)
        s = _ffn_call(s, mods, ffn_w_in_b, ffn_w_out_b, ln_g3, ln_b3, layer=l, which=1, final=last)
    return s.reshape(BATCH, SEQ, D_MODEL)
```

```python
import functools
import math

import jax
import jax.numpy as jnp
from jax import lax
from jax.experimental import pallas as pl
from jax.experimental.pallas import tpu as pltpu

D_MODEL = 1024
BATCH = 2
SEQ = 8192
DEPTH = 4
GRID_W = 64
CTX_LEN = 256
MIX_W = D_MODEL // 2
N_BRANCH = 4
CHUNK = 128
GMLP_GROUPS = 4
CONV_W = 31
MLSTM_HEADS = 4
MLSTM_DH = MIX_W // MLSTM_HEADS
QK_CONV = 3
POOL_WINDOWS = (2, 4, 8, 16)
POOL_GD = MIX_W // len(POOL_WINDOWS)
D_FF = 128 * ((8 * D_MODEL // 3 + 127) // 128)
ALPHA = (2 * DEPTH) ** 0.25
LN_EPS = 1e-6
FFN_RES = 0.5

OFF_A = 0
OFF_B = OFF_A + 2 * MIX_W
OFF_C = OFF_B + 2 * MIX_W
OFF_C_GATES = OFF_C + 3 * MIX_W
OFF_C_O = OFF_C_GATES + 4 * MLSTM_HEADS
OFF_D = OFF_C_O + MIX_W
OFF_G = OFF_D + MIX_W
IN_COLS = OFF_G + N_BRANCH * D_MODEL

LANES = 128
SUBLANES = 8
TM = 256
SEQ_B = CTX_LEN + SEQ
T_ALL = BATCH * SEQ_B
NT_B = SEQ_B // TM
NT = BATCH * NT_B
NCH_B = SEQ_B // CHUNK
CH_T = TM // CHUNK
CTX_CH = CTX_LEN // CHUNK
HALO = 16
MXU_N = 256
LANE_PAD = LANES
FF_CH = MXU_N
FFN_TILES = 2
CONV_RB = 64
CHAIN_LAG = 3
GS_RMAX = 2 * MLSTM_HEADS
MOD_ROWS = 8
CTX_MOD_ROW = BATCH
VMEM_LIMIT = 56 * 1024 * 1024

MX_A = 0
MX_B = MX_A + 2 * MIX_W
MX_O = MX_B + 2 * MIX_W
MX_D = MX_O + MIX_W
MX_G = MX_D + MIX_W
MX_COLS = MX_G + N_BRANCH * D_MODEL
CP_QK = 0
CP_V = 2 * MIX_W
CP_GI = 3 * MIX_W
CP_GF = CP_GI + LANES
CP_COLS = CP_GF + LANES

F32 = jnp.float32
BF16 = jnp.bfloat16


def _dot(a, b):
    return jnp.dot(a, b, preferred_element_type=F32)


def _dot_f32(a, b):
    return jnp.dot(a, b, preferred_element_type=F32, precision=lax.Precision.HIGHEST)


def _ln(r, g, b):
    mu = jnp.mean(r, axis=-1, keepdims=True)
    xc = r - mu
    var = jnp.mean(xc * xc, axis=-1, keepdims=True)
    return xc * lax.rsqrt(var + LN_EPS) * g + b


def _sigmoid(x):
    return 0.5 * jnp.tanh(0.5 * x) + 0.5


def _silu(x):
    hx = 0.5 * x
    return hx * jnp.tanh(hx) + hx


def _gelu_tanh(x):
    c = math.sqrt(2.0 / math.pi)
    hx = 0.5 * x
    return hx * jnp.tanh(x * (c + (c * 0.044715) * (x * x))) + hx


def _mod_row(i):
    return jnp.where(i % NT_B == 0, CTX_MOD_ROW, i // NT_B)


def _const_spec(block, index):
    return pl.BlockSpec(block, lambda *_: index, pipeline_mode=pl.Buffered(1))


def _params(sem):
    return pltpu.CompilerParams(dimension_semantics=sem, vmem_limit_bytes=VMEM_LIMIT)


def _mods_kernel(ct_ref, w_ref, b_ref, o_ref):
    s = _silu(ct_ref[...])
    w = w_ref[...]
    o_ref[...] = jnp.zeros(o_ref.shape, F32)
    for r in range(BATCH + 1):
        o_ref[r:r + 1, :] = jnp.sum(s[:, r:r + 1] * w, axis=0, keepdims=True) + b_ref[...]


def _mods_call(c_t, w_ada, b_ada):
    tn = D_MODEL
    n_col = w_ada.shape[-1] // tn
    return pl.pallas_call(
        _mods_kernel,
        grid=(DEPTH, n_col),
        in_specs=[
            pl.BlockSpec((D_MODEL, MOD_ROWS), lambda l, n: (0, 0)),
            pl.BlockSpec((None, D_MODEL, tn), lambda l, n: (l, 0, n)),
            pl.BlockSpec((None, 1, tn), lambda l, n: (l, 0, n)),
        ],
        out_specs=pl.BlockSpec((None, MOD_ROWS, tn), lambda l, n: (l, 0, n)),
        out_shape=jax.ShapeDtypeStruct((DEPTH, MOD_ROWS, w_ada.shape[-1]), F32),
        compiler_params=_params(("parallel", "parallel")),
        name="mods",
    )(c_t, w_ada, b_ada.reshape(DEPTH, 1, -1))


def _embed_kernel(x_ref, ctx_ref, o_ref):
    j = pl.program_id(0) % NT_B

    @pl.when(j == 0)
    def _():
        o_ref[...] = ctx_ref[...]

    @pl.when(j > 0)
    def _():
        quarter = D_MODEL // 4
        grid_rows = TM // GRID_W
        k = lax.broadcasted_iota(jnp.int32, (1, quarter), 1).astype(F32)
        freqs = jnp.exp(-math.log(10000.0) * k / quarter)
        r = ((j - 1) * grid_rows + lax.broadcasted_iota(jnp.int32, (SUBLANES, 1), 0)).astype(F32)
        col = lax.broadcasted_iota(jnp.int32, (GRID_W, 1), 0).astype(F32)
        er = r * freqs
        ec = col * freqs
        sin_r, cos_r, sin_c, cos_c = jnp.sin(er), jnp.cos(er), jnp.sin(ec), jnp.cos(ec)
        for q in range(grid_rows):
            rows = slice(q * GRID_W, (q + 1) * GRID_W)
            o_ref[rows, 0 * quarter:1 * quarter] = x_ref[rows, 0 * quarter:1 * quarter] + sin_r[q:q + 1, :]
            o_ref[rows, 1 * quarter:2 * quarter] = x_ref[rows, 1 * quarter:2 * quarter] + cos_r[q:q + 1, :]
            o_ref[rows, 2 * quarter:3 * quarter] = x_ref[rows, 2 * quarter:3 * quarter] + sin_c
            o_ref[rows, 3 * quarter:4 * quarter] = x_ref[rows, 3 * quarter:4 * quarter] + cos_c


def _embed_call(x, ctx):
    lat_tiles_b = SEQ // TM
    return pl.pallas_call(
        _embed_kernel,
        grid=(NT,),
        in_specs=[
            pl.BlockSpec((TM, D_MODEL), lambda i: ((i // NT_B) * lat_tiles_b + jnp.maximum(i % NT_B - 1, 0), 0)),
            pl.BlockSpec((TM, D_MODEL), lambda i: (i // NT_B, 0)),
        ],
        out_specs=pl.BlockSpec((TM, D_MODEL), lambda i: (i, 0)),
        out_shape=jax.ShapeDtypeStruct((T_ALL, D_MODEL), F32),
        compiler_params=_params(("parallel",)),
        name="embed",
    )(x.reshape(BATCH * SEQ, D_MODEL), ctx.reshape(BATCH * CTX_LEN, D_MODEL))


def _ffn_kernel(*refs, sub):
    x_refs = refs[:FFN_TILES]
    mod_refs = refs[FFN_TILES:2 * FFN_TILES]
    win_ref, wout_ref, g_ref, b_ref, o_ref, xm_ref, h_ref = refs[2 * FFN_TILES:]
    for t, (x_ref, mod_ref) in enumerate(zip(x_refs, mod_refs)):
        shift = mod_ref[3 * sub + 0:3 * sub + 1, :]
        scale = mod_ref[3 * sub + 1:3 * sub + 2, :]
        xm_ref[t * TM:(t + 1) * TM, :] = (x_ref[...] * (1.0 + scale) + shift).astype(BF16)
    xm = xm_ref[...]
    for c in range(D_FF // FF_CH):
        a1 = _dot(xm, win_ref[:, c * FF_CH:(c + 1) * FF_CH])
        a2 = _dot(xm, win_ref[:, D_FF + c * FF_CH:D_FF + (c + 1) * FF_CH])
        h_ref[:, c * FF_CH:(c + 1) * FF_CH] = (_silu(a1) * a2).astype(BF16)
    for t, (x_ref, mod_ref) in enumerate(zip(x_refs, mod_refs)):
        hid = h_ref[t * TM:(t + 1) * TM, :]
        y = jnp.concatenate([_dot(hid, wout_ref[:, n * MXU_N:(n + 1) * MXU_N]) for n in range(D_MODEL // MXU_N)],
                            axis=1)
        gate = mod_ref[3 * sub + 2:3 * sub + 3, :]
        r = ALPHA * x_ref[...] + (FFN_RES * gate) * y
        o_ref[t * TM:(t + 1) * TM, :] = _ln(r, g_ref[...], b_ref[...])


def _ffn_call(s, mods, ffn_w_in, ffn_w_out, ln_g, ln_b, *, layer, which, final):
    sub = 2 * which
    if final:
        lat_tiles_b = SEQ // TM
        out_tiles = BATCH * lat_tiles_b
        in_tile = lambda t: (t // lat_tiles_b) * NT_B + 1 + t % lat_tiles_b
    else:
        out_tiles = NT
        in_tile = lambda t: t
    tile_of = lambda i, t: in_tile(i * FFN_TILES + t)
    x_specs = [pl.BlockSpec((TM, D_MODEL), functools.partial(lambda i, t: (tile_of(i, t), 0), t=t))
               for t in range(FFN_TILES)]
    mod_specs = [pl.BlockSpec((None, 9, D_MODEL),
                              functools.partial(lambda i, t: (layer * MOD_ROWS + _mod_row(tile_of(i, t)), 0, 0), t=t))
                 for t in range(FFN_TILES)]
    return pl.pallas_call(
        functools.partial(_ffn_kernel, sub=sub),
        grid=(out_tiles // FFN_TILES,),
        in_specs=x_specs + mod_specs + [
            _const_spec((None, None, D_MODEL, 2 * D_FF), (layer, which, 0, 0)),
            _const_spec((None, None, D_FF, D_MODEL + LANE_PAD), (layer, which, 0, 0)),
            _const_spec((None, 1, D_MODEL), (layer * 3 + sub, 0, 0)),
            _const_spec((None, 1, D_MODEL), (layer * 3 + sub, 0, 0)),
        ],
        out_specs=pl.BlockSpec((FFN_TILES * TM, D_MODEL), lambda i: (i, 0)),
        out_shape=jax.ShapeDtypeStruct((out_tiles * TM, D_MODEL), F32),
        scratch_shapes=[
            pltpu.VMEM((FFN_TILES * TM, D_MODEL), BF16),
            pltpu.VMEM((FFN_TILES * TM, D_FF), BF16),
        ],
        compiler_params=_params(("parallel",)),
        name=f"ffn{which}",
    )(*([s] * FFN_TILES), *([mods] * FFN_TILES), ffn_w_in, ffn_w_out, ln_g, ln_b)


def _cproj_kernel(x_ref, prev_ref, next_ref, mod_ref, w_ref, b_ref, cw_ref,
                  q_ref, kt_ref, v_ref, g_ref, gt_ref, xe_ref, p_ref):
    j = pl.program_id(0) % NT_B
    first = jnp.logical_or(j == 0, j == 1)
    last = jnp.logical_or(j == 0, j == NT_B - 1)
    shift = mod_ref[3:4, :]
    scale = mod_ref[4:5, :]

    def modulate(v):
        return (v * (1.0 + scale) + shift).astype(BF16)

    xe_ref[0:HALO, :] = modulate(prev_ref[...])
    xe_ref[HALO:HALO + TM, :] = modulate(x_ref[...])
    xe_ref[HALO + TM:2 * HALO + TM, :] = modulate(next_ref[...])
    xe = xe_ref[...]
    rows = lax.broadcasted_iota(jnp.int32, (TM + 2 * HALO, 1), 0)
    valid = jnp.logical_and(jnp.logical_or(rows >= HALO, jnp.logical_not(first)),
                            jnp.logical_or(rows < HALO + TM, jnp.logical_not(last)))
    p_ref[...] = jnp.where(valid, _dot(xe, w_ref[:, CP_QK:CP_V]) + b_ref[:, CP_QK:CP_V], 0.0)
    conv = (cw_ref[0:1, :] * p_ref[HALO - 1:HALO - 1 + TM, :]
            + cw_ref[1:2, :] * p_ref[HALO:HALO + TM, :]
            + cw_ref[2:3, :] * p_ref[HALO + 1:HALO + 1 + TM, :])
    qk = _silu(conv)
    q_ref[...] = qk[:, :MIX_W].astype(BF16)
    k = qk[:, MIX_W:] * MLSTM_DH ** -0.5
    for c in range(CH_T):
        for h in range(MLSTM_HEADS):
            blk = k[c * CHUNK:(c + 1) * CHUNK, h * MLSTM_DH:(h + 1) * MLSTM_DH]
            r0 = (c * MLSTM_HEADS + h) * MLSTM_DH
            kt_ref[r0:r0 + MLSTM_DH, :] = blk.T.astype(BF16)

    xm = xe_ref[HALO:HALO + TM, :]
    pvg = _dot(xm, w_ref[:, CP_V:CP_COLS]) + b_ref[:, CP_V:CP_COLS]
    v_ref[...] = pvg[:, :MIX_W].astype(BF16)
    li = pvg[:, CP_GI - CP_V:CP_GF - CP_V]
    fraw = pvg[:, CP_GF - CP_V:]
    lf = jnp.minimum(fraw, 0.0) - jnp.log1p(jnp.exp(-jnp.abs(fraw)))
    row = lax.broadcasted_iota(jnp.int32, (CHUNK, LANES), 0)
    lane = lax.broadcasted_iota(jnp.int32, (CHUNK, LANES), 1)
    fwd_lane = lane < MLSTM_HEADS
    lower_f = (lane <= row).astype(F32)
    for c in range(CH_T):
        tok = slice(c * CHUNK, (c + 1) * CHUNK)
        prefix = _dot_f32(lower_f, lf[tok])
        suffix = prefix[CHUNK - 1:CHUNK, :] - prefix + lf[tok]
        cum = jnp.where(fwd_lane, prefix, suffix)
        r = li[tok] - cum
        rf, rb = r, r
        k = 1
        while k < CHUNK:
            rf = jnp.maximum(rf, jnp.where(row >= k, pltpu.roll(rf, k, 0), -jnp.inf))
            rb = jnp.maximum(rb, jnp.where(row < CHUNK - k, pltpu.roll(rb, CHUNK - k, 0), -jnp.inf))
            k *= 2
        rmax = jnp.where(fwd_lane, rf, rb)
        g_ref[tok, :] = jnp.where(lane < GS_RMAX, cum, pltpu.roll(rmax, GS_RMAX, 1))
        gt_ref[c * SUBLANES:(c + 1) * SUBLANES, :] = r.T[:SUBLANES, :]


def _cproj_call(s, mods, w_cp, b_cp, qk_conv_w, *, layer):
    hb = TM // HALO
    n_hb = T_ALL // HALO
    return pl.pallas_call(
        _cproj_kernel,
        grid=(NT,),
        in_specs=[
            pl.BlockSpec((TM, D_MODEL), lambda i: (i, 0)),
            pl.BlockSpec((HALO, D_MODEL), lambda i: (jnp.maximum(i * hb - 1, 0), 0)),
            pl.BlockSpec((HALO, D_MODEL), lambda i: (jnp.minimum((i + 1) * hb, n_hb - 1), 0)),
            pl.BlockSpec((None, 9, D_MODEL), lambda i: (layer * MOD_ROWS + _mod_row(i), 0, 0)),
            _const_spec((None, D_MODEL, CP_COLS), (layer, 0, 0)),
            _const_spec((None, 1, CP_COLS), (layer, 0, 0)),
            _const_spec((None, QK_CONV, 2 * MIX_W), (layer, 0, 0)),
        ],
        out_specs=[
            pl.BlockSpec((TM, MIX_W), lambda i: (i, 0)),
            pl.BlockSpec((TM * MLSTM_HEADS, MLSTM_DH), lambda i: (i, 0)),
            pl.BlockSpec((TM, MIX_W), lambda i: (i, 0)),
            pl.BlockSpec((TM, LANES), lambda i: (i, 0)),
            pl.BlockSpec((CH_T * SUBLANES, CHUNK), lambda i: (i, 0)),
        ],
        out_shape=[
            jax.ShapeDtypeStruct((T_ALL, MIX_W), BF16),
            jax.ShapeDtypeStruct((T_ALL * MLSTM_HEADS, MLSTM_DH), BF16),
            jax.ShapeDtypeStruct((T_ALL, MIX_W), BF16),
            jax.ShapeDtypeStruct((T_ALL, LANES), F32),
            jax.ShapeDtypeStruct((T_ALL // CHUNK * SUBLANES, CHUNK), F32),
        ],
        scratch_shapes=[
            pltpu.VMEM((TM + 2 * HALO, D_MODEL), BF16),
            pltpu.VMEM((TM + 2 * HALO, 2 * MIX_W), F32),
        ],
        compiler_params=_params(("parallel",)),
        name="cproj",
    )(s, s, s, mods, w_cp, b_cp, qk_conv_w)


def _mlstm_kernel(qf_ref, ktf_ref, vf_ref, gf_ref, gtf_ref, qb_ref, ktb_ref, vb_ref, gb_ref, gtb_ref,
                  hf_ref, hb_ref, c_ref, m_ref):
    @pl.when(pl.program_id(0) == 0)
    def _():
        c_ref[...] = jnp.zeros(c_ref.shape, F32)
        m_ref[...] = jnp.zeros(m_ref.shape, F32)

    row = lax.broadcasted_iota(jnp.int32, (CHUNK, CHUNK), 0)
    col = lax.broadcasted_iota(jnp.int32, (CHUNK, CHUNK), 1)
    ones_col = jnp.ones((CHUNK, MLSTM_DH), BF16)
    nh = MLSTM_HEADS
    dirs = ((qf_ref, ktf_ref, vf_ref, gf_ref, gtf_ref, hf_ref, col <= row, CHUNK - 1),
            (qb_ref, ktb_ref, vb_ref, gb_ref, gtb_ref, hb_ref, col >= row, 0))
    heads = []
    for d, (q_ref, kt_ref, v_ref, g_ref, gt_ref, h_ref, seen, end_row) in enumerate(dirs):
        for b in range(BATCH):
            g = g_ref[b]
            gt = gt_ref[b]
            for h in range(nh):
                lane = d * nh + h
                hd = slice(h * MLSTM_DH, (h + 1) * MLSTM_DH)
                heads.append(dict(
                    idx=(d * BATCH + b) * nh + h, seen=seen, end_row=end_row, out=(h_ref, b, hd), g=g, lane=lane,
                    cum=g[:, lane:lane + 1], rmax=g[:, GS_RMAX + lane:GS_RMAX + lane + 1], r=gt[lane:lane + 1, :],
                    q=q_ref[b, :, hd], kt=kt_ref[b, hd, :], v=v_ref[b, :, hd]))

    for hd in heads:
        hd["qk"] = _dot(hd["q"], hd["kt"])
        hd["m_prev"] = m_ref[hd["idx"], 0:1, 0:1]
        hd["c_old"] = c_ref[hd["idx"]]
        hd["v_aug"] = jnp.concatenate([hd["v"], ones_col], axis=1)
    for hd in heads:
        e = hd["end_row"]
        top = jnp.maximum(hd["m_prev"], hd["rmax"][e:e + 1, :])
        w_row = jnp.exp(hd["r"] - top)
        ktw = (hd["kt"].astype(F32) * w_row).astype(BF16)
        c_ref[hd["idx"]] = jnp.exp(hd["m_prev"] - top) * hd["c_old"] + _dot(ktw, hd["v_aug"])
        m_ref[hd["idx"]] = jnp.broadcast_to(hd["cum"][e:e + 1, :] + top, m_ref.shape[1:])
    lane_row = lax.broadcasted_iota(jnp.int32, (1, LANES), 1)
    for k0 in range(0, len(heads), nh):
        group = heads[k0:k0 + nh]
        g = group[0]["g"]
        m_row = jnp.zeros((1, LANES), F32)
        for hd in group:
            m_row = jnp.where(lane_row == GS_RMAX + hd["lane"], hd["m_prev"], m_row)
        top_all = jnp.maximum(m_row, g)
        w_all = jnp.exp(m_row - top_all)
        floor_all = jnp.exp(-(pltpu.roll(g, GS_RMAX, 1) + top_all))
        for hd in group:
            ln = GS_RMAX + hd["lane"]
            hd["top"], hd["w"], hd["floor"] = (t[:, ln:ln + 1] for t in (top_all, w_all, floor_all))
    for hd in heads:
        top = jnp.broadcast_to(hd["top"], (CHUNK, CHUNK))
        s = hd["qk"] * jnp.exp(jnp.where(hd["seen"], hd["r"] - top, -jnp.inf))
        qw = hd["q"].astype(F32) * hd["w"]
        lhs = jnp.concatenate([s.astype(BF16), qw.astype(BF16)], axis=1)
        rhs = jnp.concatenate([hd["v_aug"], hd["c_old"].astype(BF16)], axis=0)
        hd["na"] = _dot(lhs, rhs)
    for hd in heads:
        den = jnp.maximum(jnp.abs(hd["na"][:, MLSTM_DH:]), hd["floor"])
        h_ref, b, cols = hd["out"]
        h_ref[b, :, cols] = hd["na"][:, :MLSTM_DH] / den


def _mlstm_call(q, kt, v, g, gt):
    fwd = lambda s: s
    bwd = lambda s: jnp.where(s < CTX_CH, CTX_CH - 1 - s, NCH_B + CTX_CH - 1 - s)
    q3 = q.reshape(BATCH, SEQ_B, MIX_W)
    kt3 = kt.reshape(BATCH, SEQ_B * MLSTM_HEADS, MLSTM_DH)
    v3 = v.reshape(BATCH, SEQ_B, MIX_W)
    g3 = g.reshape(BATCH, SEQ_B, LANES)
    gt3 = gt.reshape(BATCH, NCH_B * SUBLANES, CHUNK)

    def specs(order):
        return [
            pl.BlockSpec((BATCH, CHUNK, MIX_W), lambda s: (0, order(s), 0)),
            pl.BlockSpec((BATCH, CHUNK * MLSTM_HEADS, MLSTM_DH), lambda s: (0, order(s), 0)),
            pl.BlockSpec((BATCH, CHUNK, MIX_W), lambda s: (0, order(s), 0)),
            pl.BlockSpec((BATCH, CHUNK, LANES), lambda s: (0, order(s), 0)),
            pl.BlockSpec((BATCH, SUBLANES, CHUNK), lambda s: (0, order(s), 0)),
        ]

    n_state = 2 * BATCH * MLSTM_HEADS
    hf, hb = pl.pallas_call(
        _mlstm_kernel,
        grid=(NCH_B,),
        in_specs=specs(fwd) + specs(bwd),
        out_specs=[
            pl.BlockSpec((BATCH, CHUNK, MIX_W), lambda s: (0, fwd(s), 0)),
            pl.BlockSpec((BATCH, CHUNK, MIX_W), lambda s: (0, bwd(s), 0)),
        ],
        out_shape=[jax.ShapeDtypeStruct((BATCH, SEQ_B, MIX_W), F32)] * 2,
        scratch_shapes=[
            pltpu.VMEM((n_state, MLSTM_DH, 2 * MLSTM_DH), F32),
            pltpu.VMEM((n_state, SUBLANES, LANES), F32),
        ],
        compiler_params=_params(("arbitrary",)),
        name="mlstm",
    )(q3, kt3, v3, g3, gt3, q3, kt3, v3, g3, gt3)
    return hf.reshape(T_ALL, MIX_W), hb.reshape(T_ALL, MIX_W)


VEC_GMLP_G, VEC_GMLP_B, VEC_CONV_B, VEC_CONV_G, VEC_CONV_LB, VEC_MLSTM_G, VEC_POOL_S = range(7)
VEC_ROWS = 8


def _mix_kernel(x_ref, prev_ref, next_ref, mod_ref, hf_ref, hb_ref, w_ref, b_ref, vec_ref, ws_ref, bst_ref,
                cw_ref, pw_ref, wbr_ref, wout_ref, g_ref, beta_ref, o_ref, xe_ref, a_ref, d_ref, ash_ref, conv_ref,
                gate_ref, pa_ref, po_ref, u_ref, vn_ref, yc_ref, yd_ref):
    j = pl.program_id(0) % NT_B
    is_ctx = j == 0
    first = jnp.logical_or(is_ctx, j == 1)
    last = jnp.logical_or(is_ctx, j == NT_B - 1)
    shift = mod_ref[3:4, :]
    scale = mod_ref[4:5, :]
    gate = mod_ref[5:6, :]

    def modulate(v):
        return (v * (1.0 + scale) + shift).astype(BF16)

    def vec(r):
        return vec_ref[r:r + 1, :]

    def proj(ext, lo, width, zero=None):
        lhs = xe_ref[...] if ext else xe_ref[HALO:HALO + TM, :]
        bias = lambda c: b_ref[:, c:c + MXU_N] if zero is None else b_ref[:, c:c + MXU_N] + zero
        return jnp.concatenate([_dot(lhs, w_ref[:, c:c + MXU_N]) + bias(c)
                                for c in range(lo, lo + width, MXU_N)], axis=1)

    xe_ref[0:HALO, :] = modulate(prev_ref[...])
    xe_ref[HALO:HALO + TM, :] = modulate(x_ref[...])
    xe_ref[HALO + TM:2 * HALO + TM, :] = modulate(next_ref[...])
    rows = lax.broadcasted_iota(jnp.int32, (TM + 2 * HALO, 1), 0)
    valid = jnp.logical_and(jnp.logical_or(rows >= HALO, jnp.logical_not(first)),
                            jnp.logical_or(rows < HALO + TM, jnp.logical_not(last)))

    pb = proj(True, MX_B, 2 * MIX_W)
    a_ref[...] = jnp.where(valid, pb[:, :MIX_W] * _sigmoid(pb[:, MIX_W:]), 0.0)
    pa_ref[...] = proj(False, MX_A, 2 * MIX_W)

    gate_chunks = [(i, n) for i in range(N_BRANCH) for n in range(D_MODEL // MXU_N)]
    sh_rows = TM + 2 * HALO - SUBLANES

    def zero_of(v):
        bits = lax.bitcast_convert_type(v[0:1, 0:LANES], jnp.uint32)
        return lax.shift_right_logical(lax.shift_right_logical(bits, jnp.uint32(16)), jnp.uint32(16)).astype(F32)

    def conv_block(cb, rb, z):
        cols = slice(cb * LANES, (cb + 1) * LANES)
        part = None
        for k in range(CONV_W):
            q8, s = divmod(HALO - CONV_W // 2 + k, SUBLANES)
            r0 = q8 * SUBLANES + rb * CONV_RB
            src = a_ref[r0:r0 + CONV_RB, cols] if s == 0 else ash_ref[s - 1, r0:r0 + CONV_RB, cols]
            term = (cw_ref[k:k + 1, cols] + z) * src
            part = term if part is None else part + term
        conv_ref[rb * CONV_RB:(rb + 1) * CONV_RB, cols] = part
        return part

    conv_blocks = [(cb, rb) for cb in range(MIX_W // LANES) for rb in range(TM // CONV_RB)]
    per_step = len(conv_blocks) // len(gate_chunks)
    zero_row = jnp.zeros((1, LANES), F32)
    v_zero = [zero_row] * len(gate_chunks)
    for step, (i, n) in enumerate(gate_chunks):
        lo = i * D_MODEL + n * MXU_N
        z_in = jnp.concatenate([v_zero[step - CHAIN_LAG]] * (MXU_N // LANES), axis=1) if step >= CHAIN_LAG else None
        gv = jnp.tanh(0.5 * proj(False, MX_G + lo, MXU_N, z_in))
        gate_ref[:, lo:lo + MXU_N] = gv
        z = zero_of(gv) + v_zero[step - 1]
        for cb, rb in conv_blocks[step * per_step:(step + 1) * per_step]:
            if rb == 0:
                cols = slice(cb * LANES, (cb + 1) * LANES)
                for s in range(1, SUBLANES):
                    ash_ref[s - 1, 0:sh_rows, cols] = a_ref[s:s + sh_rows, cols]
            z = zero_of(conv_block(cb, rb, z))
        v_zero[step] = z

    pa = _gelu_tanh(pa_ref[...])
    u_ref[...] = pa[:, :MIX_W]
    vn_ref[...] = _ln(pa[:, MIX_W:], vec(VEC_GMLP_G), vec(VEC_GMLP_B)).astype(BF16)

    d_ref[0:TM + 2 * HALO, :] = jnp.where(valid, proj(True, MX_D, MIX_W), 0.0)
    d_ref[TM + 2 * HALO:TM + 2 * HALO + SUBLANES, :] = jnp.zeros((SUBLANES, MIX_W), F32)
    pos = jnp.where(is_ctx, 0, (j - 1) * TM) + lax.broadcasted_iota(jnp.int32, (TM, 1), 0)
    n_seq = jnp.where(is_ctx, CTX_LEN, SEQ)

    def window_sum(cols, win):
        start = HALO - win // 2
        if win < SUBLANES:
            ws = d_ref[start:start + TM, cols]
            for k in range(1, win):
                ws = ws + d_ref[start + k:start + k + TM, cols]
            return ws
        base = start // SUBLANES * SUBLANES
        off = start - base
        up8 = lambda r: -(-r // SUBLANES) * SUBLANES
        rows = [up8(TM + off)]
        n = win
        while n > 1:
            n //= 2
            rows.append(up8(rows[-1] + n))
        rows.reverse()
        level = d_ref[base:base + rows[1], cols] + d_ref[base + 1:base + 1 + rows[1], cols]
        n, i = 2, 1
        while n < win:
            i += 1
            level = level[0:rows[i], :] + level[n:n + rows[i], :]
            n *= 2
        return level[off:off + TM, :]

    grp = lax.broadcasted_iota(jnp.int32, (1, LANES), 1)
    lo_row = jnp.zeros((1, LANES), jnp.int32)
    hi_row = jnp.zeros((1, LANES), jnp.int32)
    for gi, win in enumerate(POOL_WINDOWS):
        lo_row = jnp.where(grp == gi, win // 2, lo_row)
        hi_row = jnp.where(grp == gi, win - 1 - win // 2, hi_row)
    cnt_all = (jnp.minimum(pos + hi_row + 1, n_seq) - jnp.maximum(pos - lo_row, 0)).astype(F32)
    inv_cnt = 1.0 / cnt_all
    for gi, win in enumerate(POOL_WINDOWS):
        cols = slice(gi * POOL_GD, (gi + 1) * POOL_GD)
        wsum = window_sum(cols, win)
        diff = wsum * inv_cnt[:, gi:gi + 1] - d_ref[HALO:HALO + TM, cols]
        yd_ref[:, cols] = (_dot(diff.astype(BF16), pw_ref[gi]) * vec_ref[VEC_POOL_S:VEC_POOL_S + 1, cols]
                           ).astype(BF16)

    po_ref[...] = proj(False, MX_O, MIX_W)
    for h in range(MLSTM_HEADS):
        cols = slice(h * MLSTM_DH, (h + 1) * MLSTM_DH)
        hh = hf_ref[:, cols] + hb_ref[:, cols]
        mu = jnp.mean(hh, axis=-1, keepdims=True)
        hc = hh - mu
        hn = hc * lax.rsqrt(jnp.mean(hc * hc, axis=-1, keepdims=True) + LN_EPS)
        yc_ref[:, cols] = (_sigmoid(po_ref[:, cols]) * (hn * vec_ref[VEC_MLSTM_G:VEC_MLSTM_G + 1, cols])
                           ).astype(BF16)

    gd = MIX_W // GMLP_GROUPS
    z_rows = []
    for c in range(CH_T):
        z_cols = []
        for gi in range(GMLP_GROUPS):
            blk = vn_ref[c * CHUNK:(c + 1) * CHUNK, gi * gd:(gi + 1) * gd]
            z_cols.append(_dot(ws_ref[gi], blk) + bst_ref[:, gi:gi + 1])
        z_rows.append(jnp.concatenate(z_cols, axis=1))
    ya = (u_ref[...] * jnp.concatenate(z_rows, axis=0)).astype(BF16)
    yb = _silu(_ln(conv_ref[...] + vec(VEC_CONV_B), vec(VEC_CONV_G), vec(VEC_CONV_LB))).astype(BF16)
    yc = yc_ref[...]
    yd = yd_ref[...]

    merged = []
    for n in range(D_MODEL // MXU_N):
        cols = slice(n * MXU_N, (n + 1) * MXU_N)
        acc = None
        for i, yi in enumerate((ya, yb, yc, yd)):
            lo = i * D_MODEL + n * MXU_N
            p = _dot(yi, wbr_ref[i, :, cols])
            term = gate_ref[:, lo:lo + MXU_N] * p + p
            acc = term if acc is None else acc + term
        merged.append((0.5 * acc).astype(BF16))
    merged = jnp.concatenate(merged, axis=1)
    y = jnp.concatenate([_dot(merged, wout_ref[:, n * MXU_N:(n + 1) * MXU_N]) for n in range(D_MODEL // MXU_N)],
                        axis=1)
    o_ref[...] = _ln(ALPHA * x_ref[...] + gate * y, g_ref[...], beta_ref[...])


def _mix_call(s, mods, hf, hb, w_mx, b_mx, vec512, gmlp_ws, gmlp_bst, conv_w, pool_w, w_branch, w_out,
              ln_g, ln_b, *, layer):
    hb_per_tile = TM // HALO
    n_hb = T_ALL // HALO
    return pl.pallas_call(
        _mix_kernel,
        grid=(NT,),
        in_specs=[
            pl.BlockSpec((TM, D_MODEL), lambda i: (i, 0)),
            pl.BlockSpec((HALO, D_MODEL), lambda i: (jnp.maximum(i * hb_per_tile - 1, 0), 0)),
            pl.BlockSpec((HALO, D_MODEL), lambda i: (jnp.minimum((i + 1) * hb_per_tile, n_hb - 1), 0)),
            pl.BlockSpec((None, 9, D_MODEL), lambda i: (layer * MOD_ROWS + _mod_row(i), 0, 0)),
            pl.BlockSpec((TM, MIX_W), lambda i: (i, 0)),
            pl.BlockSpec((TM, MIX_W), lambda i: (i, 0)),
            _const_spec((None, D_MODEL, MX_W), (layer, 0, 0)),
            _const_spec((None, 1, MX_COLS), (layer, 0, 0)),
            _const_spec((None, VEC_ROWS, MIX_W), (layer, 0, 0)),
            _const_spec((None, GMLP_GROUPS, CHUNK, CHUNK), (layer, 0, 0, 0)),
            _const_spec((None, CHUNK, GMLP_GROUPS), (layer, 0, 0)),
            _const_spec((None, CONV_W, MIX_W), (layer, 0, 0)),
            _const_spec((None, len(POOL_WINDOWS), POOL_GD, POOL_GD), (layer, 0, 0, 0)),
            _const_spec((None, N_BRANCH, MIX_W, D_MODEL + LANE_PAD), (layer, 0, 0, 0)),
            _const_spec((None, D_MODEL, D_MODEL + LANE_PAD), (layer, 0, 0)),
            _const_spec((None, 1, D_MODEL), (layer * 3 + 1, 0, 0)),
            _const_spec((None, 1, D_MODEL), (layer * 3 + 1, 0, 0)),
        ],
        out_specs=pl.BlockSpec((TM, D_MODEL), lambda i: (i, 0)),
        out_shape=jax.ShapeDtypeStruct((T_ALL, D_MODEL), F32),
        scratch_shapes=[
            pltpu.VMEM((TM + 2 * HALO, D_MODEL), BF16),
            pltpu.VMEM((TM + 2 * HALO, MIX_W), F32),
            pltpu.VMEM((TM + 2 * HALO + SUBLANES, MIX_W), F32),
            pltpu.VMEM((SUBLANES - 1, TM + 2 * HALO, MIX_W), F32),
            pltpu.VMEM((TM, MIX_W), F32),
            pltpu.VMEM((TM, N_BRANCH * D_MODEL), F32),
            pltpu.VMEM((TM, 2 * MIX_W), F32),
            pltpu.VMEM((TM, MIX_W), F32),
            pltpu.VMEM((TM, MIX_W), F32),
            pltpu.VMEM((TM, MIX_W), BF16),
            pltpu.VMEM((TM, MIX_W), BF16),
            pltpu.VMEM((TM, MIX_W), BF16),
        ],
        compiler_params=_params(("parallel",)),
        name="mix",
    )(s, s, s, mods, hf, hb, w_mx, b_mx, vec512, gmlp_ws, gmlp_bst, conv_w, pool_w, w_branch, w_out, ln_g, ln_b)


MX_TILE = 512
MX_PAD = MX_TILE
MX_W = MX_COLS + MX_PAD
CP_TILE = 256


def _mx_src(j):
    a, o, d = (MX_O - MX_A) // MX_TILE, (MX_D - MX_A) // MX_TILE, (MX_G - MX_A) // MX_TILE
    g_end = MX_COLS // MX_TILE
    return jnp.where(j < a, OFF_A + MX_TILE * j,
                     jnp.where(j < o, OFF_C_O + MX_TILE * (j - a),
                               jnp.where(j < d, OFF_D + MX_TILE * (j - o),
                                         jnp.where(j < g_end, OFF_G + MX_TILE * (j - d), 0))))


def _prep_mx_kernel(wt_ref, o_ref):
    j = pl.program_id(1)

    @pl.when(j < MX_COLS // MX_TILE)
    def _():
        o_ref[...] = wt_ref[0].T.astype(BF16)

    @pl.when(j >= MX_COLS // MX_TILE)
    def _():
        o_ref[...] = jnp.zeros(o_ref.shape, BF16)


def _prep_cp_kernel(wt_ref, o_ref):
    j = pl.program_id(1)
    n_plain = (CP_GI - CP_QK) // CP_TILE

    @pl.when(j < n_plain)
    def _():
        o_ref[...] = wt_ref[0].T.astype(BF16)

    @pl.when(j == n_plain)
    def _():
        nh = MLSTM_HEADS
        t = wt_ref[0, 0:LANES, :].T
        lane = lax.broadcasted_iota(jnp.int32, t.shape, 1)
        up1 = pltpu.roll(t, LANES - nh, 1)
        up2 = pltpu.roll(t, LANES - 2 * nh, 1)
        gi = jnp.where(lane < nh, t, jnp.where(lane < 2 * nh, up1, 0.0))
        gf = jnp.where(lane < nh, up1, jnp.where(lane < 2 * nh, up2, 0.0))
        o_ref[:, 0:LANES] = gi.astype(BF16)
        o_ref[:, LANES:2 * LANES] = gf.astype(BF16)


def _prep_win_call(w_in):
    wt = jnp.swapaxes(w_in, 1, 2)
    n_plain = (CP_GI - CP_QK) // CP_TILE
    w_mx = pl.pallas_call(
        _prep_mx_kernel,
        grid=(DEPTH, MX_W // MX_TILE),
        in_specs=[pl.BlockSpec((pl.Element(1), pl.Element(MX_TILE), pl.Element(D_MODEL)),
                               lambda l, j: (l, pl.multiple_of(_mx_src(j), SUBLANES), 0))],
        out_specs=pl.BlockSpec((None, D_MODEL, MX_TILE), lambda l, j: (l, 0, j)),
        out_shape=jax.ShapeDtypeStruct((DEPTH, D_MODEL, MX_W), BF16),
        compiler_params=_params(("parallel", "parallel")),
        name="prep_mx",
    )(wt)
    w_cp = pl.pallas_call(
        _prep_cp_kernel,
        grid=(DEPTH, CP_COLS // CP_TILE),
        in_specs=[pl.BlockSpec((pl.Element(1), pl.Element(CP_TILE), pl.Element(D_MODEL)),
                               lambda l, j: (l, pl.multiple_of(
                                   jnp.where(j < n_plain, OFF_C + CP_TILE * j, OFF_C_GATES), SUBLANES), 0))],
        out_specs=pl.BlockSpec((None, D_MODEL, CP_TILE), lambda l, j: (l, 0, j)),
        out_shape=jax.ShapeDtypeStruct((DEPTH, D_MODEL, CP_COLS), BF16),
        compiler_params=_params(("parallel", "parallel")),
        name="prep_cp",
    )(wt)
    return w_mx, w_cp


def _pad_cast_kernel(w_ref, o_ref):
    cols = w_ref.shape[1]
    o_ref[:, 0:cols] = w_ref[...].astype(BF16)
    o_ref[:, cols:] = jnp.zeros((o_ref.shape[0], o_ref.shape[1] - cols), BF16)


def _pad_cast_call(w):
    lead, cols = w.shape[:-1], w.shape[-1]
    rows = math.prod(lead)
    rb = 512
    out = pl.pallas_call(
        _pad_cast_kernel,
        grid=(rows // rb,),
        in_specs=[pl.BlockSpec((rb, cols), lambda i: (i, 0))],
        out_specs=pl.BlockSpec((rb, cols + LANE_PAD), lambda i: (i, 0)),
        out_shape=jax.ShapeDtypeStruct((rows, cols + LANE_PAD), BF16),
        compiler_params=_params(("parallel",)),
        name="pad_cast",
    )(w.reshape(rows, cols))
    return out.reshape(lead + (cols + LANE_PAD,))


def _repack_cols(w, pieces):
    parts = [jnp.zeros(w.shape[:-1] + (p,), w.dtype) if isinstance(p, int) else w[..., p[0]:p[1]] for p in pieces]
    return jnp.concatenate(parts, axis=-1)


def kernel(x, c, ctx, c_ctx, w_ada, b_ada, ln_g, ln_b, ffn_w_in, ffn_w_out, w_in, b_in, gmlp_ln_g, gmlp_ln_b,
           gmlp_ws, gmlp_bs, conv_w, conv_b, conv_ln_g, conv_ln_b, qk_conv_w, mlstm_ln_g, pool_w, pool_scale,
           w_branch, w_out):
    mix_pieces = [(OFF_A, OFF_C), (OFF_C_O, OFF_D), (OFF_D, OFF_G), (OFF_G, IN_COLS)]
    nh = MLSTM_HEADS
    gate_pad = LANES - 2 * nh

    def cproj_cols(w):
        g = w[..., OFF_C_GATES:OFF_C_O].reshape(w.shape[:-1] + (4, nh))
        pad = jnp.zeros(w.shape[:-1] + (gate_pad,), w.dtype)
        return jnp.concatenate([w[..., OFF_C:OFF_C_GATES], g[..., 0, :], g[..., 2, :], pad,
                                g[..., 1, :], g[..., 3, :], pad], axis=-1)
    w_mx, w_cp = _prep_win_call(w_in)
    b_mx = _repack_cols(b_in, mix_pieces).reshape(DEPTH, 1, MX_COLS)
    b_cp = cproj_cols(b_in).reshape(DEPTH, 1, CP_COLS)
    ffn_w_in_b = ffn_w_in.astype(BF16)
    ffn_w_out_b = _pad_cast_call(ffn_w_out)
    w_branch_b = _pad_cast_call(w_branch)
    w_out_b = _pad_cast_call(w_out)
    gmlp_ws_b = gmlp_ws.astype(BF16)
    pool_w_b = pool_w.astype(BF16)
    gmlp_bst = jnp.swapaxes(gmlp_bs, 1, 2)
    vec512 = jnp.stack([gmlp_ln_g, gmlp_ln_b, conv_b, conv_ln_g, conv_ln_b, mlstm_ln_g, pool_scale,
                        jnp.zeros_like(pool_scale)], axis=1)
    ln_g3 = ln_g.reshape(DEPTH * 3, 1, D_MODEL)
    ln_b3 = ln_b.reshape(DEPTH * 3, 1, D_MODEL)
    c_t = jnp.concatenate([c, c_ctx[None], jnp.zeros((MOD_ROWS - BATCH - 1, D_MODEL), F32)], axis=0).T

    mods = _mods_call(c_t, w_ada, b_ada).reshape(DEPTH * MOD_ROWS, 9, D_MODEL)
    s = _embed_call(x, ctx)
    for l in range(DEPTH):
        last = l == DEPTH - 1
        s = _ffn_call(s, mods, ffn_w_in_b, ffn_w_out_b, ln_g3, ln_b3, layer=l, which=0, final=False)
        q, kt, v, g, gt = _cproj_call(s, mods, w_cp, b_cp, qk_conv_w, layer=l)
        hf, hb = _mlstm_call(q, kt, v, g, gt)
        s = _mix_call(s, mods, hf, hb, w_mx, b_mx, vec512, gmlp_ws_b, gmlp_bst, conv_w, pool_w_b, w_branch_b,
                      w_out_b, ln_g3, ln_b3, layer=l)
        s = _ffn_call(s, mods, ffn_w_in_b, ffn_w_out_b, ln_g3, ln_b3, layer=l, which=1, final=last)
    return s.reshape(BATCH, SEQ, D_MODEL)
```

```python
import functools
import math

import jax
import jax.numpy as jnp
from jax import lax
from jax.experimental import pallas as pl
from jax.experimental.pallas import tpu as pltpu

D_MODEL = 1024
BATCH = 2
SEQ = 8192
DEPTH = 4
GRID_W = 64
CTX_LEN = 256
MIX_W = D_MODEL // 2
N_BRANCH = 4
CHUNK = 128
GMLP_GROUPS = 4
CONV_W = 31
MLSTM_HEADS = 4
MLSTM_DH = MIX_W // MLSTM_HEADS
QK_CONV = 3
POOL_WINDOWS = (2, 4, 8, 16)
POOL_GD = MIX_W // len(POOL_WINDOWS)
D_FF = 128 * ((8 * D_MODEL // 3 + 127) // 128)
ALPHA = (2 * DEPTH) ** 0.25
LN_EPS = 1e-6
FFN_RES = 0.5

OFF_A = 0
OFF_B = OFF_A + 2 * MIX_W
OFF_C = OFF_B + 2 * MIX_W
OFF_C_GATES = OFF_C + 3 * MIX_W
OFF_C_O = OFF_C_GATES + 4 * MLSTM_HEADS
OFF_D = OFF_C_O + MIX_W
OFF_G = OFF_D + MIX_W
IN_COLS = OFF_G + N_BRANCH * D_MODEL

LANES = 128
SUBLANES = 8
TM = 256
SEQ_B = CTX_LEN + SEQ
T_ALL = BATCH * SEQ_B
NT_B = SEQ_B // TM
NT = BATCH * NT_B
NCH_B = SEQ_B // CHUNK
CH_T = TM // CHUNK
CTX_CH = CTX_LEN // CHUNK
HALO = 16
MXU_N = 256
LANE_PAD = LANES
FF_CH = MXU_N
FFN_TILES = 2
CONV_RB = 64
CHAIN_LAG = 3
GS_RMAX = 2 * MLSTM_HEADS
MOD_ROWS = 8
CTX_MOD_ROW = BATCH
VMEM_LIMIT = 56 * 1024 * 1024

MX_A = 0
MX_B = MX_A + 2 * MIX_W
MX_O = MX_B + 2 * MIX_W
MX_D = MX_O + MIX_W
MX_G = MX_D + MIX_W
MX_COLS = MX_G + N_BRANCH * D_MODEL
CP_QK = 0
CP_V = 2 * MIX_W
CP_GI = 3 * MIX_W
CP_GF = CP_GI + LANES
CP_COLS = CP_GF + LANES

F32 = jnp.float32
BF16 = jnp.bfloat16


def _dot(a, b):
    return jnp.dot(a, b, preferred_element_type=F32)


def _dot_f32(a, b):
    return jnp.dot(a, b, preferred_element_type=F32, precision=lax.Precision.HIGHEST)


def _ln(r, g, b):
    mu = jnp.mean(r, axis=-1, keepdims=True)
    xc = r - mu
    var = jnp.mean(xc * xc, axis=-1, keepdims=True)
    return xc * lax.rsqrt(var + LN_EPS) * g + b


def _sigmoid(x):
    return 0.5 * jnp.tanh(0.5 * x) + 0.5


def _silu(x):
    hx = 0.5 * x
    return hx * jnp.tanh(hx) + hx


def _gelu_tanh(x):
    c = math.sqrt(2.0 / math.pi)
    hx = 0.5 * x
    return hx * jnp.tanh(x * (c + (c * 0.044715) * (x * x))) + hx


def _mod_row(i):
    return jnp.where(i % NT_B == 0, CTX_MOD_ROW, i // NT_B)


def _const_spec(block, index):
    return pl.BlockSpec(block, lambda *_: index, pipeline_mode=pl.Buffered(1))


def _params(sem):
    return pltpu.CompilerParams(dimension_semantics=sem, vmem_limit_bytes=VMEM_LIMIT)


def _mods_kernel(ct_ref, w_ref, b_ref, o_ref):
    s = _silu(ct_ref[...])
    w = w_ref[...]
    o_ref[...] = jnp.zeros(o_ref.shape, F32)
    for r in range(BATCH + 1):
        o_ref[r:r + 1, :] = jnp.sum(s[:, r:r + 1] * w, axis=0, keepdims=True) + b_ref[...]


def _mods_call(c_t, w_ada, b_ada):
    tn = D_MODEL
    n_col = w_ada.shape[-1] // tn
    return pl.pallas_call(
        _mods_kernel,
        grid=(DEPTH, n_col),
        in_specs=[
            pl.BlockSpec((D_MODEL, MOD_ROWS), lambda l, n: (0, 0)),
            pl.BlockSpec((None, D_MODEL, tn), lambda l, n: (l, 0, n)),
            pl.BlockSpec((None, 1, tn), lambda l, n: (l, 0, n)),
        ],
        out_specs=pl.BlockSpec((None, MOD_ROWS, tn), lambda l, n: (l, 0, n)),
        out_shape=jax.ShapeDtypeStruct((DEPTH, MOD_ROWS, w_ada.shape[-1]), F32),
        compiler_params=_params(("parallel", "parallel")),
        name="mods",
    )(c_t, w_ada, b_ada.reshape(DEPTH, 1, -1))


def _embed_kernel(x_ref, ctx_ref, o_ref):
    j = pl.program_id(0) % NT_B

    @pl.when(j == 0)
    def _():
        o_ref[...] = ctx_ref[...]

    @pl.when(j > 0)
    def _():
        quarter = D_MODEL // 4
        grid_rows = TM // GRID_W
        k = lax.broadcasted_iota(jnp.int32, (1, quarter), 1).astype(F32)
        freqs = jnp.exp(-math.log(10000.0) * k / quarter)
        r = ((j - 1) * grid_rows + lax.broadcasted_iota(jnp.int32, (SUBLANES, 1), 0)).astype(F32)
        col = lax.broadcasted_iota(jnp.int32, (GRID_W, 1), 0).astype(F32)
        er = r * freqs
        ec = col * freqs
        sin_r, cos_r, sin_c, cos_c = jnp.sin(er), jnp.cos(er), jnp.sin(ec), jnp.cos(ec)
        for q in range(grid_rows):
            rows = slice(q * GRID_W, (q + 1) * GRID_W)
            o_ref[rows, 0 * quarter:1 * quarter] = x_ref[rows, 0 * quarter:1 * quarter] + sin_r[q:q + 1, :]
            o_ref[rows, 1 * quarter:2 * quarter] = x_ref[rows, 1 * quarter:2 * quarter] + cos_r[q:q + 1, :]
            o_ref[rows, 2 * quarter:3 * quarter] = x_ref[rows, 2 * quarter:3 * quarter] + sin_c
            o_ref[rows, 3 * quarter:4 * quarter] = x_ref[rows, 3 * quarter:4 * quarter] + cos_c


def _embed_call(x, ctx):
    lat_tiles_b = SEQ // TM
    return pl.pallas_call(
        _embed_kernel,
        grid=(NT,),
        in_specs=[
            pl.BlockSpec((TM, D_MODEL), lambda i: ((i // NT_B) * lat_tiles_b + jnp.maximum(i % NT_B - 1, 0), 0)),
            pl.BlockSpec((TM, D_MODEL), lambda i: (i // NT_B, 0)),
        ],
        out_specs=pl.BlockSpec((TM, D_MODEL), lambda i: (i, 0)),
        out_shape=jax.ShapeDtypeStruct((T_ALL, D_MODEL), F32),
        compiler_params=_params(("parallel",)),
        name="embed",
    )(x.reshape(BATCH * SEQ, D_MODEL), ctx.reshape(BATCH * CTX_LEN, D_MODEL))


def _ffn_kernel(*refs, sub):
    x_refs = refs[:FFN_TILES]
    mod_refs = refs[FFN_TILES:2 * FFN_TILES]
    win_ref, wout_ref, g_ref, b_ref, o_ref, xm_ref, h_ref = refs[2 * FFN_TILES:]
    for t, (x_ref, mod_ref) in enumerate(zip(x_refs, mod_refs)):
        shift = mod_ref[3 * sub + 0:3 * sub + 1, :]
        scale = mod_ref[3 * sub + 1:3 * sub + 2, :]
        xm_ref[t * TM:(t + 1) * TM, :] = (x_ref[...] * (1.0 + scale) + shift).astype(BF16)
    xm = xm_ref[...]
    for c in range(D_FF // FF_CH):
        a1 = _dot(xm, win_ref[:, c * FF_CH:(c + 1) * FF_CH])
        a2 = _dot(xm, win_ref[:, D_FF + c * FF_CH:D_FF + (c + 1) * FF_CH])
        h_ref[:, c * FF_CH:(c + 1) * FF_CH] = (_silu(a1) * a2).astype(BF16)
    for t, (x_ref, mod_ref) in enumerate(zip(x_refs, mod_refs)):
        hid = h_ref[t * TM:(t + 1) * TM, :]
        y = jnp.concatenate([_dot(hid, wout_ref[:, n * MXU_N:(n + 1) * MXU_N]) for n in range(D_MODEL // MXU_N)],
                            axis=1)
        gate = mod_ref[3 * sub + 2:3 * sub + 3, :]
        r = ALPHA * x_ref[...] + (FFN_RES * gate) * y
        o_ref[t * TM:(t + 1) * TM, :] = _ln(r, g_ref[...], b_ref[...])


def _ffn_call(s, mods, ffn_w_in, ffn_w_out, ln_g, ln_b, *, layer, which, final):
    sub = 2 * which
    if final:
        lat_tiles_b = SEQ // TM
        out_tiles = BATCH * lat_tiles_b
        in_tile = lambda t: (t // lat_tiles_b) * NT_B + 1 + t % lat_tiles_b
    else:
        out_tiles = NT
        in_tile = lambda t: t
    tile_of = lambda i, t: in_tile(i * FFN_TILES + t)
    x_specs = [pl.BlockSpec((TM, D_MODEL), functools.partial(lambda i, t: (tile_of(i, t), 0), t=t))
               for t in range(FFN_TILES)]
    mod_specs = [pl.BlockSpec((None, 9, D_MODEL),
                              functools.partial(lambda i, t: (layer * MOD_ROWS + _mod_row(tile_of(i, t)), 0, 0), t=t))
                 for t in range(FFN_TILES)]
    return pl.pallas_call(
        functools.partial(_ffn_kernel, sub=sub),
        grid=(out_tiles // FFN_TILES,),
        in_specs=x_specs + mod_specs + [
            _const_spec((None, None, D_MODEL, 2 * D_FF), (layer, which, 0, 0)),
            _const_spec((None, None, D_FF, D_MODEL + LANE_PAD), (layer, which, 0, 0)),
            _const_spec((None, 1, D_MODEL), (layer * 3 + sub, 0, 0)),
            _const_spec((None, 1, D_MODEL), (layer * 3 + sub, 0, 0)),
        ],
        out_specs=pl.BlockSpec((FFN_TILES * TM, D_MODEL), lambda i: (i, 0)),
        out_shape=jax.ShapeDtypeStruct((out_tiles * TM, D_MODEL), F32),
        scratch_shapes=[
            pltpu.VMEM((FFN_TILES * TM, D_MODEL), BF16),
            pltpu.VMEM((FFN_TILES * TM, D_FF), BF16),
        ],
        compiler_params=_params(("parallel",)),
        name=f"ffn{which}",
    )(*([s] * FFN_TILES), *([mods] * FFN_TILES), ffn_w_in, ffn_w_out, ln_g, ln_b)


def _cproj_kernel(x_ref, prev_ref, next_ref, mod_ref, w_ref, b_ref, cw_ref,
                  q_ref, kt_ref, v_ref, g_ref, gt_ref, xe_ref, p_ref):
    j = pl.program_id(0) % NT_B
    first = jnp.logical_or(j == 0, j == 1)
    last = jnp.logical_or(j == 0, j == NT_B - 1)
    shift = mod_ref[3:4, :]
    scale = mod_ref[4:5, :]

    def modulate(v):
        return (v * (1.0 + scale) + shift).astype(BF16)

    xe_ref[0:HALO, :] = modulate(prev_ref[...])
    xe_ref[HALO:HALO + TM, :] = modulate(x_ref[...])
    xe_ref[HALO + TM:2 * HALO + TM, :] = modulate(next_ref[...])
    xe = xe_ref[...]
    rows = lax.broadcasted_iota(jnp.int32, (TM + 2 * HALO, 1), 0)
    valid = jnp.logical_and(jnp.logical_or(rows >= HALO, jnp.logical_not(first)),
                            jnp.logical_or(rows < HALO + TM, jnp.logical_not(last)))
    p_ref[...] = jnp.where(valid, _dot(xe, w_ref[:, CP_QK:CP_V]) + b_ref[:, CP_QK:CP_V], 0.0)
    conv = (cw_ref[0:1, :] * p_ref[HALO - 1:HALO - 1 + TM, :]
            + cw_ref[1:2, :] * p_ref[HALO:HALO + TM, :]
            + cw_ref[2:3, :] * p_ref[HALO + 1:HALO + 1 + TM, :])
    qk = _silu(conv)
    q_ref[...] = qk[:, :MIX_W].astype(BF16)
    k = qk[:, MIX_W:] * MLSTM_DH ** -0.5
    for c in range(CH_T):
        for h in range(MLSTM_HEADS):
            blk = k[c * CHUNK:(c + 1) * CHUNK, h * MLSTM_DH:(h + 1) * MLSTM_DH]
            r0 = (c * MLSTM_HEADS + h) * MLSTM_DH
            kt_ref[r0:r0 + MLSTM_DH, :] = blk.T.astype(BF16)

    xm = xe_ref[HALO:HALO + TM, :]
    pvg = _dot(xm, w_ref[:, CP_V:CP_COLS]) + b_ref[:, CP_V:CP_COLS]
    v_ref[...] = pvg[:, :MIX_W].astype(BF16)
    li = pvg[:, CP_GI - CP_V:CP_GF - CP_V]
    fraw = pvg[:, CP_GF - CP_V:]
    lf = jnp.minimum(fraw, 0.0) - jnp.log1p(jnp.exp(-jnp.abs(fraw)))
    row = lax.broadcasted_iota(jnp.int32, (CHUNK, LANES), 0)
    lane = lax.broadcasted_iota(jnp.int32, (CHUNK, LANES), 1)
    fwd_lane = lane < MLSTM_HEADS
    lower_f = (lane <= row).astype(F32)
    for c in range(CH_T):
        tok = slice(c * CHUNK, (c + 1) * CHUNK)
        prefix = _dot_f32(lower_f, lf[tok])
        suffix = prefix[CHUNK - 1:CHUNK, :] - prefix + lf[tok]
        cum = jnp.where(fwd_lane, prefix, suffix)
        r = li[tok] - cum
        rf, rb = r, r
        k = 1
        while k < CHUNK:
            rf = jnp.maximum(rf, jnp.where(row >= k, pltpu.roll(rf, k, 0), -jnp.inf))
            rb = jnp.maximum(rb, jnp.where(row < CHUNK - k, pltpu.roll(rb, CHUNK - k, 0), -jnp.inf))
            k *= 2
        rmax = jnp.where(fwd_lane, rf, rb)
        g_ref[tok, :] = jnp.where(lane < GS_RMAX, cum, pltpu.roll(rmax, GS_RMAX, 1))
        gt_ref[c * SUBLANES:(c + 1) * SUBLANES, :] = r.T[:SUBLANES, :]


def _cproj_call(s, mods, w_cp, b_cp, qk_conv_w, *, layer):
    hb = TM // HALO
    n_hb = T_ALL // HALO
    return pl.pallas_call(
        _cproj_kernel,
        grid=(NT,),
        in_specs=[
            pl.BlockSpec((TM, D_MODEL), lambda i: (i, 0)),
            pl.BlockSpec((HALO, D_MODEL), lambda i: (jnp.maximum(i * hb - 1, 0), 0)),
            pl.BlockSpec((HALO, D_MODEL), lambda i: (jnp.minimum((i + 1) * hb, n_hb - 1), 0)),
            pl.BlockSpec((None, 9, D_MODEL), lambda i: (layer * MOD_ROWS + _mod_row(i), 0, 0)),
            _const_spec((None, D_MODEL, CP_COLS), (layer, 0, 0)),
            _const_spec((None, 1, CP_COLS), (layer, 0, 0)),
            _const_spec((None, QK_CONV, 2 * MIX_W), (layer, 0, 0)),
        ],
        out_specs=[
            pl.BlockSpec((TM, MIX_W), lambda i: (i, 0)),
            pl.BlockSpec((TM * MLSTM_HEADS, MLSTM_DH), lambda i: (i, 0)),
            pl.BlockSpec((TM, MIX_W), lambda i: (i, 0)),
            pl.BlockSpec((TM, LANES), lambda i: (i, 0)),
            pl.BlockSpec((CH_T * SUBLANES, CHUNK), lambda i: (i, 0)),
        ],
        out_shape=[
            jax.ShapeDtypeStruct((T_ALL, MIX_W), BF16),
            jax.ShapeDtypeStruct((T_ALL * MLSTM_HEADS, MLSTM_DH), BF16),
            jax.ShapeDtypeStruct((T_ALL, MIX_W), BF16),
            jax.ShapeDtypeStruct((T_ALL, LANES), F32),
            jax.ShapeDtypeStruct((T_ALL // CHUNK * SUBLANES, CHUNK), F32),
        ],
        scratch_shapes=[
            pltpu.VMEM((TM + 2 * HALO, D_MODEL), BF16),
            pltpu.VMEM((TM + 2 * HALO, 2 * MIX_W), F32),
        ],
        compiler_params=_params(("parallel",)),
        name="cproj",
    )(s, s, s, mods, w_cp, b_cp, qk_conv_w)


def _mlstm_kernel(qf_ref, ktf_ref, vf_ref, gf_ref, gtf_ref, qb_ref, ktb_ref, vb_ref, gb_ref, gtb_ref,
                  hf_ref, hb_ref, c_ref, m_ref):
    @pl.when(pl.program_id(0) == 0)
    def _():
        c_ref[...] = jnp.zeros(c_ref.shape, F32)
        m_ref[...] = jnp.zeros(m_ref.shape, F32)

    row = lax.broadcasted_iota(jnp.int32, (CHUNK, CHUNK), 0)
    col = lax.broadcasted_iota(jnp.int32, (CHUNK, CHUNK), 1)
    ones_col = jnp.ones((CHUNK, MLSTM_DH), BF16)
    nh = MLSTM_HEADS
    dirs = ((qf_ref, ktf_ref, vf_ref, gf_ref, gtf_ref, hf_ref, col <= row, CHUNK - 1),
            (qb_ref, ktb_ref, vb_ref, gb_ref, gtb_ref, hb_ref, col >= row, 0))
    heads = []
    for d, (q_ref, kt_ref, v_ref, g_ref, gt_ref, h_ref, seen, end_row) in enumerate(dirs):
        for b in range(BATCH):
            g = g_ref[b]
            gt = gt_ref[b]
            for h in range(nh):
                lane = d * nh + h
                hd = slice(h * MLSTM_DH, (h + 1) * MLSTM_DH)
                heads.append(dict(
                    idx=(d * BATCH + b) * nh + h, seen=seen, end_row=end_row, out=(h_ref, b, hd), g=g, lane=lane,
                    cum=g[:, lane:lane + 1], rmax=g[:, GS_RMAX + lane:GS_RMAX + lane + 1], r=gt[lane:lane + 1, :],
                    q=q_ref[b, :, hd], kt=kt_ref[b, hd, :], v=v_ref[b, :, hd]))

    for hd in heads:
        hd["qk"] = _dot(hd["q"], hd["kt"])
        hd["m_prev"] = m_ref[hd["idx"], 0:1, 0:1]
        hd["c_old"] = c_ref[hd["idx"]]
        hd["v_aug"] = jnp.concatenate([hd["v"], ones_col], axis=1)
    for hd in heads:
        e = hd["end_row"]
        top = jnp.maximum(hd["m_prev"], hd["rmax"][e:e + 1, :])
        w_row = jnp.exp(hd["r"] - top)
        ktw = (hd["kt"].astype(F32) * w_row).astype(BF16)
        c_ref[hd["idx"]] = jnp.exp(hd["m_prev"] - top) * hd["c_old"] + _dot(ktw, hd["v_aug"])
        m_ref[hd["idx"]] = jnp.broadcast_to(hd["cum"][e:e + 1, :] + top, m_ref.shape[1:])
    lane_row = lax.broadcasted_iota(jnp.int32, (1, LANES), 1)
    for k0 in range(0, len(heads), nh):
        group = heads[k0:k0 + nh]
        g = group[0]["g"]
        m_row = jnp.zeros((1, LANES), F32)
        for hd in group:
            m_row = jnp.where(lane_row == GS_RMAX + hd["lane"], hd["m_prev"], m_row)
        top_all = jnp.maximum(m_row, g)
        w_all = jnp.exp(m_row - top_all)
        floor_all = jnp.exp(-(pltpu.roll(g, GS_RMAX, 1) + top_all))
        for hd in group:
            ln = GS_RMAX + hd["lane"]
            hd["top"], hd["w"], hd["floor"] = (t[:, ln:ln + 1] for t in (top_all, w_all, floor_all))
    for hd in heads:
        top = jnp.broadcast_to(hd["top"], (CHUNK, CHUNK))
        s = hd["qk"] * jnp.exp(jnp.where(hd["seen"], hd["r"] - top, -jnp.inf))
        qw = hd["q"].astype(F32) * hd["w"]
        lhs = jnp.concatenate([s.astype(BF16), qw.astype(BF16)], axis=1)
        rhs = jnp.concatenate([hd["v_aug"], hd["c_old"].astype(BF16)], axis=0)
        hd["na"] = _dot(lhs, rhs)
    for hd in heads:
        den = jnp.maximum(jnp.abs(hd["na"][:, MLSTM_DH:]), hd["floor"])
        h_ref, b, cols = hd["out"]
        h_ref[b, :, cols] = hd["na"][:, :MLSTM_DH] / den


def _mlstm_call(q, kt, v, g, gt):
    fwd = lambda s: s
    bwd = lambda s: jnp.where(s < CTX_CH, CTX_CH - 1 - s, NCH_B + CTX_CH - 1 - s)
    q3 = q.reshape(BATCH, SEQ_B, MIX_W)
    kt3 = kt.reshape(BATCH, SEQ_B * MLSTM_HEADS, MLSTM_DH)
    v3 = v.reshape(BATCH, SEQ_B, MIX_W)
    g3 = g.reshape(BATCH, SEQ_B, LANES)
    gt3 = gt.reshape(BATCH, NCH_B * SUBLANES, CHUNK)

    def specs(order):
        return [
            pl.BlockSpec((BATCH, CHUNK, MIX_W), lambda s: (0, order(s), 0)),
            pl.BlockSpec((BATCH, CHUNK * MLSTM_HEADS, MLSTM_DH), lambda s: (0, order(s), 0)),
            pl.BlockSpec((BATCH, CHUNK, MIX_W), lambda s: (0, order(s), 0)),
            pl.BlockSpec((BATCH, CHUNK, LANES), lambda s: (0, order(s), 0)),
            pl.BlockSpec((BATCH, SUBLANES, CHUNK), lambda s: (0, order(s), 0)),
        ]

    n_state = 2 * BATCH * MLSTM_HEADS
    hf, hb = pl.pallas_call(
        _mlstm_kernel,
        grid=(NCH_B,),
        in_specs=specs(fwd) + specs(bwd),
        out_specs=[
            pl.BlockSpec((BATCH, CHUNK, MIX_W), lambda s: (0, fwd(s), 0)),
            pl.BlockSpec((BATCH, CHUNK, MIX_W), lambda s: (0, bwd(s), 0)),
        ],
        out_shape=[jax.ShapeDtypeStruct((BATCH, SEQ_B, MIX_W), F32)] * 2,
        scratch_shapes=[
            pltpu.VMEM((n_state, MLSTM_DH, 2 * MLSTM_DH), F32),
            pltpu.VMEM((n_state, SUBLANES, LANES), F32),
        ],
        compiler_params=_params(("arbitrary",)),
        name="mlstm",
    )(q3, kt3, v3, g3, gt3, q3, kt3, v3, g3, gt3)
    return hf.reshape(T_ALL, MIX_W), hb.reshape(T_ALL, MIX_W)


VEC_GMLP_G, VEC_GMLP_B, VEC_CONV_B, VEC_CONV_G, VEC_CONV_LB, VEC_MLSTM_G, VEC_POOL_S = range(7)
VEC_ROWS = 8


def _mix_kernel(x_ref, prev_ref, next_ref, mod_ref, hf_ref, hb_ref, w_ref, b_ref, vec_ref, ws_ref, bst_ref,
                cw_ref, pw_ref, wbr_ref, wout_ref, g_ref, beta_ref, o_ref, xe_ref, a_ref, d_ref, ash_ref, conv_ref,
                gate_ref, pa_ref, po_ref, u_ref, vn_ref, yc_ref, yd_ref):
    j = pl.program_id(0) % NT_B
    is_ctx = j == 0
    first = jnp.logical_or(is_ctx, j == 1)
    last = jnp.logical_or(is_ctx, j == NT_B - 1)
    shift = mod_ref[3:4, :]
    scale = mod_ref[4:5, :]
    gate = mod_ref[5:6, :]

    def modulate(v):
        return (v * (1.0 + scale) + shift).astype(BF16)

    def vec(r):
        return vec_ref[r:r + 1, :]

    def proj(ext, lo, width, zero=None, bias_scale=None):
        lhs = xe_ref[...] if ext else xe_ref[HALO:HALO + TM, :]

        def bias(c):
            row = b_ref[:, c:c + MXU_N]
            row = row if bias_scale is None else bias_scale * row
            return row if zero is None else row + zero

        return jnp.concatenate([_dot(lhs, w_ref[:, c:c + MXU_N]) + bias(c)
                                for c in range(lo, lo + width, MXU_N)], axis=1)

    xe_ref[0:HALO, :] = modulate(prev_ref[...])
    xe_ref[HALO:HALO + TM, :] = modulate(x_ref[...])
    xe_ref[HALO + TM:2 * HALO + TM, :] = modulate(next_ref[...])
    rows = lax.broadcasted_iota(jnp.int32, (TM + 2 * HALO, 1), 0)
    valid = jnp.logical_and(jnp.logical_or(rows >= HALO, jnp.logical_not(first)),
                            jnp.logical_or(rows < HALO + TM, jnp.logical_not(last)))

    pb = proj(True, MX_B, 2 * MIX_W)
    a_ref[...] = jnp.where(valid, pb[:, :MIX_W] * _sigmoid(pb[:, MIX_W:]), 0.0)
    pa_ref[...] = proj(False, MX_A, 2 * MIX_W)

    gate_chunks = [(i, n) for i in range(N_BRANCH) for n in range(D_MODEL // MXU_N)]
    sh_rows = TM + 2 * HALO - SUBLANES

    def zero_of(v):
        bits = lax.bitcast_convert_type(v[0:1, 0:LANES], jnp.uint32)
        return lax.shift_right_logical(lax.shift_right_logical(bits, jnp.uint32(16)), jnp.uint32(16)).astype(F32)

    def conv_block(cb, rb, z):
        cols = slice(cb * LANES, (cb + 1) * LANES)
        part = None
        for k in range(CONV_W):
            q8, s = divmod(HALO - CONV_W // 2 + k, SUBLANES)
            r0 = q8 * SUBLANES + rb * CONV_RB
            src = a_ref[r0:r0 + CONV_RB, cols] if s == 0 else ash_ref[s - 1, r0:r0 + CONV_RB, cols]
            term = (cw_ref[k:k + 1, cols] + z) * src
            part = term if part is None else part + term
        conv_ref[rb * CONV_RB:(rb + 1) * CONV_RB, cols] = part
        return part

    conv_blocks = [(cb, rb) for cb in range(MIX_W // LANES) for rb in range(TM // CONV_RB)]
    per_step = len(conv_blocks) // len(gate_chunks)
    zero_row = jnp.zeros((1, LANES), F32)
    v_zero = [zero_row] * len(gate_chunks)
    for step, (i, n) in enumerate(gate_chunks):
        lo = i * D_MODEL + n * MXU_N
        z_in = jnp.concatenate([v_zero[step - CHAIN_LAG]] * (MXU_N // LANES), axis=1) if step >= CHAIN_LAG else None
        gv = jnp.tanh(proj(False, MX_G + lo, MXU_N, z_in, bias_scale=0.5))
        gate_ref[:, lo:lo + MXU_N] = gv
        z = zero_of(gv) + v_zero[step - 1]
        for cb, rb in conv_blocks[step * per_step:(step + 1) * per_step]:
            if rb == 0:
                cols = slice(cb * LANES, (cb + 1) * LANES)
                for s in range(1, SUBLANES):
                    ash_ref[s - 1, 0:sh_rows, cols] = a_ref[s:s + sh_rows, cols]
            z = zero_of(conv_block(cb, rb, z))
        v_zero[step] = z

    pa = _gelu_tanh(pa_ref[...])
    u_ref[...] = pa[:, :MIX_W]
    vn_ref[...] = _ln(pa[:, MIX_W:], vec(VEC_GMLP_G), vec(VEC_GMLP_B)).astype(BF16)

    d_ref[0:TM + 2 * HALO, :] = jnp.where(valid, proj(True, MX_D, MIX_W), 0.0)
    d_ref[TM + 2 * HALO:TM + 2 * HALO + SUBLANES, :] = jnp.zeros((SUBLANES, MIX_W), F32)
    pos = jnp.where(is_ctx, 0, (j - 1) * TM) + lax.broadcasted_iota(jnp.int32, (TM, 1), 0)
    n_seq = jnp.where(is_ctx, CTX_LEN, SEQ)

    def window_sum(cols, win):
        start = HALO - win // 2
        if win < SUBLANES:
            ws = d_ref[start:start + TM, cols]
            for k in range(1, win):
                ws = ws + d_ref[start + k:start + k + TM, cols]
            return ws
        base = start // SUBLANES * SUBLANES
        off = start - base
        up8 = lambda r: -(-r // SUBLANES) * SUBLANES
        rows = [up8(TM + off)]
        n = win
        while n > 1:
            n //= 2
            rows.append(up8(rows[-1] + n))
        rows.reverse()
        level = d_ref[base:base + rows[1], cols] + d_ref[base + 1:base + 1 + rows[1], cols]
        n, i = 2, 1
        while n < win:
            i += 1
            level = level[0:rows[i], :] + level[n:n + rows[i], :]
            n *= 2
        return level[off:off + TM, :]

    grp = lax.broadcasted_iota(jnp.int32, (1, LANES), 1)
    lo_row = jnp.zeros((1, LANES), jnp.int32)
    hi_row = jnp.zeros((1, LANES), jnp.int32)
    for gi, win in enumerate(POOL_WINDOWS):
        lo_row = jnp.where(grp == gi, win // 2, lo_row)
        hi_row = jnp.where(grp == gi, win - 1 - win // 2, hi_row)
    cnt_all = (jnp.minimum(pos + hi_row + 1, n_seq) - jnp.maximum(pos - lo_row, 0)).astype(F32)
    inv_cnt = 1.0 / cnt_all
    for gi, win in enumerate(POOL_WINDOWS):
        cols = slice(gi * POOL_GD, (gi + 1) * POOL_GD)
        wsum = window_sum(cols, win)
        diff = wsum * inv_cnt[:, gi:gi + 1] - d_ref[HALO:HALO + TM, cols]
        yd_ref[:, cols] = (_dot(diff.astype(BF16), pw_ref[gi]) * vec_ref[VEC_POOL_S:VEC_POOL_S + 1, cols]
                           ).astype(BF16)

    po_ref[...] = proj(False, MX_O, MIX_W)
    for h in range(MLSTM_HEADS):
        cols = slice(h * MLSTM_DH, (h + 1) * MLSTM_DH)
        hh = hf_ref[:, cols] + hb_ref[:, cols]
        mu = jnp.mean(hh, axis=-1, keepdims=True)
        hc = hh - mu
        hn = hc * lax.rsqrt(jnp.mean(hc * hc, axis=-1, keepdims=True) + LN_EPS)
        yc_ref[:, cols] = (_sigmoid(po_ref[:, cols]) * (hn * vec_ref[VEC_MLSTM_G:VEC_MLSTM_G + 1, cols])
                           ).astype(BF16)

    gd = MIX_W // GMLP_GROUPS
    z_rows = []
    for c in range(CH_T):
        z_cols = []
        for gi in range(GMLP_GROUPS):
            blk = vn_ref[c * CHUNK:(c + 1) * CHUNK, gi * gd:(gi + 1) * gd]
            z_cols.append(_dot(ws_ref[gi], blk) + bst_ref[:, gi:gi + 1])
        z_rows.append(jnp.concatenate(z_cols, axis=1))
    ya = (u_ref[...] * jnp.concatenate(z_rows, axis=0)).astype(BF16)
    yb = _silu(_ln(conv_ref[...] + vec(VEC_CONV_B), vec(VEC_CONV_G), vec(VEC_CONV_LB))).astype(BF16)
    yc = yc_ref[...]
    yd = yd_ref[...]

    merged = []
    for n in range(D_MODEL // MXU_N):
        cols = slice(n * MXU_N, (n + 1) * MXU_N)
        acc = None
        for i, yi in enumerate((ya, yb, yc, yd)):
            lo = i * D_MODEL + n * MXU_N
            p = _dot(yi, wbr_ref[i, :, cols])
            term = gate_ref[:, lo:lo + MXU_N] * p + p
            acc = term if acc is None else acc + term
        merged.append((0.5 * acc).astype(BF16))
    merged = jnp.concatenate(merged, axis=1)
    y = jnp.concatenate([_dot(merged, wout_ref[:, n * MXU_N:(n + 1) * MXU_N]) for n in range(D_MODEL // MXU_N)],
                        axis=1)
    o_ref[...] = _ln(ALPHA * x_ref[...] + gate * y, g_ref[...], beta_ref[...])


def _mix_call(s, mods, hf, hb, w_mx, b_mx, vec512, gmlp_ws, gmlp_bst, conv_w, pool_w, w_branch, w_out,
              ln_g, ln_b, *, layer):
    hb_per_tile = TM // HALO
    n_hb = T_ALL // HALO
    return pl.pallas_call(
        _mix_kernel,
        grid=(NT,),
        in_specs=[
            pl.BlockSpec((TM, D_MODEL), lambda i: (i, 0)),
            pl.BlockSpec((HALO, D_MODEL), lambda i: (jnp.maximum(i * hb_per_tile - 1, 0), 0)),
            pl.BlockSpec((HALO, D_MODEL), lambda i: (jnp.minimum((i + 1) * hb_per_tile, n_hb - 1), 0)),
            pl.BlockSpec((None, 9, D_MODEL), lambda i: (layer * MOD_ROWS + _mod_row(i), 0, 0)),
            pl.BlockSpec((TM, MIX_W), lambda i: (i, 0)),
            pl.BlockSpec((TM, MIX_W), lambda i: (i, 0)),
            _const_spec((None, D_MODEL, MX_W), (layer, 0, 0)),
            _const_spec((None, 1, MX_COLS), (layer, 0, 0)),
            _const_spec((None, VEC_ROWS, MIX_W), (layer, 0, 0)),
            _const_spec((None, GMLP_GROUPS, CHUNK, CHUNK), (layer, 0, 0, 0)),
            _const_spec((None, CHUNK, GMLP_GROUPS), (layer, 0, 0)),
            _const_spec((None, CONV_W, MIX_W), (layer, 0, 0)),
            _const_spec((None, len(POOL_WINDOWS), POOL_GD, POOL_GD), (layer, 0, 0, 0)),
            _const_spec((None, N_BRANCH, MIX_W, D_MODEL + LANE_PAD), (layer, 0, 0, 0)),
            _const_spec((None, D_MODEL, D_MODEL + LANE_PAD), (layer, 0, 0)),
            _const_spec((None, 1, D_MODEL), (layer * 3 + 1, 0, 0)),
            _const_spec((None, 1, D_MODEL), (layer * 3 + 1, 0, 0)),
        ],
        out_specs=pl.BlockSpec((TM, D_MODEL), lambda i: (i, 0)),
        out_shape=jax.ShapeDtypeStruct((T_ALL, D_MODEL), F32),
        scratch_shapes=[
            pltpu.VMEM((TM + 2 * HALO, D_MODEL), BF16),
            pltpu.VMEM((TM + 2 * HALO, MIX_W), F32),
            pltpu.VMEM((TM + 2 * HALO + SUBLANES, MIX_W), F32),
            pltpu.VMEM((SUBLANES - 1, TM + 2 * HALO, MIX_W), F32),
            pltpu.VMEM((TM, MIX_W), F32),
            pltpu.VMEM((TM, N_BRANCH * D_MODEL), F32),
            pltpu.VMEM((TM, 2 * MIX_W), F32),
            pltpu.VMEM((TM, MIX_W), F32),
            pltpu.VMEM((TM, MIX_W), F32),
            pltpu.VMEM((TM, MIX_W), BF16),
            pltpu.VMEM((TM, MIX_W), BF16),
            pltpu.VMEM((TM, MIX_W), BF16),
        ],
        compiler_params=_params(("parallel",)),
        name="mix",
    )(s, s, s, mods, hf, hb, w_mx, b_mx, vec512, gmlp_ws, gmlp_bst, conv_w, pool_w, w_branch, w_out, ln_g, ln_b)


MX_TILE = 512
MX_PAD = MX_TILE
MX_W = MX_COLS + MX_PAD
CP_TILE = 256


def _mx_src(j):
    a, o, d = (MX_O - MX_A) // MX_TILE, (MX_D - MX_A) // MX_TILE, (MX_G - MX_A) // MX_TILE
    g_end = MX_COLS // MX_TILE
    return jnp.where(j < a, OFF_A + MX_TILE * j,
                     jnp.where(j < o, OFF_C_O + MX_TILE * (j - a),
                               jnp.where(j < d, OFF_D + MX_TILE * (j - o),
                                         jnp.where(j < g_end, OFF_G + MX_TILE * (j - d), 0))))


def _prep_mx_kernel(wt_ref, o_ref):
    j = pl.program_id(1)

    @pl.when(j < MX_G // MX_TILE)
    def _():
        o_ref[...] = wt_ref[0].T.astype(BF16)

    @pl.when(jnp.logical_and(j >= MX_G // MX_TILE, j < MX_COLS // MX_TILE))
    def _():
        o_ref[...] = (0.5 * wt_ref[0].T).astype(BF16)

    @pl.when(j >= MX_COLS // MX_TILE)
    def _():
        o_ref[...] = jnp.zeros(o_ref.shape, BF16)


def _prep_cp_kernel(wt_ref, o_ref):
    j = pl.program_id(1)
    n_plain = (CP_GI - CP_QK) // CP_TILE

    @pl.when(j < n_plain)
    def _():
        o_ref[...] = wt_ref[0].T.astype(BF16)

    @pl.when(j == n_plain)
    def _():
        nh = MLSTM_HEADS
        t = wt_ref[0, 0:LANES, :].T
        lane = lax.broadcasted_iota(jnp.int32, t.shape, 1)
        up1 = pltpu.roll(t, LANES - nh, 1)
        up2 = pltpu.roll(t, LANES - 2 * nh, 1)
        gi = jnp.where(lane < nh, t, jnp.where(lane < 2 * nh, up1, 0.0))
        gf = jnp.where(lane < nh, up1, jnp.where(lane < 2 * nh, up2, 0.0))
        o_ref[:, 0:LANES] = gi.astype(BF16)
        o_ref[:, LANES:2 * LANES] = gf.astype(BF16)


def _prep_win_call(w_in):
    wt = jnp.swapaxes(w_in, 1, 2)
    n_plain = (CP_GI - CP_QK) // CP_TILE
    w_mx = pl.pallas_call(
        _prep_mx_kernel,
        grid=(DEPTH, MX_W // MX_TILE),
        in_specs=[pl.BlockSpec((pl.Element(1), pl.Element(MX_TILE), pl.Element(D_MODEL)),
                               lambda l, j: (l, pl.multiple_of(_mx_src(j), SUBLANES), 0))],
        out_specs=pl.BlockSpec((None, D_MODEL, MX_TILE), lambda l, j: (l, 0, j)),
        out_shape=jax.ShapeDtypeStruct((DEPTH, D_MODEL, MX_W), BF16),
        compiler_params=_params(("parallel", "parallel")),
        name="prep_mx",
    )(wt)
    w_cp = pl.pallas_call(
        _prep_cp_kernel,
        grid=(DEPTH, CP_COLS // CP_TILE),
        in_specs=[pl.BlockSpec((pl.Element(1), pl.Element(CP_TILE), pl.Element(D_MODEL)),
                               lambda l, j: (l, pl.multiple_of(
                                   jnp.where(j < n_plain, OFF_C + CP_TILE * j, OFF_C_GATES), SUBLANES), 0))],
        out_specs=pl.BlockSpec((None, D_MODEL, CP_TILE), lambda l, j: (l, 0, j)),
        out_shape=jax.ShapeDtypeStruct((DEPTH, D_MODEL, CP_COLS), BF16),
        compiler_params=_params(("parallel", "parallel")),
        name="prep_cp",
    )(wt)
    return w_mx, w_cp


def _pad_cast_kernel(w_ref, o_ref):
    cols = w_ref.shape[1]
    o_ref[:, 0:cols] = w_ref[...].astype(BF16)
    o_ref[:, cols:] = jnp.zeros((o_ref.shape[0], o_ref.shape[1] - cols), BF16)


def _pad_cast_call(w):
    lead, cols = w.shape[:-1], w.shape[-1]
    rows = math.prod(lead)
    rb = 512
    out = pl.pallas_call(
        _pad_cast_kernel,
        grid=(rows // rb,),
        in_specs=[pl.BlockSpec((rb, cols), lambda i: (i, 0))],
        out_specs=pl.BlockSpec((rb, cols + LANE_PAD), lambda i: (i, 0)),
        out_shape=jax.ShapeDtypeStruct((rows, cols + LANE_PAD), BF16),
        compiler_params=_params(("parallel",)),
        name="pad_cast",
    )(w.reshape(rows, cols))
    return out.reshape(lead + (cols + LANE_PAD,))


def _repack_cols(w, pieces):
    parts = [jnp.zeros(w.shape[:-1] + (p,), w.dtype) if isinstance(p, int) else w[..., p[0]:p[1]] for p in pieces]
    return jnp.concatenate(parts, axis=-1)


def kernel(x, c, ctx, c_ctx, w_ada, b_ada, ln_g, ln_b, ffn_w_in, ffn_w_out, w_in, b_in, gmlp_ln_g, gmlp_ln_b,
           gmlp_ws, gmlp_bs, conv_w, conv_b, conv_ln_g, conv_ln_b, qk_conv_w, mlstm_ln_g, pool_w, pool_scale,
           w_branch, w_out):
    mix_pieces = [(OFF_A, OFF_C), (OFF_C_O, OFF_D), (OFF_D, OFF_G), (OFF_G, IN_COLS)]
    nh = MLSTM_HEADS
    gate_pad = LANES - 2 * nh

    def cproj_cols(w):
        g = w[..., OFF_C_GATES:OFF_C_O].reshape(w.shape[:-1] + (4, nh))
        pad = jnp.zeros(w.shape[:-1] + (gate_pad,), w.dtype)
        return jnp.concatenate([w[..., OFF_C:OFF_C_GATES], g[..., 0, :], g[..., 2, :], pad,
                                g[..., 1, :], g[..., 3, :], pad], axis=-1)
    w_mx, w_cp = _prep_win_call(w_in)
    b_mx = _repack_cols(b_in, mix_pieces).reshape(DEPTH, 1, MX_COLS)
    b_cp = cproj_cols(b_in).reshape(DEPTH, 1, CP_COLS)
    ffn_w_in_b = ffn_w_in.astype(BF16)
    ffn_w_out_b = _pad_cast_call(ffn_w_out)
    w_branch_b = _pad_cast_call(w_branch)
    w_out_b = _pad_cast_call(w_out)
    gmlp_ws_b = gmlp_ws.astype(BF16)
    pool_w_b = pool_w.astype(BF16)
    gmlp_bst = jnp.swapaxes(gmlp_bs, 1, 2)
    vec512 = jnp.stack([gmlp_ln_g, gmlp_ln_b, conv_b, conv_ln_g, conv_ln_b, mlstm_ln_g, pool_scale,
                        jnp.zeros_like(pool_scale)], axis=1)
    ln_g3 = ln_g.reshape(DEPTH * 3, 1, D_MODEL)
    ln_b3 = ln_b.reshape(DEPTH * 3, 1, D_MODEL)
    c_t = jnp.concatenate([c, c_ctx[None], jnp.zeros((MOD_ROWS - BATCH - 1, D_MODEL), F32)], axis=0).T

    mods = _mods_call(c_t, w_ada, b_ada).reshape(DEPTH * MOD_ROWS, 9, D_MODEL)
    s = _embed_call(x, ctx)
    for l in range(DEPTH):
        last = l == DEPTH - 1
        s = _ffn_call(s, mods, ffn_w_in_b, ffn_w_out_b, ln_g3, ln_b3, layer=l, which=0, final=False)
        q, kt, v, g, gt = _cproj_call(s, mods, w_cp, b_cp, qk_conv_w, layer=l)
        hf, hb = _mlstm_call(q, kt, v, g, gt)
        s = _mix_call(s, mods, hf, hb, w_mx, b_mx, vec512, gmlp_ws_b, gmlp_bst, conv_w, pool_w_b, w_branch_b,
                      w_out_b, ln_g3, ln_b3, layer=l)
        s = _ffn_call(s, mods, ffn_w_in_b, ffn_w_out_b, ln_g3, ln_b3, layer=l, which=1, final=last)
    return s.reshape(BATCH, SEQ, D_MODEL)
```

```python
import functools
import math

import jax
import jax.numpy as jnp
from jax import lax
from jax.experimental import pallas as pl
from jax.experimental.pallas import tpu as pltpu

D_MODEL = 1024
BATCH = 2
SEQ = 8192
DEPTH = 4
GRID_W = 64
CTX_LEN = 256
MIX_W = D_MODEL // 2
N_BRANCH = 4
CHUNK = 128
GMLP_GROUPS = 4
CONV_W = 31
MLSTM_HEADS = 4
MLSTM_DH = MIX_W // MLSTM_HEADS
QK_CONV = 3
POOL_WINDOWS = (2, 4, 8, 16)
POOL_GD = MIX_W // len(POOL_WINDOWS)
D_FF = 128 * ((8 * D_MODEL // 3 + 127) // 128)
ALPHA = (2 * DEPTH) ** 0.25
LN_EPS = 1e-6
FFN_RES = 0.5

OFF_A = 0
OFF_B = OFF_A + 2 * MIX_W
OFF_C = OFF_B + 2 * MIX_W
OFF_C_GATES = OFF_C + 3 * MIX_W
OFF_C_O = OFF_C_GATES + 4 * MLSTM_HEADS
OFF_D = OFF_C_O + MIX_W
OFF_G = OFF_D + MIX_W
IN_COLS = OFF_G + N_BRANCH * D_MODEL

LANES = 128
SUBLANES = 8
TM = 256
SEQ_B = CTX_LEN + SEQ
T_ALL = BATCH * SEQ_B
NT_B = SEQ_B // TM
NT = BATCH * NT_B
NCH_B = SEQ_B // CHUNK
CH_T = TM // CHUNK
CTX_CH = CTX_LEN // CHUNK
HALO = 16
MXU_N = 256
LANE_PAD = LANES
FF_CH = MXU_N
FFN_TILES = 3
FFN_TILES_FINAL = 2
CONV_RB = 64
CHAIN_LAG = 3
GS_RMAX = 2 * MLSTM_HEADS
MOD_ROWS = 8
CTX_MOD_ROW = BATCH
VMEM_LIMIT = 56 * 1024 * 1024

MX_A = 0
MX_B = MX_A + 2 * MIX_W
MX_O = MX_B + 2 * MIX_W
MX_D = MX_O + MIX_W
MX_G = MX_D + MIX_W
MX_COLS = MX_G + N_BRANCH * D_MODEL
CP_QK = 0
CP_V = 2 * MIX_W
CP_GI = 3 * MIX_W
CP_GF = CP_GI + LANES
CP_COLS = CP_GF + LANES

F32 = jnp.float32
BF16 = jnp.bfloat16


def _dot(a, b):
    return jnp.dot(a, b, preferred_element_type=F32)


def _dot_f32(a, b):
    return jnp.dot(a, b, preferred_element_type=F32, precision=lax.Precision.HIGHEST)


def _ln(r, g, b):
    mu = jnp.mean(r, axis=-1, keepdims=True)
    xc = r - mu
    var = jnp.mean(xc * xc, axis=-1, keepdims=True)
    return xc * lax.rsqrt(var + LN_EPS) * g + b


def _sigmoid(x):
    return 0.5 * jnp.tanh(0.5 * x) + 0.5


def _silu(x):
    hx = 0.5 * x
    return hx * jnp.tanh(hx) + hx


def _gelu_tanh(x):
    c = math.sqrt(2.0 / math.pi)
    hx = 0.5 * x
    return hx * jnp.tanh(x * (c + (c * 0.044715) * (x * x))) + hx


def _mod_row(i):
    return jnp.where(i % NT_B == 0, CTX_MOD_ROW, i // NT_B)


def _const_spec(block, index):
    return pl.BlockSpec(block, lambda *_: index, pipeline_mode=pl.Buffered(1))


def _params(sem):
    return pltpu.CompilerParams(dimension_semantics=sem, vmem_limit_bytes=VMEM_LIMIT)


def _mods_kernel(ct_ref, w_ref, b_ref, o_ref):
    s = _silu(ct_ref[...])
    w = w_ref[...]
    o_ref[...] = jnp.zeros(o_ref.shape, F32)
    for r in range(BATCH + 1):
        o_ref[r:r + 1, :] = jnp.sum(s[:, r:r + 1] * w, axis=0, keepdims=True) + b_ref[...]


def _mods_call(c_t, w_ada, b_ada):
    tn = D_MODEL
    n_col = w_ada.shape[-1] // tn
    return pl.pallas_call(
        _mods_kernel,
        grid=(DEPTH, n_col),
        in_specs=[
            pl.BlockSpec((D_MODEL, MOD_ROWS), lambda l, n: (0, 0)),
            pl.BlockSpec((None, D_MODEL, tn), lambda l, n: (l, 0, n)),
            pl.BlockSpec((None, 1, tn), lambda l, n: (l, 0, n)),
        ],
        out_specs=pl.BlockSpec((None, MOD_ROWS, tn), lambda l, n: (l, 0, n)),
        out_shape=jax.ShapeDtypeStruct((DEPTH, MOD_ROWS, w_ada.shape[-1]), F32),
        compiler_params=_params(("parallel", "parallel")),
        name="mods",
    )(c_t, w_ada, b_ada.reshape(DEPTH, 1, -1))


def _embed_kernel(x_ref, ctx_ref, o_ref):
    j = pl.program_id(0) % NT_B

    @pl.when(j == 0)
    def _():
        o_ref[...] = ctx_ref[...]

    @pl.when(j > 0)
    def _():
        quarter = D_MODEL // 4
        grid_rows = TM // GRID_W
        k = lax.broadcasted_iota(jnp.int32, (1, quarter), 1).astype(F32)
        freqs = jnp.exp(-math.log(10000.0) * k / quarter)
        r = ((j - 1) * grid_rows + lax.broadcasted_iota(jnp.int32, (SUBLANES, 1), 0)).astype(F32)
        col = lax.broadcasted_iota(jnp.int32, (GRID_W, 1), 0).astype(F32)
        er = r * freqs
        ec = col * freqs
        sin_r, cos_r, sin_c, cos_c = jnp.sin(er), jnp.cos(er), jnp.sin(ec), jnp.cos(ec)
        for q in range(grid_rows):
            rows = slice(q * GRID_W, (q + 1) * GRID_W)
            o_ref[rows, 0 * quarter:1 * quarter] = x_ref[rows, 0 * quarter:1 * quarter] + sin_r[q:q + 1, :]
            o_ref[rows, 1 * quarter:2 * quarter] = x_ref[rows, 1 * quarter:2 * quarter] + cos_r[q:q + 1, :]
            o_ref[rows, 2 * quarter:3 * quarter] = x_ref[rows, 2 * quarter:3 * quarter] + sin_c
            o_ref[rows, 3 * quarter:4 * quarter] = x_ref[rows, 3 * quarter:4 * quarter] + cos_c


def _embed_call(x, ctx):
    lat_tiles_b = SEQ // TM
    return pl.pallas_call(
        _embed_kernel,
        grid=(NT,),
        in_specs=[
            pl.BlockSpec((TM, D_MODEL), lambda i: ((i // NT_B) * lat_tiles_b + jnp.maximum(i % NT_B - 1, 0), 0)),
            pl.BlockSpec((TM, D_MODEL), lambda i: (i // NT_B, 0)),
        ],
        out_specs=pl.BlockSpec((TM, D_MODEL), lambda i: (i, 0)),
        out_shape=jax.ShapeDtypeStruct((T_ALL, D_MODEL), F32),
        compiler_params=_params(("parallel",)),
        name="embed",
    )(x.reshape(BATCH * SEQ, D_MODEL), ctx.reshape(BATCH * CTX_LEN, D_MODEL))


def _ffn_kernel(*refs, sub, tiles):
    x_refs = refs[:tiles]
    mod_refs = refs[tiles:2 * tiles]
    win_ref, wout_ref, g_ref, b_ref, o_ref, xm_ref, h_ref = refs[2 * tiles:]
    for t, (x_ref, mod_ref) in enumerate(zip(x_refs, mod_refs)):
        shift = mod_ref[3 * sub + 0:3 * sub + 1, :]
        scale = mod_ref[3 * sub + 1:3 * sub + 2, :]
        xm_ref[t * TM:(t + 1) * TM, :] = (x_ref[...] * (1.0 + scale) + shift).astype(BF16)
    xm = xm_ref[...]
    for c in range(D_FF // FF_CH):
        a1 = _dot(xm, win_ref[:, c * FF_CH:(c + 1) * FF_CH])
        a2 = _dot(xm, win_ref[:, D_FF + c * FF_CH:D_FF + (c + 1) * FF_CH])
        h_ref[:, c * FF_CH:(c + 1) * FF_CH] = (_silu(a1) * a2).astype(BF16)
    for t, (x_ref, mod_ref) in enumerate(zip(x_refs, mod_refs)):
        hid = h_ref[t * TM:(t + 1) * TM, :]
        y = jnp.concatenate([_dot(hid, wout_ref[:, n * MXU_N:(n + 1) * MXU_N]) for n in range(D_MODEL // MXU_N)],
                            axis=1)
        gate = mod_ref[3 * sub + 2:3 * sub + 3, :]
        r = ALPHA * x_ref[...] + (FFN_RES * gate) * y
        o_ref[t * TM:(t + 1) * TM, :] = _ln(r, g_ref[...], b_ref[...])


def _ffn_call(s, mods, ffn_w_in, ffn_w_out, ln_g, ln_b, *, layer, which, final):
    sub = 2 * which
    if final:
        lat_tiles_b = SEQ // TM
        out_tiles, tiles = BATCH * lat_tiles_b, FFN_TILES_FINAL
        in_tile = lambda t: (t // lat_tiles_b) * NT_B + 1 + t % lat_tiles_b
    else:
        out_tiles, tiles = NT, FFN_TILES
        in_tile = lambda t: t
    tile_of = lambda i, t: in_tile(i * tiles + t)
    x_specs = [pl.BlockSpec((TM, D_MODEL), functools.partial(lambda i, t: (tile_of(i, t), 0), t=t))
               for t in range(tiles)]
    mod_specs = [pl.BlockSpec((None, 9, D_MODEL),
                              functools.partial(lambda i, t: (layer * MOD_ROWS + _mod_row(tile_of(i, t)), 0, 0), t=t))
                 for t in range(tiles)]
    return pl.pallas_call(
        functools.partial(_ffn_kernel, sub=sub, tiles=tiles),
        grid=(out_tiles // tiles,),
        in_specs=x_specs + mod_specs + [
            _const_spec((None, None, D_MODEL, 2 * D_FF), (layer, which, 0, 0)),
            _const_spec((None, None, D_FF, D_MODEL + LANE_PAD), (layer, which, 0, 0)),
            _const_spec((None, 1, D_MODEL), (layer * 3 + sub, 0, 0)),
            _const_spec((None, 1, D_MODEL), (layer * 3 + sub, 0, 0)),
        ],
        out_specs=pl.BlockSpec((tiles * TM, D_MODEL), lambda i: (i, 0)),
        out_shape=jax.ShapeDtypeStruct((out_tiles * TM, D_MODEL), F32),
        scratch_shapes=[
            pltpu.VMEM((tiles * TM, D_MODEL), BF16),
            pltpu.VMEM((tiles * TM, D_FF), BF16),
        ],
        compiler_params=_params(("parallel",)),
        name=f"ffn{which}",
    )(*([s] * tiles), *([mods] * tiles), ffn_w_in, ffn_w_out, ln_g, ln_b)


def _cproj_kernel(x_ref, prev_ref, next_ref, mod_ref, w_ref, b_ref, cw_ref,
                  q_ref, kt_ref, v_ref, g_ref, gt_ref, xe_ref, p_ref):
    j = pl.program_id(0) % NT_B
    first = jnp.logical_or(j == 0, j == 1)
    last = jnp.logical_or(j == 0, j == NT_B - 1)
    shift = mod_ref[3:4, :]
    scale = mod_ref[4:5, :]

    def modulate(v):
        return (v * (1.0 + scale) + shift).astype(BF16)

    xe_ref[0:HALO, :] = modulate(prev_ref[...])
    xe_ref[HALO:HALO + TM, :] = modulate(x_ref[...])
    xe_ref[HALO + TM:2 * HALO + TM, :] = modulate(next_ref[...])
    xe = xe_ref[...]
    rows = lax.broadcasted_iota(jnp.int32, (TM + 2 * HALO, 1), 0)
    valid = jnp.logical_and(jnp.logical_or(rows >= HALO, jnp.logical_not(first)),
                            jnp.logical_or(rows < HALO + TM, jnp.logical_not(last)))
    p_ref[...] = jnp.where(valid, _dot(xe, w_ref[:, CP_QK:CP_V]) + b_ref[:, CP_QK:CP_V], 0.0)
    conv = (cw_ref[0:1, :] * p_ref[HALO - 1:HALO - 1 + TM, :]
            + cw_ref[1:2, :] * p_ref[HALO:HALO + TM, :]
            + cw_ref[2:3, :] * p_ref[HALO + 1:HALO + 1 + TM, :])
    qk = _silu(conv)
    q_ref[...] = qk[:, :MIX_W].astype(BF16)
    k = qk[:, MIX_W:] * MLSTM_DH ** -0.5
    for c in range(CH_T):
        for h in range(MLSTM_HEADS):
            blk = k[c * CHUNK:(c + 1) * CHUNK, h * MLSTM_DH:(h + 1) * MLSTM_DH]
            r0 = (c * MLSTM_HEADS + h) * MLSTM_DH
            kt_ref[r0:r0 + MLSTM_DH, :] = blk.T.astype(BF16)

    xm = xe_ref[HALO:HALO + TM, :]
    pvg = _dot(xm, w_ref[:, CP_V:CP_COLS]) + b_ref[:, CP_V:CP_COLS]
    v_ref[...] = pvg[:, :MIX_W].astype(BF16)
    li = pvg[:, CP_GI - CP_V:CP_GF - CP_V]
    fraw = pvg[:, CP_GF - CP_V:]
    lf = jnp.minimum(fraw, 0.0) - jnp.log1p(jnp.exp(-jnp.abs(fraw)))
    row = lax.broadcasted_iota(jnp.int32, (CHUNK, LANES), 0)
    lane = lax.broadcasted_iota(jnp.int32, (CHUNK, LANES), 1)
    fwd_lane = lane < MLSTM_HEADS
    lower_f = (lane <= row).astype(F32)
    for c in range(CH_T):
        tok = slice(c * CHUNK, (c + 1) * CHUNK)
        prefix = _dot_f32(lower_f, lf[tok])
        suffix = prefix[CHUNK - 1:CHUNK, :] - prefix + lf[tok]
        cum = jnp.where(fwd_lane, prefix, suffix)
        r = li[tok] - cum
        rf, rb = r, r
        k = 1
        while k < CHUNK:
            rf = jnp.maximum(rf, jnp.where(row >= k, pltpu.roll(rf, k, 0), -jnp.inf))
            rb = jnp.maximum(rb, jnp.where(row < CHUNK - k, pltpu.roll(rb, CHUNK - k, 0), -jnp.inf))
            k *= 2
        rmax = jnp.where(fwd_lane, rf, rb)
        g_ref[tok, :] = jnp.where(lane < GS_RMAX, cum, pltpu.roll(rmax, GS_RMAX, 1))
        gt_ref[c * SUBLANES:(c + 1) * SUBLANES, :] = r.T[:SUBLANES, :]


def _cproj_call(s, mods, w_cp, b_cp, qk_conv_w, *, layer):
    hb = TM // HALO
    n_hb = T_ALL // HALO
    return pl.pallas_call(
        _cproj_kernel,
        grid=(NT,),
        in_specs=[
            pl.BlockSpec((TM, D_MODEL), lambda i: (i, 0)),
            pl.BlockSpec((HALO, D_MODEL), lambda i: (jnp.maximum(i * hb - 1, 0), 0)),
            pl.BlockSpec((HALO, D_MODEL), lambda i: (jnp.minimum((i + 1) * hb, n_hb - 1), 0)),
            pl.BlockSpec((None, 9, D_MODEL), lambda i: (layer * MOD_ROWS + _mod_row(i), 0, 0)),
            _const_spec((None, D_MODEL, CP_COLS), (layer, 0, 0)),
            _const_spec((None, 1, CP_COLS), (layer, 0, 0)),
            _const_spec((None, QK_CONV, 2 * MIX_W), (layer, 0, 0)),
        ],
        out_specs=[
            pl.BlockSpec((TM, MIX_W), lambda i: (i, 0)),
            pl.BlockSpec((TM * MLSTM_HEADS, MLSTM_DH), lambda i: (i, 0)),
            pl.BlockSpec((TM, MIX_W), lambda i: (i, 0)),
            pl.BlockSpec((TM, LANES), lambda i: (i, 0)),
            pl.BlockSpec((CH_T * SUBLANES, CHUNK), lambda i: (i, 0)),
        ],
        out_shape=[
            jax.ShapeDtypeStruct((T_ALL, MIX_W), BF16),
            jax.ShapeDtypeStruct((T_ALL * MLSTM_HEADS, MLSTM_DH), BF16),
            jax.ShapeDtypeStruct((T_ALL, MIX_W), BF16),
            jax.ShapeDtypeStruct((T_ALL, LANES), F32),
            jax.ShapeDtypeStruct((T_ALL // CHUNK * SUBLANES, CHUNK), F32),
        ],
        scratch_shapes=[
            pltpu.VMEM((TM + 2 * HALO, D_MODEL), BF16),
            pltpu.VMEM((TM + 2 * HALO, 2 * MIX_W), F32),
        ],
        compiler_params=_params(("parallel",)),
        name="cproj",
    )(s, s, s, mods, w_cp, b_cp, qk_conv_w)


def _mlstm_kernel(qf_ref, ktf_ref, vf_ref, gf_ref, gtf_ref, qb_ref, ktb_ref, vb_ref, gb_ref, gtb_ref,
                  hf_ref, hb_ref, c_ref, m_ref):
    @pl.when(pl.program_id(0) == 0)
    def _():
        c_ref[...] = jnp.zeros(c_ref.shape, F32)
        m_ref[...] = jnp.zeros(m_ref.shape, F32)

    row = lax.broadcasted_iota(jnp.int32, (CHUNK, CHUNK), 0)
    col = lax.broadcasted_iota(jnp.int32, (CHUNK, CHUNK), 1)
    ones_col = jnp.ones((CHUNK, MLSTM_DH), BF16)
    nh = MLSTM_HEADS
    dirs = ((qf_ref, ktf_ref, vf_ref, gf_ref, gtf_ref, hf_ref, col <= row, CHUNK - 1),
            (qb_ref, ktb_ref, vb_ref, gb_ref, gtb_ref, hb_ref, col >= row, 0))
    heads = []
    for d, (q_ref, kt_ref, v_ref, g_ref, gt_ref, h_ref, seen, end_row) in enumerate(dirs):
        for b in range(BATCH):
            g = g_ref[b]
            gt = gt_ref[b]
            for h in range(nh):
                lane = d * nh + h
                hd = slice(h * MLSTM_DH, (h + 1) * MLSTM_DH)
                heads.append(dict(
                    idx=(d * BATCH + b) * nh + h, seen=seen, end_row=end_row, out=(h_ref, b, hd), g=g, lane=lane,
                    cum=g[:, lane:lane + 1], rmax=g[:, GS_RMAX + lane:GS_RMAX + lane + 1], r=gt[lane:lane + 1, :],
                    q=q_ref[b, :, hd], kt=kt_ref[b, hd, :], v=v_ref[b, :, hd]))

    for hd in heads:
        hd["qk"] = _dot(hd["q"], hd["kt"])
        hd["m_prev"] = m_ref[hd["idx"], 0:1, 0:1]
        hd["c_old"] = c_ref[hd["idx"]]
        hd["v_aug"] = jnp.concatenate([hd["v"], ones_col], axis=1)
    for hd in heads:
        e = hd["end_row"]
        top = jnp.maximum(hd["m_prev"], hd["rmax"][e:e + 1, :])
        w_row = jnp.exp(hd["r"] - top)
        ktw = (hd["kt"].astype(F32) * w_row).astype(BF16)
        c_ref[hd["idx"]] = jnp.exp(hd["m_prev"] - top) * hd["c_old"] + _dot(ktw, hd["v_aug"])
        m_ref[hd["idx"]] = jnp.broadcast_to(hd["cum"][e:e + 1, :] + top, m_ref.shape[1:])
    lane_row = lax.broadcasted_iota(jnp.int32, (1, LANES), 1)
    for k0 in range(0, len(heads), nh):
        group = heads[k0:k0 + nh]
        g = group[0]["g"]
        m_row = jnp.zeros((1, LANES), F32)
        for hd in group:
            m_row = jnp.where(lane_row == GS_RMAX + hd["lane"], hd["m_prev"], m_row)
        top_all = jnp.maximum(m_row, g)
        w_all = jnp.exp(m_row - top_all)
        floor_all = jnp.exp(-(pltpu.roll(g, GS_RMAX, 1) + top_all))
        for hd in group:
            ln = GS_RMAX + hd["lane"]
            hd["top"], hd["w"], hd["floor"] = (t[:, ln:ln + 1] for t in (top_all, w_all, floor_all))
    for hd in heads:
        top = jnp.broadcast_to(hd["top"], (CHUNK, CHUNK))
        s = hd["qk"] * jnp.exp(jnp.where(hd["seen"], hd["r"] - top, -jnp.inf))
        qw = hd["q"].astype(F32) * hd["w"]
        lhs = jnp.concatenate([s.astype(BF16), qw.astype(BF16)], axis=1)
        rhs = jnp.concatenate([hd["v_aug"], hd["c_old"].astype(BF16)], axis=0)
        hd["na"] = _dot(lhs, rhs)
    for hd in heads:
        den = jnp.maximum(jnp.abs(hd["na"][:, MLSTM_DH:]), hd["floor"])
        h_ref, b, cols = hd["out"]
        h_ref[b, :, cols] = hd["na"][:, :MLSTM_DH] / den


def _mlstm_call(q, kt, v, g, gt):
    fwd = lambda s: s
    bwd = lambda s: jnp.where(s < CTX_CH, CTX_CH - 1 - s, NCH_B + CTX_CH - 1 - s)
    q3 = q.reshape(BATCH, SEQ_B, MIX_W)
    kt3 = kt.reshape(BATCH, SEQ_B * MLSTM_HEADS, MLSTM_DH)
    v3 = v.reshape(BATCH, SEQ_B, MIX_W)
    g3 = g.reshape(BATCH, SEQ_B, LANES)
    gt3 = gt.reshape(BATCH, NCH_B * SUBLANES, CHUNK)

    def specs(order):
        return [
            pl.BlockSpec((BATCH, CHUNK, MIX_W), lambda s: (0, order(s), 0)),
            pl.BlockSpec((BATCH, CHUNK * MLSTM_HEADS, MLSTM_DH), lambda s: (0, order(s), 0)),
            pl.BlockSpec((BATCH, CHUNK, MIX_W), lambda s: (0, order(s), 0)),
            pl.BlockSpec((BATCH, CHUNK, LANES), lambda s: (0, order(s), 0)),
            pl.BlockSpec((BATCH, SUBLANES, CHUNK), lambda s: (0, order(s), 0)),
        ]

    n_state = 2 * BATCH * MLSTM_HEADS
    hf, hb = pl.pallas_call(
        _mlstm_kernel,
        grid=(NCH_B,),
        in_specs=specs(fwd) + specs(bwd),
        out_specs=[
            pl.BlockSpec((BATCH, CHUNK, MIX_W), lambda s: (0, fwd(s), 0)),
            pl.BlockSpec((BATCH, CHUNK, MIX_W), lambda s: (0, bwd(s), 0)),
        ],
        out_shape=[jax.ShapeDtypeStruct((BATCH, SEQ_B, MIX_W), F32)] * 2,
        scratch_shapes=[
            pltpu.VMEM((n_state, MLSTM_DH, 2 * MLSTM_DH), F32),
            pltpu.VMEM((n_state, SUBLANES, LANES), F32),
        ],
        compiler_params=_params(("arbitrary",)),
        name="mlstm",
    )(q3, kt3, v3, g3, gt3, q3, kt3, v3, g3, gt3)
    return hf.reshape(T_ALL, MIX_W), hb.reshape(T_ALL, MIX_W)


VEC_GMLP_G, VEC_GMLP_B, VEC_CONV_B, VEC_CONV_G, VEC_CONV_LB, VEC_MLSTM_G, VEC_POOL_S = range(7)
VEC_ROWS = 8


def _mix_kernel(x_ref, prev_ref, next_ref, mod_ref, hf_ref, hb_ref, w_ref, b_ref, vec_ref, ws_ref, bst_ref,
                cw_ref, pw_ref, wbr_ref, wout_ref, g_ref, beta_ref, o_ref, xe_ref, a_ref, d_ref, ash_ref, conv_ref,
                gate_ref, pa_ref, po_ref, u_ref, vn_ref, yc_ref, yd_ref):
    j = pl.program_id(0) % NT_B
    is_ctx = j == 0
    first = jnp.logical_or(is_ctx, j == 1)
    last = jnp.logical_or(is_ctx, j == NT_B - 1)
    shift = mod_ref[3:4, :]
    scale = mod_ref[4:5, :]
    gate = mod_ref[5:6, :]

    def modulate(v):
        return (v * (1.0 + scale) + shift).astype(BF16)

    def vec(r):
        return vec_ref[r:r + 1, :]

    def proj(ext, lo, width, zero=None, bias_scale=None):
        lhs = xe_ref[...] if ext else xe_ref[HALO:HALO + TM, :]

        def bias(c):
            row = b_ref[:, c:c + MXU_N]
            row = row if bias_scale is None else bias_scale * row
            return row if zero is None else row + zero

        return jnp.concatenate([_dot(lhs, w_ref[:, c:c + MXU_N]) + bias(c)
                                for c in range(lo, lo + width, MXU_N)], axis=1)

    xe_ref[0:HALO, :] = modulate(prev_ref[...])
    xe_ref[HALO:HALO + TM, :] = modulate(x_ref[...])
    xe_ref[HALO + TM:2 * HALO + TM, :] = modulate(next_ref[...])
    rows = lax.broadcasted_iota(jnp.int32, (TM + 2 * HALO, 1), 0)
    valid = jnp.logical_and(jnp.logical_or(rows >= HALO, jnp.logical_not(first)),
                            jnp.logical_or(rows < HALO + TM, jnp.logical_not(last)))

    pb = proj(True, MX_B, 2 * MIX_W)
    a_ref[...] = jnp.where(valid, pb[:, :MIX_W] * _sigmoid(pb[:, MIX_W:]), 0.0)
    pa_ref[...] = proj(False, MX_A, 2 * MIX_W)

    gate_chunks = [(i, n) for i in range(N_BRANCH) for n in range(D_MODEL // MXU_N)]
    sh_rows = TM + 2 * HALO - SUBLANES

    def zero_of(v):
        bits = lax.bitcast_convert_type(v[0:1, 0:LANES], jnp.uint32)
        return lax.shift_right_logical(lax.shift_right_logical(bits, jnp.uint32(16)), jnp.uint32(16)).astype(F32)

    def conv_block(cb, rb, z):
        cols = slice(cb * LANES, (cb + 1) * LANES)
        part = None
        for k in range(CONV_W):
            q8, s = divmod(HALO - CONV_W // 2 + k, SUBLANES)
            r0 = q8 * SUBLANES + rb * CONV_RB
            src = a_ref[r0:r0 + CONV_RB, cols] if s == 0 else ash_ref[s - 1, r0:r0 + CONV_RB, cols]
            term = (cw_ref[k:k + 1, cols] + z) * src
            part = term if part is None else part + term
        conv_ref[rb * CONV_RB:(rb + 1) * CONV_RB, cols] = part
        return part

    conv_blocks = [(cb, rb) for cb in range(MIX_W // LANES) for rb in range(TM // CONV_RB)]
    per_step = len(conv_blocks) // len(gate_chunks)
    zero_row = jnp.zeros((1, LANES), F32)
    v_zero = [zero_row] * len(gate_chunks)
    for step, (i, n) in enumerate(gate_chunks):
        lo = i * D_MODEL + n * MXU_N
        z_in = jnp.concatenate([v_zero[step - CHAIN_LAG]] * (MXU_N // LANES), axis=1) if step >= CHAIN_LAG else None
        gv = jnp.tanh(proj(False, MX_G + lo, MXU_N, z_in, bias_scale=0.5))
        gate_ref[:, lo:lo + MXU_N] = gv
        z = zero_of(gv) + v_zero[step - 1]
        for cb, rb in conv_blocks[step * per_step:(step + 1) * per_step]:
            if rb == 0:
                cols = slice(cb * LANES, (cb + 1) * LANES)
                for s in range(1, SUBLANES):
                    ash_ref[s - 1, 0:sh_rows, cols] = a_ref[s:s + sh_rows, cols]
            z = zero_of(conv_block(cb, rb, z))
        v_zero[step] = z

    pa = _gelu_tanh(pa_ref[...])
    u_ref[...] = pa[:, :MIX_W]
    vn_ref[...] = _ln(pa[:, MIX_W:], vec(VEC_GMLP_G), vec(VEC_GMLP_B)).astype(BF16)

    d_ref[0:TM + 2 * HALO, :] = jnp.where(valid, proj(True, MX_D, MIX_W), 0.0)
    d_ref[TM + 2 * HALO:TM + 2 * HALO + SUBLANES, :] = jnp.zeros((SUBLANES, MIX_W), F32)
    pos = jnp.where(is_ctx, 0, (j - 1) * TM) + lax.broadcasted_iota(jnp.int32, (TM, 1), 0)
    n_seq = jnp.where(is_ctx, CTX_LEN, SEQ)

    def window_sum(cols, win):
        start = HALO - win // 2
        if win < SUBLANES:
            ws = d_ref[start:start + TM, cols]
            for k in range(1, win):
                ws = ws + d_ref[start + k:start + k + TM, cols]
            return ws
        base = start // SUBLANES * SUBLANES
        off = start - base
        up8 = lambda r: -(-r // SUBLANES) * SUBLANES
        rows = [up8(TM + off)]
        n = win
        while n > 1:
            n //= 2
            rows.append(up8(rows[-1] + n))
        rows.reverse()
        level = d_ref[base:base + rows[1], cols] + d_ref[base + 1:base + 1 + rows[1], cols]
        n, i = 2, 1
        while n < win:
            i += 1
            level = level[0:rows[i], :] + level[n:n + rows[i], :]
            n *= 2
        return level[off:off + TM, :]

    grp = lax.broadcasted_iota(jnp.int32, (1, LANES), 1)
    lo_row = jnp.zeros((1, LANES), jnp.int32)
    hi_row = jnp.zeros((1, LANES), jnp.int32)
    for gi, win in enumerate(POOL_WINDOWS):
        lo_row = jnp.where(grp == gi, win // 2, lo_row)
        hi_row = jnp.where(grp == gi, win - 1 - win // 2, hi_row)
    cnt_all = (jnp.minimum(pos + hi_row + 1, n_seq) - jnp.maximum(pos - lo_row, 0)).astype(F32)
    inv_cnt = 1.0 / cnt_all
    for gi, win in enumerate(POOL_WINDOWS):
        cols = slice(gi * POOL_GD, (gi + 1) * POOL_GD)
        wsum = window_sum(cols, win)
        diff = wsum * inv_cnt[:, gi:gi + 1] - d_ref[HALO:HALO + TM, cols]
        yd_ref[:, cols] = (_dot(diff.astype(BF16), pw_ref[gi]) * vec_ref[VEC_POOL_S:VEC_POOL_S + 1, cols]
                           ).astype(BF16)

    po_ref[...] = proj(False, MX_O, MIX_W)
    for h in range(MLSTM_HEADS):
        cols = slice(h * MLSTM_DH, (h + 1) * MLSTM_DH)
        hh = hf_ref[:, cols] + hb_ref[:, cols]
        mu = jnp.mean(hh, axis=-1, keepdims=True)
        hc = hh - mu
        hn = hc * lax.rsqrt(jnp.mean(hc * hc, axis=-1, keepdims=True) + LN_EPS)
        yc_ref[:, cols] = (_sigmoid(po_ref[:, cols]) * (hn * vec_ref[VEC_MLSTM_G:VEC_MLSTM_G + 1, cols])
                           ).astype(BF16)

    gd = MIX_W // GMLP_GROUPS
    z_rows = []
    for c in range(CH_T):
        z_cols = []
        for gi in range(GMLP_GROUPS):
            blk = vn_ref[c * CHUNK:(c + 1) * CHUNK, gi * gd:(gi + 1) * gd]
            z_cols.append(_dot(ws_ref[gi], blk) + bst_ref[:, gi:gi + 1])
        z_rows.append(jnp.concatenate(z_cols, axis=1))
    ya = (u_ref[...] * jnp.concatenate(z_rows, axis=0)).astype(BF16)
    yb = _silu(_ln(conv_ref[...] + vec(VEC_CONV_B), vec(VEC_CONV_G), vec(VEC_CONV_LB))).astype(BF16)
    yc = yc_ref[...]
    yd = yd_ref[...]

    merged = []
    for n in range(D_MODEL // MXU_N):
        cols = slice(n * MXU_N, (n + 1) * MXU_N)
        acc = None
        for i, yi in enumerate((ya, yb, yc, yd)):
            lo = i * D_MODEL + n * MXU_N
            p = _dot(yi, wbr_ref[i, :, cols])
            term = gate_ref[:, lo:lo + MXU_N] * p + p
            acc = term if acc is None else acc + term
        merged.append((0.5 * acc).astype(BF16))
    merged = jnp.concatenate(merged, axis=1)
    y = jnp.concatenate([_dot(merged, wout_ref[:, n * MXU_N:(n + 1) * MXU_N]) for n in range(D_MODEL // MXU_N)],
                        axis=1)
    o_ref[...] = _ln(ALPHA * x_ref[...] + gate * y, g_ref[...], beta_ref[...])


def _mix_call(s, mods, hf, hb, w_mx, b_mx, vec512, gmlp_ws, gmlp_bst, conv_w, pool_w, w_branch, w_out,
              ln_g, ln_b, *, layer):
    hb_per_tile = TM // HALO
    n_hb = T_ALL // HALO
    return pl.pallas_call(
        _mix_kernel,
        grid=(NT,),
        in_specs=[
            pl.BlockSpec((TM, D_MODEL), lambda i: (i, 0)),
            pl.BlockSpec((HALO, D_MODEL), lambda i: (jnp.maximum(i * hb_per_tile - 1, 0), 0)),
            pl.BlockSpec((HALO, D_MODEL), lambda i: (jnp.minimum((i + 1) * hb_per_tile, n_hb - 1), 0)),
            pl.BlockSpec((None, 9, D_MODEL), lambda i: (layer * MOD_ROWS + _mod_row(i), 0, 0)),
            pl.BlockSpec((TM, MIX_W), lambda i: (i, 0)),
            pl.BlockSpec((TM, MIX_W), lambda i: (i, 0)),
            _const_spec((None, D_MODEL, MX_W), (layer, 0, 0)),
            _const_spec((None, 1, MX_COLS), (layer, 0, 0)),
            _const_spec((None, VEC_ROWS, MIX_W), (layer, 0, 0)),
            _const_spec((None, GMLP_GROUPS, CHUNK, CHUNK), (layer, 0, 0, 0)),
            _const_spec((None, CHUNK, GMLP_GROUPS), (layer, 0, 0)),
            _const_spec((None, CONV_W, MIX_W), (layer, 0, 0)),
            _const_spec((None, len(POOL_WINDOWS), POOL_GD, POOL_GD), (layer, 0, 0, 0)),
            _const_spec((None, N_BRANCH, MIX_W, D_MODEL + LANE_PAD), (layer, 0, 0, 0)),
            _const_spec((None, D_MODEL, D_MODEL + LANE_PAD), (layer, 0, 0)),
            _const_spec((None, 1, D_MODEL), (layer * 3 + 1, 0, 0)),
            _const_spec((None, 1, D_MODEL), (layer * 3 + 1, 0, 0)),
        ],
        out_specs=pl.BlockSpec((TM, D_MODEL), lambda i: (i, 0)),
        out_shape=jax.ShapeDtypeStruct((T_ALL, D_MODEL), F32),
        scratch_shapes=[
            pltpu.VMEM((TM + 2 * HALO, D_MODEL), BF16),
            pltpu.VMEM((TM + 2 * HALO, MIX_W), F32),
            pltpu.VMEM((TM + 2 * HALO + SUBLANES, MIX_W), F32),
            pltpu.VMEM((SUBLANES - 1, TM + 2 * HALO, MIX_W), F32),
            pltpu.VMEM((TM, MIX_W), F32),
            pltpu.VMEM((TM, N_BRANCH * D_MODEL), F32),
            pltpu.VMEM((TM, 2 * MIX_W), F32),
            pltpu.VMEM((TM, MIX_W), F32),
            pltpu.VMEM((TM, MIX_W), F32),
            pltpu.VMEM((TM, MIX_W), BF16),
            pltpu.VMEM((TM, MIX_W), BF16),
            pltpu.VMEM((TM, MIX_W), BF16),
        ],
        compiler_params=_params(("parallel",)),
        name="mix",
    )(s, s, s, mods, hf, hb, w_mx, b_mx, vec512, gmlp_ws, gmlp_bst, conv_w, pool_w, w_branch, w_out, ln_g, ln_b)


MX_TILE = 512
MX_PAD = MX_TILE
MX_W = MX_COLS + MX_PAD
CP_TILE = 256


def _mx_src(j):
    a, o, d = (MX_O - MX_A) // MX_TILE, (MX_D - MX_A) // MX_TILE, (MX_G - MX_A) // MX_TILE
    g_end = MX_COLS // MX_TILE
    return jnp.where(j < a, OFF_A + MX_TILE * j,
                     jnp.where(j < o, OFF_C_O + MX_TILE * (j - a),
                               jnp.where(j < d, OFF_D + MX_TILE * (j - o),
                                         jnp.where(j < g_end, OFF_G + MX_TILE * (j - d), 0))))


def _prep_mx_kernel(wt_ref, o_ref):
    j = pl.program_id(1)

    @pl.when(j < MX_G // MX_TILE)
    def _():
        o_ref[...] = wt_ref[0].T.astype(BF16)

    @pl.when(jnp.logical_and(j >= MX_G // MX_TILE, j < MX_COLS // MX_TILE))
    def _():
        o_ref[...] = (0.5 * wt_ref[0].T).astype(BF16)

    @pl.when(j >= MX_COLS // MX_TILE)
    def _():
        o_ref[...] = jnp.zeros(o_ref.shape, BF16)


def _prep_cp_kernel(wt_ref, o_ref):
    j = pl.program_id(1)
    n_plain = (CP_GI - CP_QK) // CP_TILE

    @pl.when(j < n_plain)
    def _():
        o_ref[...] = wt_ref[0].T.astype(BF16)

    @pl.when(j == n_plain)
    def _():
        nh = MLSTM_HEADS
        t = wt_ref[0, 0:LANES, :].T
        lane = lax.broadcasted_iota(jnp.int32, t.shape, 1)
        up1 = pltpu.roll(t, LANES - nh, 1)
        up2 = pltpu.roll(t, LANES - 2 * nh, 1)
        gi = jnp.where(lane < nh, t, jnp.where(lane < 2 * nh, up1, 0.0))
        gf = jnp.where(lane < nh, up1, jnp.where(lane < 2 * nh, up2, 0.0))
        o_ref[:, 0:LANES] = gi.astype(BF16)
        o_ref[:, LANES:2 * LANES] = gf.astype(BF16)


def _prep_win_call(w_in):
    wt = jnp.swapaxes(w_in, 1, 2)
    n_plain = (CP_GI - CP_QK) // CP_TILE
    w_mx = pl.pallas_call(
        _prep_mx_kernel,
        grid=(DEPTH, MX_W // MX_TILE),
        in_specs=[pl.BlockSpec((pl.Element(1), pl.Element(MX_TILE), pl.Element(D_MODEL)),
                               lambda l, j: (l, pl.multiple_of(_mx_src(j), SUBLANES), 0))],
        out_specs=pl.BlockSpec((None, D_MODEL, MX_TILE), lambda l, j: (l, 0, j)),
        out_shape=jax.ShapeDtypeStruct((DEPTH, D_MODEL, MX_W), BF16),
        compiler_params=_params(("parallel", "parallel")),
        name="prep_mx",
    )(wt)
    w_cp = pl.pallas_call(
        _prep_cp_kernel,
        grid=(DEPTH, CP_COLS // CP_TILE),
        in_specs=[pl.BlockSpec((pl.Element(1), pl.Element(CP_TILE), pl.Element(D_MODEL)),
                               lambda l, j: (l, pl.multiple_of(
                                   jnp.where(j < n_plain, OFF_C + CP_TILE * j, OFF_C_GATES), SUBLANES), 0))],
        out_specs=pl.BlockSpec((None, D_MODEL, CP_TILE), lambda l, j: (l, 0, j)),
        out_shape=jax.ShapeDtypeStruct((DEPTH, D_MODEL, CP_COLS), BF16),
        compiler_params=_params(("parallel", "parallel")),
        name="prep_cp",
    )(wt)
    return w_mx, w_cp


def _pad_cast_kernel(w_ref, o_ref):
    cols = w_ref.shape[1]
    o_ref[:, 0:cols] = w_ref[...].astype(BF16)
    o_ref[:, cols:] = jnp.zeros((o_ref.shape[0], o_ref.shape[1] - cols), BF16)


def _pad_cast_call(w):
    lead, cols = w.shape[:-1], w.shape[-1]
    rows = math.prod(lead)
    rb = 512
    out = pl.pallas_call(
        _pad_cast_kernel,
        grid=(rows // rb,),
        in_specs=[pl.BlockSpec((rb, cols), lambda i: (i, 0))],
        out_specs=pl.BlockSpec((rb, cols + LANE_PAD), lambda i: (i, 0)),
        out_shape=jax.ShapeDtypeStruct((rows, cols + LANE_PAD), BF16),
        compiler_params=_params(("parallel",)),
        name="pad_cast",
    )(w.reshape(rows, cols))
    return out.reshape(lead + (cols + LANE_PAD,))


def _repack_cols(w, pieces):
    parts = [jnp.zeros(w.shape[:-1] + (p,), w.dtype) if isinstance(p, int) else w[..., p[0]:p[1]] for p in pieces]
    return jnp.concatenate(parts, axis=-1)


def kernel(x, c, ctx, c_ctx, w_ada, b_ada, ln_g, ln_b, ffn_w_in, ffn_w_out, w_in, b_in, gmlp_ln_g, gmlp_ln_b,
           gmlp_ws, gmlp_bs, conv_w, conv_b, conv_ln_g, conv_ln_b, qk_conv_w, mlstm_ln_g, pool_w, pool_scale,
           w_branch, w_out):
    mix_pieces = [(OFF_A, OFF_C), (OFF_C_O, OFF_D), (OFF_D, OFF_G), (OFF_G, IN_COLS)]
    nh = MLSTM_HEADS
    gate_pad = LANES - 2 * nh

    def cproj_cols(w):
        g = w[..., OFF_C_GATES:OFF_C_O].reshape(w.shape[:-1] + (4, nh))
        pad = jnp.zeros(w.shape[:-1] + (gate_pad,), w.dtype)
        return jnp.concatenate([w[..., OFF_C:OFF_C_GATES], g[..., 0, :], g[..., 2, :], pad,
                                g[..., 1, :], g[..., 3, :], pad], axis=-1)
    w_mx, w_cp = _prep_win_call(w_in)
    b_mx = _repack_cols(b_in, mix_pieces).reshape(DEPTH, 1, MX_COLS)
    b_cp = cproj_cols(b_in).reshape(DEPTH, 1, CP_COLS)
    ffn_w_in_b = ffn_w_in.astype(BF16)
    ffn_w_out_b = _pad_cast_call(ffn_w_out)
    w_branch_b = _pad_cast_call(w_branch)
    w_out_b = _pad_cast_call(w_out)
    gmlp_ws_b = gmlp_ws.astype(BF16)
    pool_w_b = pool_w.astype(BF16)
    gmlp_bst = jnp.swapaxes(gmlp_bs, 1, 2)
    vec512 = jnp.stack([gmlp_ln_g, gmlp_ln_b, conv_b, conv_ln_g, conv_ln_b, mlstm_ln_g, pool_scale,
                        jnp.zeros_like(pool_scale)], axis=1)
    ln_g3 = ln_g.reshape(DEPTH * 3, 1, D_MODEL)
    ln_b3 = ln_b.reshape(DEPTH * 3, 1, D_MODEL)
    c_t = jnp.concatenate([c, c_ctx[None], jnp.zeros((MOD_ROWS - BATCH - 1, D_MODEL), F32)], axis=0).T

    mods = _mods_call(c_t, w_ada, b_ada).reshape(DEPTH * MOD_ROWS, 9, D_MODEL)
    s = _embed_call(x, ctx)
    for l in range(DEPTH):
        last = l == DEPTH - 1
        s = _ffn_call(s, mods, ffn_w_in_b, ffn_w_out_b, ln_g3, ln_b3, layer=l, which=0, final=False)
        q, kt, v, g, gt = _cproj_call(s, mods, w_cp, b_cp, qk_conv_w, layer=l)
        hf, hb = _mlstm_call(q, kt, v, g, gt)
        s = _mix_call(s, mods, hf, hb, w_mx, b_mx, vec512, gmlp_ws_b, gmlp_bst, conv_w, pool_w_b, w_branch_b,
                      w_out_b, ln_g3, ln_b3, layer=l)
        s = _ffn_call(s, mods, ffn_w_in_b, ffn_w_out_b, ln_g3, ln_b3, layer=l, which=1, final=last)
    return s.reshape(BATCH, SEQ, D_MODEL)
```
